```python
import math
import jax, jax.numpy as jnp
from jax import lax
import numpy as np

D_MODEL = 1024
BATCH = 8
SEQ = 4096
DEPTH = 1

MEM_LEN = 256
ROPE_THETA = 10000.0
Q_BLOCK = 128
LN_EPS = 1e-5
DEEPNORM_ALPHA = (2 * DEPTH) ** 0.25
DEEPNORM_BETA = (8 * DEPTH) ** -0.25

DA_HEADS = 4
DA_HEAD_DIM = 64
DA_WIDTH = DA_HEADS * 2 * DA_HEAD_DIM
DSA_HEADS = 8
DSA_HEAD_DIM = 64
DSA_WIDTH = DSA_HEADS * DSA_HEAD_DIM
IDX_HEADS = 8
IDX_DIM = 64
DSA_TOPK_MAX = 256
MEM_HEADS = 4
MEM_HEAD_DIM = 128
MEM_WIDTH = MEM_HEADS * MEM_HEAD_DIM
N_BRANCHES = 3

SPLIT_SIZES = (DA_WIDTH, DA_WIDTH, DA_WIDTH,
               DSA_WIDTH, DSA_HEAD_DIM, DSA_HEAD_DIM,
               IDX_HEADS * IDX_DIM, IDX_DIM, IDX_HEADS,
               MEM_WIDTH, N_BRANCHES * D_MODEL)
IN_WIDTH = sum(SPLIT_SIZES)

N_EXPERTS = 256
TOP_K = 8
N_GROUPS = 8
TOPK_GROUPS = 4
EXPERT_FF = 256
SHARED_FF = 256
ROUTED_SCALE = 2.5
MOE_BLOCK = 128

kernel_name = 'hybrid_diffattn_dsa_memxattn_moe_deepnorm'


def layer_norm(x, g, b):
    xf = x.astype(jnp.float32)
    mu = jnp.mean(xf, -1, keepdims=True)
    var = jnp.mean(jnp.square(xf - mu), -1, keepdims=True)
    y = (xf - mu) * lax.rsqrt(var + LN_EPS) * g.astype(jnp.float32) + b.astype(jnp.float32)
    return y.astype(x.dtype)


def rms_norm(x, g):
    xf = x.astype(jnp.float32)
    y = xf * lax.rsqrt(jnp.mean(jnp.square(xf), -1, keepdims=True) + LN_EPS) * g.astype(jnp.float32)
    return y.astype(x.dtype)


def rope_tables(seq, dim):
    inv = 1.0 / (ROPE_THETA ** (jnp.arange(0, dim, 2, dtype=jnp.float32) / dim))
    ang = jnp.arange(seq, dtype=jnp.float32)[:, None] * inv[None, :]
    ang = jnp.concatenate([ang, ang], -1)
    return jnp.cos(ang), jnp.sin(ang)


def apply_rope(t, cos, sin):
    shape = (1, t.shape[1]) + (1,) * (t.ndim - 3) + (t.shape[-1],)
    c = cos.reshape(shape)
    s = sin.reshape(shape)
    tf = t.astype(jnp.float32)
    half = t.shape[-1] // 2
    rot = jnp.concatenate([-tf[..., half:], tf[..., :half]], -1)
    return (tf * c + rot * s).astype(t.dtype)


def to_blocks(t):
    b, s = t.shape[:2]
    return t.reshape((b, s // Q_BLOCK, Q_BLOCK) + t.shape[2:]).swapaxes(0, 1)


def from_blocks(t):
    nb, b, qb = t.shape[:3]
    return t.swapaxes(0, 1).reshape((b, nb * qb) + t.shape[3:])


def differential_attention(q, k, v, lam):
    S = q.shape[1]
    scale = q.shape[-1] ** -0.5
    kpos = jnp.arange(S)

    def block(args):
        qb, bi = args
        qpos = bi * Q_BLOCK + jnp.arange(Q_BLOCK)
        causal = kpos[None, :] <= qpos[:, None]
        s = jnp.einsum('bqhmd,bkhmd->bhmqk', qb, k).astype(jnp.float32) * scale
        p = jax.nn.softmax(jnp.where(causal, s, -jnp.inf), axis=-1)
        a = p[:, :, 0] - lam * p[:, :, 1]
        return jnp.einsum('bhqk,bkhe->bqhe', a.astype(v.dtype), v)

    out = lax.map(block, (to_blocks(q), jnp.arange(S // Q_BLOCK)))
    return from_blocks(out)


def indexer_sparse_attention(q, k, v, iq, ik, iw):
    S = q.shape[1]
    topk = min(DSA_TOPK_MAX, S // 4)
    scale = q.shape[-1] ** -0.5
    kpos = jnp.arange(S)

    def block(args):
        qb, iqb, iwb, bi = args
        qpos = bi * Q_BLOCK + jnp.arange(Q_BLOCK)
        causal = kpos[None, :] <= qpos[:, None]
        logits = jnp.einsum('bqjd,bsd->bqjs', iqb, ik).astype(jnp.float32)
        score = jnp.einsum('bqj,bqjs->bqs', iwb.astype(jnp.float32), jax.nn.relu(logits))
        score = jnp.where(causal, score, -jnp.inf)
        _, idx = lax.top_k(score, topk)
        valid = idx <= qpos[None, :, None]
        ks = jax.vmap(lambda kk, ii: kk[ii])(k, idx)
        vs = jax.vmap(lambda vv, ii: vv[ii])(v, idx)
        s = jnp.einsum('bqhd,bqnd->bqhn', qb, ks).astype(jnp.float32) * scale
        p = jax.nn.softmax(jnp.where(valid[:, :, None, :], s, -jnp.inf), axis=-1)
        return jnp.einsum('bqhn,bqnd->bqhd', p.astype(vs.dtype), vs)

    out = lax.map(block, (to_blocks(q), to_blocks(iq), to_blocks(iw), jnp.arange(S // Q_BLOCK)))
    return from_blocks(out)


def memory_attention(q, mk, mv):
    scale = q.shape[-1] ** -0.5
    s = jnp.einsum('bshe,bmhe->bhsm', q, mk).astype(jnp.float32) * scale
    p = jax.nn.softmax(s, axis=-1)
    return jnp.einsum('bhsm,bmhe->bshe', p.astype(mv.dtype), mv)


def token_mixer(h, mem, w_in, lq1, lk1, lq2, lk2, subln_g, w_mem_kv, w_ba, w_bb, w_bc, w_o, lam_init):
    B, S, _ = h.shape
    proj = h @ w_in
    cuts = np.cumsum(SPLIT_SIZES)[:-1].tolist()
    da_q, da_k, da_v, ds_q, ds_k, ds_v, ix_q, ix_k, ix_w, mem_q, gates = jnp.split(proj, cuts, axis=-1)

    cos_a, sin_a = rope_tables(S, DA_HEAD_DIM)
    da_q = apply_rope(da_q.reshape(B, S, DA_HEADS, 2, DA_HEAD_DIM), cos_a, sin_a)
    da_k = apply_rope(da_k.reshape(B, S, DA_HEADS, 2, DA_HEAD_DIM), cos_a, sin_a)
    da_v = da_v.reshape(B, S, DA_HEADS, 2 * DA_HEAD_DIM)
    f32 = jnp.float32
    lam = (jnp.exp(jnp.sum(lq1.astype(f32) * lk1.astype(f32)))
           - jnp.exp(jnp.sum(lq2.astype(f32) * lk2.astype(f32))) + lam_init)
    y_a = differential_attention(da_q, da_k, da_v, lam)
    y_a = rms_norm(y_a, subln_g) * (1.0 - lam_init)

    cos_b, sin_b = rope_tables(S, DSA_HEAD_DIM)
    cos_i, sin_i = rope_tables(S, IDX_DIM)
    ds_q = apply_rope(ds_q.reshape(B, S, DSA_HEADS, DSA_HEAD_DIM), cos_b, sin_b)
    ds_k = apply_rope(ds_k, cos_b, sin_b)
    ix_q = apply_rope(ix_q.reshape(B, S, IDX_HEADS, IDX_DIM), cos_i, sin_i)
    ix_k = apply_rope(ix_k, cos_i, sin_i)
    y_b = indexer_sparse_attention(ds_q, ds_k, ds_v, ix_q, ix_k, ix_w)

    mkv = (mem @ w_mem_kv).reshape(B, mem.shape[1], 2, MEM_HEADS, MEM_HEAD_DIM)
    y_c = memory_attention(mem_q.reshape(B, S, MEM_HEADS, MEM_HEAD_DIM), mkv[:, :, 0], mkv[:, :, 1])

    g = jax.nn.sigmoid(gates.astype(f32)).astype(h.dtype).reshape(B, S, N_BRANCHES, D_MODEL)
    merged = (g[:, :, 0] * (y_a.reshape(B, S, DA_WIDTH) @ w_ba)
              + g[:, :, 1] * (y_b.reshape(B, S, DSA_WIDTH) @ w_bb)
              + g[:, :, 2] * (y_c.reshape(B, S, MEM_WIDTH) @ w_bc))
    return merged @ w_o


def route(xt, w_router, bias):
    T = xt.shape[0]
    scores = jax.nn.sigmoid((xt @ w_router).astype(jnp.float32))
    biased = scores + bias.astype(jnp.float32)
    grouped = biased.reshape(T, N_GROUPS, N_EXPERTS // N_GROUPS)
    group_score = jnp.sum(lax.top_k(grouped, 2)[0], -1)
    _, gidx = lax.top_k(group_score, TOPK_GROUPS)
    gmask = jnp.any(gidx[:, :, None] == jnp.arange(N_GROUPS)[None, None, :], axis=1)
    emask = jnp.repeat(gmask, N_EXPERTS // N_GROUPS, axis=1)
    _, eidx = lax.top_k(jnp.where(emask, biased, -jnp.inf), TOP_K)
    w = jnp.take_along_axis(scores, eidx, axis=-1)
    w = w / jnp.sum(w, -1, keepdims=True) * ROUTED_SCALE
    return eidx, w


def routed_experts(xt, eidx, gate, w_eg, w_eu, w_ed):
    T, D = xt.shape
    A = T * TOP_K
    flat_e = eidx.reshape(A)
    flat_tok = jnp.repeat(jnp.arange(T, dtype=jnp.int32), TOP_K)
    flat_w = gate.reshape(A)
    order = jnp.argsort(flat_e)
    se = flat_e[order]
    counts = jnp.zeros((N_EXPERTS,), jnp.int32).at[flat_e].add(1)
    padded = (counts + MOE_BLOCK - 1) // MOE_BLOCK * MOE_BLOCK
    start = jnp.cumsum(counts) - counts
    pend = jnp.cumsum(padded)
    pstart = pend - padded
    dest = pstart[se] + jnp.arange(A, dtype=jnp.int32) - start[se]
    n_blocks = (A + N_EXPERTS * (MOE_BLOCK - 1) + MOE_BLOCK - 1) // MOE_BLOCK
    P = n_blocks * MOE_BLOCK
    row_tok = jnp.zeros((P,), jnp.int32).at[dest].set(flat_tok[order])
    row_w = jnp.zeros((P,), xt.dtype).at[dest].set(flat_w[order].astype(xt.dtype))
    blk_start = jnp.arange(n_blocks, dtype=jnp.int32) * MOE_BLOCK
    blk_e = jnp.minimum(jnp.searchsorted(pend, blk_start, side='right'), N_EXPERTS - 1)

    def step(acc, args):
        tok, wt, e = args
        xb = xt[tok]
        hb = jax.nn.silu(xb @ w_eg[e]) * (xb @ w_eu[e])
        yb = (hb @ w_ed[e]) * wt[:, None]
        return acc.at[tok].add(yb), None

    acc, _ = lax.scan(step, jnp.zeros_like(xt),
                      (row_tok.reshape(n_blocks, MOE_BLOCK), row_w.reshape(n_blocks, MOE_BLOCK), blk_e))
    return acc


def moe_ffn(h, w_router, router_bias, w_eg, w_eu, w_ed, w_sg, w_su, w_sd):
    B, S, D = h.shape
    xt = h.reshape(B * S, D)
    eidx, gate = route(xt, w_router, router_bias)
    routed = routed_experts(xt, eidx, gate, w_eg, w_eu, w_ed)
    shared = (jax.nn.silu(xt @ w_sg) * (xt @ w_su)) @ w_sd
    return (routed + shared).reshape(B, S, D)


def setup_inputs(seed: int = 0) -> dict:
    key = jax.random.key(seed)
    ks = jax.random.split(key, 25)
    f32 = jnp.float32
    L = DEPTH
    D = D_MODEL

    def nrm(k, shape, scale):
        return jax.random.normal(k, shape, f32) * scale

    return {
        'x': nrm(ks[0], (BATCH, SEQ, D), 1.0),
        'mem': nrm(ks[1], (BATCH, MEM_LEN, D), 1.0),
        'w_in': nrm(ks[2], (L, D, IN_WIDTH), D ** -0.5),
        'da_lambda_q1': nrm(ks[3], (L, DA_HEAD_DIM), 0.1),
        'da_lambda_k1': nrm(ks[4], (L, DA_HEAD_DIM), 0.1),
        'da_lambda_q2': nrm(ks[5], (L, DA_HEAD_DIM), 0.1),
        'da_lambda_k2': nrm(ks[6], (L, DA_HEAD_DIM), 0.1),
        'da_subln_g': 1.0 + nrm(ks[7], (L, 2 * DA_HEAD_DIM), 0.02),
        'w_mem_kv': nrm(ks[8], (L, D, 2 * MEM_WIDTH), D ** -0.5),
        'w_branch_a': nrm(ks[9], (L, DA_WIDTH, D), DA_WIDTH ** -0.5),
        'w_branch_b': nrm(ks[10], (L, DSA_WIDTH, D), DSA_WIDTH ** -0.5),
        'w_branch_c': nrm(ks[11], (L, MEM_WIDTH, D), MEM_WIDTH ** -0.5),
        'w_out': nrm(ks[12], (L, D, D), D ** -0.5 * DEEPNORM_BETA),
        'ln1_g': 1.0 + nrm(ks[13], (L, D), 0.02),
        'ln1_b': nrm(ks[14], (L, D), 0.02),
        'w_router': nrm(ks[15], (L, D, N_EXPERTS), D ** -0.5),
        'router_bias': nrm(ks[16], (L, N_EXPERTS), 0.01),
        'w_exp_gate': nrm(ks[17], (L, N_EXPERTS, D, EXPERT_FF), D ** -0.5),
        'w_exp_up': nrm(ks[18], (L, N_EXPERTS, D, EXPERT_FF), D ** -0.5),
        'w_exp_down': nrm(ks[19], (L, N_EXPERTS, EXPERT_FF, D), EXPERT_FF ** -0.5 * DEEPNORM_BETA),
        'w_sh_gate': nrm(ks[20], (L, D, SHARED_FF), D ** -0.5),
        'w_sh_up': nrm(ks[21], (L, D, SHARED_FF), D ** -0.5),
        'w_sh_down': nrm(ks[22], (L, SHARED_FF, D), SHARED_FF ** -0.5 * DEEPNORM_BETA),
        'ln2_g': 1.0 + nrm(ks[23], (L, D), 0.02),
        'ln2_b': nrm(ks[24], (L, D), 0.02),
    }


def reference(x, mem, w_in, da_lambda_q1, da_lambda_k1, da_lambda_q2, da_lambda_k2, da_subln_g,
              w_mem_kv, w_branch_a, w_branch_b, w_branch_c, w_out, ln1_g, ln1_b,
              w_router, router_bias, w_exp_gate, w_exp_up, w_exp_down,
              w_sh_gate, w_sh_up, w_sh_down, ln2_g, ln2_b):
    h = x
    for l in range(DEPTH):
        lam_init = 0.8 - 0.6 * math.exp(-0.3 * l)
        mix = token_mixer(h, mem, w_in[l], da_lambda_q1[l], da_lambda_k1[l], da_lambda_q2[l],
                          da_lambda_k2[l], da_subln_g[l], w_mem_kv[l], w_branch_a[l],
                          w_branch_b[l], w_branch_c[l], w_out[l], lam_init)
        h = layer_norm(DEEPNORM_ALPHA * h + mix, ln1_g[l], ln1_b[l])
        ffn = moe_ffn(h, w_router[l], router_bias[l], w_exp_gate[l], w_exp_up[l], w_exp_down[l],
                      w_sh_gate[l], w_sh_up[l], w_sh_down[l])
        h = layer_norm(DEEPNORM_ALPHA * h + ffn, ln2_g[l], ln2_b[l])
    return h
```

```python
import functools
import math

import jax
import jax.numpy as jnp
import numpy as np
from jax import lax
from jax.experimental import pallas as pl
from jax.experimental.pallas import tpu as pltpu

F32 = jnp.float32
BF16 = jnp.bfloat16
I32 = jnp.int32

LANES = 128
ROPE_THETA = 10000.0
LN_EPS = 1e-5

DA_HEADS = 4
DA_HEAD_DIM = 64
DSA_HEADS = 8
DSA_HEAD_DIM = 64
IDX_HEADS = 8
DSA_TOPK_MAX = 256
MEM_HEADS = 4
MEM_HEAD_DIM = 128
N_BRANCHES = 3

N_EXPERTS = 256
TOP_K = 8
N_GROUPS = 8
TOPK_GROUPS = 4
GROUP_SIZE = N_EXPERTS // N_GROUPS
ROUTED_SCALE = 2.5

NEG_BIG = -1e30
INT_MIN = -(2 ** 31)

VMEM_LIMIT = 56 * 1024 * 1024


def _cparams(sem):
    return pltpu.CompilerParams(dimension_semantics=sem, vmem_limit_bytes=VMEM_LIMIT)


def _dot(a, b):
    return jnp.dot(a, b, preferred_element_type=F32)


def _dot_nt(a, b):
    return lax.dot_general(a, b, (((1,), (1,)), ((), ())), preferred_element_type=F32)


def _layer_norm(x, g, b):
    mu = jnp.mean(x, -1, keepdims=True)
    xc = x - mu
    var = jnp.mean(xc * xc, -1, keepdims=True)
    return xc * lax.rsqrt(var + LN_EPS) * g + b


def _inproj_kernel(x_ref, wr_ref, wn_ref, tab_ref,
                   daq_ref, dak_ref, dsq_ref, ixq_ref, kv_ref, vk_ref, ikk_ref,
                   dav_ref, memq_ref, ixw_ref):
    xb = x_ref[...].astype(BF16)

    def rope(t, kind):
        c = tab_ref[3 * kind]
        sa = tab_ref[3 * kind + 1]
        sb = tab_ref[3 * kind + 2]
        return t * c + pltpu.roll(t, 96, 1) * sa + pltpu.roll(t, 32, 1) * sb

    groups = ((daq_ref, 0.125), (dak_ref, 1.0), (dsq_ref, 0.125), (ixq_ref, 1.0))
    for gi, (ref, scale) in enumerate(groups):
        y = _dot(xb, wr_ref[:, gi * 512:(gi + 1) * 512])
        for c in range(4):
            r = rope(y[:, c * LANES:(c + 1) * LANES], 0)
            if scale != 1.0:
                r = r * scale
            ref[:, c * LANES:(c + 1) * LANES] = r.astype(BF16)
    y = _dot(xb, wr_ref[:, 2048:2432])
    kv_ref[...] = rope(y[:, 0:128], 1).astype(BF16)
    vk_ref[...] = rope(y[:, 128:256], 2).astype(BF16)
    ikk_ref[...] = rope(y[:, 256:384], 0).astype(BF16)
    y = _dot(xb, wn_ref[...])
    dav_ref[...] = y[:, 0:512].astype(BF16)
    memq_ref[...] = y[:, 512:1024].astype(BF16)
    ixw_ref[...] = y[:, 1024:1152]


def _rope_tables(seq):
    dim = 64
    inv = 1.0 / (ROPE_THETA ** (jnp.arange(0, dim, 2, dtype=F32) / dim))
    ang = jnp.arange(seq, dtype=F32)[:, None] * inv[None, :]
    ang = jnp.concatenate([ang, ang], -1)
    cos, sin = jnp.cos(ang), jnp.sin(ang)
    local = jnp.arange(dim)[None, :]
    sa = jnp.where(local < dim // 2, -sin, 0.0)
    sb = jnp.where(local >= dim // 2, sin, 0.0)
    one, zero = jnp.ones_like(cos), jnp.zeros_like(cos)
    cat = lambda a, b: jnp.concatenate([a, b], -1)
    return jnp.stack([cat(cos, cos), cat(sa, sa), cat(sb, sb),
                      cat(cos, one), cat(sa, zero), cat(sb, zero),
                      cat(one, cos), cat(zero, sa), cat(zero, sb)], 0)


def _in_projection(x2d, w_in, seq, tm):
    T, D = x2d.shape
    o = np.cumsum([0, 512, 512, 512, 512, 64, 64, 512, 64, 8, 512]).tolist()
    col = lambda i: w_in[:, o[i]:o[i + 1]]
    da_q, da_k, da_v, ds_q, ds_k, ds_v, ix_q, ix_k, ix_w, mem_q = [col(i) for i in range(10)]
    ds_q = ds_q.reshape(D, DSA_HEADS // 2, 2, DSA_HEAD_DIM)[:, :, ::-1, :].reshape(D, 512)
    w_rope = jnp.concatenate([da_q, da_k, ds_q, ix_q, ds_k, ds_v, ds_v, ds_k, ix_k, ix_k], 1).astype(BF16)
    w_plain = jnp.concatenate([da_v, mem_q, ix_w, jnp.zeros((D, LANES - IDX_HEADS), w_in.dtype)], 1).astype(BF16)
    tabs = _rope_tables(seq)
    nseq = seq // tm
    bf = lambda n: jax.ShapeDtypeStruct((T, n), BF16)
    row = lambda n: pl.BlockSpec((tm, n), lambda i: (i, 0))
    return pl.pallas_call(
        _inproj_kernel,
        grid=(T // tm,),
        in_specs=[row(D),
                  pl.BlockSpec(w_rope.shape, lambda i: (0, 0)),
                  pl.BlockSpec(w_plain.shape, lambda i: (0, 0)),
                  pl.BlockSpec((9, tm, LANES), lambda i: (0, i % nseq, 0))],
        out_specs=[row(512), row(512), row(512), row(512), row(128), row(128), row(128),
                   row(512), row(512), row(128)],
        out_shape=[bf(512), bf(512), bf(512), bf(512), bf(128), bf(128), bf(128),
                   bf(512), bf(512), jax.ShapeDtypeStruct((T, LANES), F32)],
        compiler_params=_cparams(("parallel",)),
        name="in_projection",
    )(x2d, w_rope, w_plain, tabs)


def _diffattn_kernel(lam_ref, q_ref, k_ref, v_ref, g_ref, o_ref, *, tq, lam_init):
    i = pl.program_id(2)
    lam = lam_ref[0]
    q = q_ref[...]
    lane = lax.broadcasted_iota(I32, q.shape, 1)
    zero = jnp.zeros_like(q)
    qs = (jnp.where(lane < DA_HEAD_DIM, q, zero), jnp.where(lane >= DA_HEAD_DIM, q, zero))
    row = lax.broadcasted_iota(I32, (tq, tq), 0)
    col = lax.broadcasted_iota(I32, (tq, tq), 1)

    def step(j, carry, diagonal):
        off = pl.multiple_of(j * tq, tq)
        kj = k_ref[pl.ds(off, tq), :]
        vj = v_ref[pl.ds(off, tq), :]
        out = []
        for mp in range(2):
            m, l, a = carry[3 * mp:3 * mp + 3]
            s = _dot_nt(qs[mp], kj)
            if diagonal:
                s = jnp.where(col <= row, s, -jnp.inf)
            mn = jnp.maximum(m, jnp.max(s, -1, keepdims=True))
            p = jnp.exp(s - mn)
            al = jnp.exp(m - mn)
            l = al * l + jnp.sum(p, -1, keepdims=True)
            a = al * a + _dot(p.astype(BF16), vj)
            out += [mn, l, a]
        return tuple(out)

    init = (jnp.full((tq, 1), -jnp.inf, F32), jnp.zeros((tq, 1), F32), jnp.zeros((tq, LANES), F32)) * 2
    carry = lax.fori_loop(0, i, lambda j, c: step(j, c, False), init)
    m1, l1, a1, m2, l2, a2 = step(i, carry, True)
    o = a1 / l1 - lam * (a2 / l2)
    y = o * lax.rsqrt(jnp.mean(o * o, -1, keepdims=True) + LN_EPS) * g_ref[...] * (1.0 - lam_init)
    o_ref[...] = y.astype(BF16)


def _diff_attention(daq, dak, dav, lam, subln_g, batch, seq, lam_init, tq):
    T = batch * seq
    nq = seq // tq
    return pl.pallas_call(
        functools.partial(_diffattn_kernel, tq=tq, lam_init=lam_init),
        grid=(batch, DA_HEADS, nq),
        in_specs=[pl.BlockSpec(memory_space=pltpu.SMEM),
                  pl.BlockSpec((tq, LANES), lambda b, h, i: (b * nq + i, h)),
                  pl.BlockSpec((seq, LANES), lambda b, h, i: (b, h)),
                  pl.BlockSpec((seq, LANES), lambda b, h, i: (b, h)),
                  pl.BlockSpec((1, LANES), lambda b, h, i: (0, 0))],
        out_specs=pl.BlockSpec((tq, LANES), lambda b, h, i: (b * nq + i, h)),
        out_shape=jax.ShapeDtypeStruct((T, DA_HEADS * LANES), BF16),
        compiler_params=_cparams(("parallel", "parallel", "arbitrary")),
        name="diff_attention",
    )(lam.reshape(1), daq, dak, dav, subln_g.reshape(1, LANES).astype(F32))


def _dsa_kernel(ixq_ref, ixw_ref, dsq_ref, ikk_ref, kv_ref, vk_ref, o_ref,
                qi_s, qd_s, keys_s, m_s, l_s, acc_s, *, tq, topk):
    i = pl.program_id(1)
    nblk = i + 1
    lane = lax.broadcasted_iota(I32, (tq, LANES), 1)
    lower = lane < DSA_HEAD_DIM
    for h in range(IDX_HEADS):
        sl = slice((h // 2) * LANES, (h // 2 + 1) * LANES)
        keep = lower if h % 2 == 0 else jnp.logical_not(lower)
        blk = ixq_ref[:, sl]
        qi_s[h] = jnp.where(keep, blk, jnp.zeros_like(blk))
        blk = dsq_ref[:, sl]
        qd_s[h] = jnp.where(keep, blk, jnp.zeros_like(blk))
    row = lax.broadcasted_iota(I32, (tq, tq), 0)
    col = lax.broadcasted_iota(I32, (tq, tq), 1)

    def score_chunk(j, _):
        off = pl.multiple_of(j * tq, tq)
        ik = ikk_ref[pl.ds(off, tq), :]
        sc = jnp.zeros((tq, tq), F32)
        for h in range(IDX_HEADS):
            sc = sc + ixw_ref[:, h:h + 1] * jnp.maximum(_dot_nt(qi_s[h], ik), 0.0)
        sc = jnp.where(j * tq + col <= i * tq + row, sc, -jnp.inf)
        bits = pltpu.bitcast(sc, I32)
        keys_s[j] = bits ^ ((bits >> 31) & 0x7FFFFFFF)
        return 0

    lax.fori_loop(0, nblk, score_chunk, 0)

    def count(pred):
        def body(j, acc):
            kj = keys_s[j]
            for c in range(tq // LANES):
                acc = acc + pred(kj[:, c * LANES:(c + 1) * LANES], j * tq + c * LANES + lane).astype(F32)
            return acc
        acc = lax.fori_loop(0, nblk, body, jnp.zeros((tq, LANES), F32))
        return jnp.sum(acc, -1, keepdims=True)

    def bcast(v):
        return jnp.broadcast_to(v, (tq, LANES))

    kf = float(topk)
    zero_i = jnp.zeros((tq, 1), I32)
    c0 = count(lambda k, idx: k >= 0)
    tau = jnp.where(c0 >= kf, zero_i, jnp.full((tq, 1), INT_MIN, I32))

    def tau_bit(b, tau):
        cand = tau | (jnp.int32(1) << (30 - b))
        cb = bcast(cand)
        c = count(lambda k, idx: k >= cb)
        return jnp.where(c >= kf, cand, tau)

    tau = lax.fori_loop(0, 31, tau_bit, tau)
    taub = bcast(tau)
    need = kf - count(lambda k, idx: k > taub)

    def cut_bit(b, cut):
        cand = cut | (jnp.int32(1) << (12 - b))
        cb = bcast(cand)
        c = count(lambda k, idx: (k == taub) & (idx < cb))
        return jnp.where(c <= need, cand, cut)

    cut = lax.fori_loop(0, 13, cut_bit, zero_i)
    cutb = bcast(cut)

    m_s[...] = jnp.full(m_s.shape, NEG_BIG, F32)
    l_s[...] = jnp.zeros(l_s.shape, F32)
    acc_s[...] = jnp.zeros(acc_s.shape, F32)

    def attend(j, _):
        off = pl.multiple_of(j * tq, tq)
        kj = keys_s[j]
        biases = []
        for c in range(tq // LANES):
            kc = kj[:, c * LANES:(c + 1) * LANES]
            idx = j * tq + c * LANES + lane
            sel = (kc > taub) | ((kc == taub) & (idx < cutb))
            sel = sel & (idx <= i * tq + row[:, :LANES])
            biases.append(jnp.where(sel, 0.0, NEG_BIG))
        bias = jnp.concatenate(biases, 1)
        kvj = kv_ref[pl.ds(off, tq), :]
        vkj = vk_ref[pl.ds(off, tq), :]
        for h in range(DSA_HEADS):
            kk = kvj if h % 2 == 0 else vkj
            s = _dot_nt(qd_s[h], kk) + bias
            m_old = m_s[h]
            mn = jnp.maximum(m_old, jnp.max(s, -1, keepdims=True))
            p = jnp.exp(s - mn)
            al = jnp.exp(m_old - mn)
            l_s[h] = al * l_s[h] + jnp.sum(p, -1, keepdims=True)
            acc_s[h] = al * acc_s[h] + _dot(p.astype(BF16), kk)
            m_s[h] = mn
        return 0

    lax.fori_loop(0, nblk, attend, 0)
    for pr in range(DSA_HEADS // 2):
        o_up = acc_s[2 * pr] / l_s[2 * pr]
        o_lo = acc_s[2 * pr + 1] / l_s[2 * pr + 1]
        o_ref[:, pr * LANES:(pr + 1) * LANES] = jnp.where(lower, o_lo, o_up).astype(BF16)


def _sparse_attention(ixq, ixw, dsq, ikk, kv, vk, batch, seq, tq):
    T = batch * seq
    nq = seq // tq
    topk = min(DSA_TOPK_MAX, seq // 4)
    tile = lambda n: pl.BlockSpec((tq, n), lambda b, i: (b * nq + i, 0))
    full = pl.BlockSpec((seq, LANES), lambda b, i: (b, 0))
    return pl.pallas_call(
        functools.partial(_dsa_kernel, tq=tq, topk=topk),
        grid=(batch, nq),
        in_specs=[tile(512), tile(LANES), tile(512), full, full, full],
        out_specs=tile(512),
        out_shape=jax.ShapeDtypeStruct((T, 512), BF16),
        scratch_shapes=[pltpu.VMEM((IDX_HEADS, tq, LANES), BF16),
                        pltpu.VMEM((DSA_HEADS, tq, LANES), BF16),
                        pltpu.VMEM((nq, tq, tq), I32),
                        pltpu.VMEM((DSA_HEADS, tq, 1), F32),
                        pltpu.VMEM((DSA_HEADS, tq, 1), F32),
                        pltpu.VMEM((DSA_HEADS, tq, LANES), F32)],
        compiler_params=_cparams(("parallel", "arbitrary")),
        name="sparse_attention",
    )(ixq, ixw, dsq, ikk, kv, vk)


def _matmul_kernel(x_ref, w_ref, o_ref):
    o_ref[...] = _dot(x_ref[...].astype(BF16), w_ref[...]).astype(o_ref.dtype)


def _matmul(x, w, tm, out_dtype):
    M, K = x.shape
    N = w.shape[1]
    return pl.pallas_call(
        _matmul_kernel,
        grid=(M // tm,),
        in_specs=[pl.BlockSpec((tm, K), lambda i: (i, 0)), pl.BlockSpec((K, N), lambda i: (0, 0))],
        out_specs=pl.BlockSpec((tm, N), lambda i: (i, 0)),
        out_shape=jax.ShapeDtypeStruct((M, N), out_dtype),
        compiler_params=_cparams(("parallel",)),
        name="matmul",
    )(x, w)


def _memattn_kernel(q_ref, kv_ref, o_ref):
    scale = MEM_HEAD_DIM ** -0.5
    width = MEM_HEADS * MEM_HEAD_DIM
    for h in range(MEM_HEADS):
        sl = slice(h * MEM_HEAD_DIM, (h + 1) * MEM_HEAD_DIM)
        k = kv_ref[:, sl]
        v = kv_ref[:, width + h * MEM_HEAD_DIM:width + (h + 1) * MEM_HEAD_DIM]
        s = _dot_nt(q_ref[:, sl], k) * scale
        p = jnp.exp(s - jnp.max(s, -1, keepdims=True))
        o = _dot(p.astype(BF16), v) / jnp.sum(p, -1, keepdims=True)
        o_ref[:, sl] = o.astype(BF16)


def _memory_attention(memq, mkv, batch, seq, mem_len, tq):
    T = batch * seq
    nq = seq // tq
    return pl.pallas_call(
        _memattn_kernel,
        grid=(batch, nq),
        in_specs=[pl.BlockSpec((tq, 512), lambda b, i: (b * nq + i, 0)),
                  pl.BlockSpec((mem_len, 1024), lambda b, i: (b, 0))],
        out_specs=pl.BlockSpec((tq, 512), lambda b, i: (b * nq + i, 0)),
        out_shape=jax.ShapeDtypeStruct((T, 512), BF16),
        compiler_params=_cparams(("parallel", "parallel")),
        name="memory_attention",
    )(memq, mkv)


def _merge_kernel(x_ref, ya_ref, yb_ref, yc_ref, wg_ref, wa_ref, wb_ref, wc_ref, wo_ref,
                  g_ref, b_ref, h_ref, *, alpha):
    x = x_ref[...]
    xb = x.astype(BF16)
    d = x.shape[1]
    merged = None
    for br, (y_ref, w_ref) in enumerate(((ya_ref, wa_ref), (yb_ref, wb_ref), (yc_ref, wc_ref))):
        gate = jax.nn.sigmoid(_dot(xb, wg_ref[:, br * d:(br + 1) * d]))
        term = gate * _dot(y_ref[...], w_ref[...])
        merged = term if merged is None else merged + term
    mix = _dot(merged.astype(BF16), wo_ref[...])
    h_ref[...] = _layer_norm(alpha * x + mix, g_ref[...], b_ref[...])


def _merge_project_norm(x2d, ya, yb, yc, w_gates, w_ba, w_bb, w_bc, w_o, ln_g, ln_b, alpha, tm):
    T, D = x2d.shape
    row = lambda n: pl.BlockSpec((tm, n), lambda i: (i, 0))
    whole = lambda a: pl.BlockSpec(a.shape, lambda i: (0, 0))
    ws = [w.astype(BF16) for w in (w_gates, w_ba, w_bb, w_bc, w_o)]
    vec = [v.reshape(1, D).astype(F32) for v in (ln_g, ln_b)]
    return pl.pallas_call(
        functools.partial(_merge_kernel, alpha=alpha),
        grid=(T // tm,),
        in_specs=[row(D), row(512), row(512), row(512)] + [whole(w) for w in ws] + [whole(v) for v in vec],
        out_specs=row(D),
        out_shape=jax.ShapeDtypeStruct((T, D), F32),
        compiler_params=_cparams(("parallel",)),
        name="merge_project_norm",
    )(x2d, ya, yb, yc, *ws, *vec)


def _router_kernel(h_ref, wr_ref, bias_ref, eidx_ref, gate_ref, rank_ref, cnt_ref, carry_s, *, tm):
    @pl.when(pl.program_id(0) == 0)
    def _():
        carry_s[...] = jnp.zeros(carry_s.shape, F32)

    neg = -jnp.inf
    logits = _dot_nt(wr_ref[...], h_ref[...].astype(BF16))
    scores = jax.nn.sigmoid(logits)
    biased = scores + bias_ref[...]
    erow = lax.broadcasted_iota(I32, (N_EXPERTS, tm), 0)
    big = jnp.int32(2 ** 30)

    def top1(vals, ids):
        mx = jnp.max(vals, 0, keepdims=True)
        am = jnp.min(jnp.where(vals == mx, ids, big), 0, keepdims=True)
        return mx, am

    gs = []
    ids = lax.broadcasted_iota(I32, (GROUP_SIZE, tm), 0)
    for g in range(N_GROUPS):
        v = biased[g * GROUP_SIZE:(g + 1) * GROUP_SIZE]
        m1, a1 = top1(v, ids)
        m2, _ = top1(jnp.where(ids == a1, neg, v), ids)
        gs.append(m1 + m2)
    gscore = jnp.concatenate(gs, 0)
    grow = lax.broadcasted_iota(I32, (N_GROUPS, tm), 0)
    gsel = jnp.zeros((N_GROUPS, tm), F32)
    for _ in range(TOPK_GROUPS):
        _, ga = top1(gscore, grow)
        hit = grow == ga
        gsel = jnp.where(hit, 1.0, gsel)
        gscore = jnp.where(hit, neg, gscore)
    masked = jnp.concatenate(
        [jnp.where(gsel[g:g + 1] > 0.0, biased[g * GROUP_SIZE:(g + 1) * GROUP_SIZE], neg)
         for g in range(N_GROUPS)], 0)

    eids, ws = [], []
    hot = jnp.zeros((N_EXPERTS, tm), F32)
    for _ in range(TOP_K):
        _, ea = top1(masked, erow)
        hit = erow == ea
        eids.append(ea)
        ws.append(jnp.sum(jnp.where(hit, scores, 0.0), 0, keepdims=True))
        hot = jnp.where(hit, 1.0, hot)
        masked = jnp.where(hit, neg, masked)
    w = jnp.concatenate(ws, 0)
    eidx_ref[...] = jnp.concatenate(eids, 0)
    gate_ref[...] = w / jnp.sum(w, 0, keepdims=True) * ROUTED_SCALE

    r_i =lax.broadcasted_iota(I32, (tm, tm), 0)
    c_i = lax.broadcasted_iota(I32, (tm, tm), 1)
    before = jnp.where(r_i < c_i, 1.0, 0.0).astype(BF16)
    pos = carry_s[...] + _dot(hot.astype(BF16), before)
    rank_ref[...] = jnp.concatenate(
        [jnp.sum(jnp.where(erow == e, pos, 0.0), 0, keepdims=True) for e in eids], 0).astype(I32)
    carry_s[...] = carry_s[...] + jnp.sum(hot, 1, keepdims=True)
    cnt_ref[...] = carry_s[...].astype(I32)


def _route(h2d, w_router, router_bias, tm):
    T, D = h2d.shape
    out = lambda dt: jax.ShapeDtypeStruct((TOP_K, T), dt)
    blk = pl.BlockSpec((TOP_K, tm), lambda i: (0, i))
    return pl.pallas_call(
        functools.partial(_router_kernel, tm=tm),
        grid=(T // tm,),
        in_specs=[pl.BlockSpec((tm, D), lambda i: (i, 0)),
                  pl.BlockSpec((N_EXPERTS, D), lambda i: (0, 0)),
                  pl.BlockSpec((N_EXPERTS, 1), lambda i: (0, 0))],
        out_specs=[blk, blk, blk, pl.BlockSpec((N_EXPERTS, 1), lambda i: (0, 0))],
        out_shape=[out(I32), out(F32), out(I32), jax.ShapeDtypeStruct((N_EXPERTS, 1), I32)],
        scratch_shapes=[pltpu.VMEM((N_EXPERTS, 1), F32)],
        compiler_params=_cparams(("arbitrary",)),
        name="router",
    )(h2d, w_router.T.astype(BF16), router_bias.reshape(N_EXPERTS, 1).astype(F32))


def _experts_kernel(blk_e_ref, tok_hbm, h_hbm, rw_ref, wg_ref, wu_ref, wd_ref, y_ref,
                    tok_s, xbuf, wgb, wub, wdb, tok_sem, row_sem, *, rows):
    i = pl.program_id(0)
    n = pl.num_programs(0)

    def tok_copy(blk, slot):
        return pltpu.make_async_copy(tok_hbm.at[blk], tok_s.at[slot], tok_sem.at[slot])

    def issue_rows(slot):
        def body(r, _):
            t = tok_s[slot, r]
            pltpu.make_async_copy(h_hbm.at[pl.ds(t, 1)], xbuf.at[slot, pl.ds(r, 1)], row_sem.at[slot]).start()
            return 0
        lax.fori_loop(0, rows, body, 0)

    @pl.when(i == 0)
    def _():
        tok_copy(0, 0).start()
        tok_copy(0, 0).wait()
        issue_rows(0)

        @pl.when(n > 1)
        def _():
            tok_copy(1, 1).start()

    nxt = (i + 1) % 2

    @pl.when(i + 1 < n)
    def _():
        tok_copy(i + 1, nxt).wait()
        issue_rows(nxt)

    @pl.when(i + 2 < n)
    def _():
        tok_copy(i + 2, i % 2).start()

    e = blk_e_ref[i]
    prev = blk_e_ref[jnp.maximum(i - 1, 0)]

    @pl.when((i == 0) | (e != prev))
    def _():
        wgb[...] = wg_ref[0].astype(BF16)
        wub[...] = wu_ref[0].astype(BF16)
        wdb[...] = wd_ref[0].astype(BF16)

    slot = i % 2
    pltpu.make_async_copy(h_hbm.at[pl.ds(0, rows)], xbuf.at[slot], row_sem.at[slot]).wait()
    x = xbuf[slot].astype(BF16)
    hidden = jax.nn.silu(_dot(x, wgb[...])) * _dot(x, wub[...])
    y_ref[...] = _dot(hidden.astype(BF16), wdb[...]) * rw_ref[...]


def _routed_experts(h2d, blk_e, row_tok, row_w, w_eg, w_eu, w_ed, rows):
    T, D = h2d.shape
    n_blocks = blk_e.shape[0]
    ff = w_eg.shape[-1]
    return pl.pallas_call(
        functools.partial(_experts_kernel, rows=rows),
        grid_spec=pltpu.PrefetchScalarGridSpec(
            num_scalar_prefetch=1,
            grid=(n_blocks,),
            in_specs=[pl.BlockSpec(memory_space=pl.ANY),
                      pl.BlockSpec(memory_space=pl.ANY),
                      pl.BlockSpec((rows, 1), lambda i, be: (i, 0)),
                      pl.BlockSpec((1, D, ff), lambda i, be: (be[i], 0, 0)),
                      pl.BlockSpec((1, D, ff), lambda i, be: (be[i], 0, 0)),
                      pl.BlockSpec((1, ff, D), lambda i, be: (be[i], 0, 0))],
            out_specs=pl.BlockSpec((rows, D), lambda i, be: (i, 0)),
            scratch_shapes=[pltpu.SMEM((2, rows), I32),
                            pltpu.VMEM((2, rows, D), F32),
                            pltpu.VMEM((D, ff), BF16),
                            pltpu.VMEM((D, ff), BF16),
                            pltpu.VMEM((ff, D), BF16),
                            pltpu.SemaphoreType.DMA((2,)),
                            pltpu.SemaphoreType.DMA((2,))]),
        out_shape=jax.ShapeDtypeStruct((n_blocks * rows, D), F32),
        compiler_params=_cparams(("arbitrary",)),
        name="routed_experts",
    )(blk_e, row_tok.reshape(n_blocks, rows), h2d, row_w.reshape(n_blocks * rows, 1), w_eg, w_eu, w_ed)


def _combine_kernel(dest_hbm, ys_hbm, h_ref, wsg_ref, wsu_ref, wsd_ref, g_ref, b_ref, o_ref,
                    dest_s, buf, dest_sem, row_sem, *, tm, alpha):
    i = pl.program_id(0)
    n = pl.num_programs(0)

    def dest_copy(blk, slot):
        return pltpu.make_async_copy(dest_hbm.at[blk], dest_s.at[slot], dest_sem.at[slot])

    def issue_rows(slot):
        def body(r, _):
            for k in range(TOP_K):
                d = dest_s[slot, k * tm + r]
                pltpu.make_async_copy(ys_hbm.at[pl.ds(d, 1)], buf.at[slot, k, pl.ds(r, 1)],
                                      row_sem.at[slot]).start()
            return 0
        lax.fori_loop(0, tm, body, 0)

    @pl.when(i == 0)
    def _():
        dest_copy(0, 0).start()
        dest_copy(0, 0).wait()
        issue_rows(0)

        @pl.when(n > 1)
        def _():
            dest_copy(1, 1).start()

    nxt = (i + 1) % 2

    @pl.when(i + 1 < n)
    def _():
        dest_copy(i + 1, nxt).wait()
        issue_rows(nxt)

    @pl.when(i + 2 < n)
    def _():
        dest_copy(i + 2, i % 2).start()

    h = h_ref[...]
    hb = h.astype(BF16)
    shared = _dot((jax.nn.silu(_dot(hb, wsg_ref[...])) * _dot(hb, wsu_ref[...])).astype(BF16), wsd_ref[...])
    slot = i % 2
    total = alpha * h + shared
    for k in range(TOP_K):
        pltpu.make_async_copy(ys_hbm.at[pl.ds(0, tm)], buf.at[slot, k], row_sem.at[slot]).wait()
    for k in range(TOP_K):
        total = total + buf[slot, k]
    o_ref[...] = _layer_norm(total, g_ref[...], b_ref[...])


def _combine_shared_norm(h2d, ys, dest, w_sg, w_su, w_sd, ln_g, ln_b, alpha, tm):
    T, D = h2d.shape
    nt = T // tm
    dest_blocks = dest.reshape(TOP_K, nt, tm).transpose(1, 0, 2).reshape(nt, TOP_K * tm)
    ws = [w.astype(BF16) for w in (w_sg, w_su, w_sd)]
    vec = [v.reshape(1, D).astype(F32) for v in (ln_g, ln_b)]
    whole = lambda a: pl.BlockSpec(a.shape, lambda i: (0, 0))
    return pl.pallas_call(
        functools.partial(_combine_kernel, tm=tm, alpha=alpha),
        grid=(nt,),
        in_specs=[pl.BlockSpec(memory_space=pl.ANY), pl.BlockSpec(memory_space=pl.ANY),
                  pl.BlockSpec((tm, D), lambda i: (i, 0))] + [whole(w) for w in ws] + [whole(v) for v in vec],
        out_specs=pl.BlockSpec((tm, D), lambda i: (i, 0)),
        out_shape=jax.ShapeDtypeStruct((T, D), F32),
        scratch_shapes=[pltpu.SMEM((2, TOP_K * tm), I32),
                        pltpu.VMEM((2, TOP_K, tm, D), F32),
                        pltpu.SemaphoreType.DMA((2,)),
                        pltpu.SemaphoreType.DMA((2,))],
        compiler_params=_cparams(("arbitrary",)),
        name="combine_shared_norm",
    )(dest_blocks, ys, h2d, *ws, *vec)


MOE_ROWS = 128


def _dispatch_plan(eidx, gate, rank, counts, rows):
    T = eidx.shape[1]
    A = T * TOP_K
    counts = counts.reshape(N_EXPERTS)
    padded = (counts + rows - 1) // rows * rows
    pend = jnp.cumsum(padded)
    pstart = pend - padded
    dest = pstart[eidx] + rank
    n_blocks = (A + N_EXPERTS * (rows - 1) + rows - 1) // rows
    P = n_blocks * rows
    tok = jnp.broadcast_to(jnp.arange(T, dtype=I32)[None, :], (TOP_K, T))
    row_tok = jnp.zeros((P,), I32).at[dest.reshape(A)].set(tok.reshape(A))
    row_w = jnp.zeros((P,), F32).at[dest.reshape(A)].set(gate.reshape(A))
    blk_start = jnp.arange(n_blocks, dtype=I32) * rows
    blk_e = jnp.minimum(jnp.searchsorted(pend, blk_start, side='right'), N_EXPERTS - 1).astype(I32)
    return dest, blk_e, row_tok, row_w


def _layer(h2d, mem2d, p, l, depth, batch, seq, mem_len):
    alpha = (2 * depth) ** 0.25
    lam_init = 0.8 - 0.6 * math.exp(-0.3 * l)
    w_in = p['w_in'][l]
    daq, dak, dsq, ixq, kv, vk, ikk, dav, memq, ixw = _in_projection(h2d, w_in, seq, tm=min(512, seq))
    lam = (jnp.exp(jnp.sum(p['lq1'][l].astype(F32) * p['lk1'][l].astype(F32)))
           - jnp.exp(jnp.sum(p['lq2'][l].astype(F32) * p['lk2'][l].astype(F32))) + lam_init)
    ya = _diff_attention(daq, dak, dav, lam, p['subln_g'][l], batch, seq, lam_init, tq=min(512, seq))
    yb = _sparse_attention(ixq, ixw, dsq, ikk, kv, vk, batch, seq, tq=256)
    mkv = _matmul(mem2d, p['w_mem_kv'][l].astype(BF16), tm=mem_len, out_dtype=BF16)
    yc = _memory_attention(memq, mkv, batch, seq, mem_len, tq=min(512, seq))
    h1 = _merge_project_norm(h2d, ya, yb, yc, w_in[:, 3272:], p['w_ba'][l], p['w_bb'][l], p['w_bc'][l],
                             p['w_o'][l], p['ln1_g'][l], p['ln1_b'][l], alpha, tm=256)
    eidx, gate, rank, counts = _route(h1, p['w_router'][l], p['router_bias'][l], tm=512)
    dest, blk_e, row_tok, row_w = _dispatch_plan(eidx, gate, rank, counts, MOE_ROWS)
    ys = _routed_experts(h1, blk_e, row_tok, row_w, p['w_eg'][l], p['w_eu'][l], p['w_ed'][l], MOE_ROWS)
    return _combine_shared_norm(h1, ys, dest, p['w_sg'][l], p['w_su'][l], p['w_sd'][l],
                                p['ln2_g'][l], p['ln2_b'][l], alpha, tm=128)


def kernel(x, mem, w_in, da_lambda_q1, da_lambda_k1, da_lambda_q2, da_lambda_k2, da_subln_g, w_mem_kv, w_branch_a, w_branch_b, w_branch_c, w_out, ln1_g, ln1_b, w_router, router_bias, w_exp_gate, w_exp_up, w_exp_down, w_sh_gate, w_sh_up, w_sh_down, ln2_g, ln2_b):
    batch, seq, d = x.shape
    mem_len = mem.shape[1]
    depth = w_in.shape[0]
    p = dict(w_in=w_in, lq1=da_lambda_q1, lk1=da_lambda_k1, lq2=da_lambda_q2, lk2=da_lambda_k2,
             subln_g=da_subln_g, w_mem_kv=w_mem_kv, w_ba=w_branch_a, w_bb=w_branch_b, w_bc=w_branch_c,
             w_o=w_out, ln1_g=ln1_g, ln1_b=ln1_b, w_router=w_router, router_bias=router_bias,
             w_eg=w_exp_gate, w_eu=w_exp_up, w_ed=w_exp_down, w_sg=w_sh_gate, w_su=w_sh_up,
             w_sd=w_sh_down, ln2_g=ln2_g, ln2_b=ln2_b)
    h = x.reshape(batch * seq, d)
    mem2d = mem.reshape(batch * mem_len, d)
    for l in range(depth):
        h = _layer(h, mem2d, p, l, depth, batch, seq, mem_len)
    return h.reshape(batch, seq, d)
```

```python
import functools
import math

import jax
import jax.numpy as jnp
import numpy as np
from jax import lax
from jax.experimental import pallas as pl
from jax.experimental.pallas import tpu as pltpu

F32 = jnp.float32
BF16 = jnp.bfloat16
I32 = jnp.int32

LANES = 128
ROPE_THETA = 10000.0
LN_EPS = 1e-5

DA_HEADS = 4
DA_HEAD_DIM = 64
DSA_HEADS = 8
DSA_HEAD_DIM = 64
IDX_HEADS = 8
DSA_TOPK_MAX = 256
MEM_HEADS = 4
MEM_HEAD_DIM = 128
N_BRANCHES = 3

N_EXPERTS = 256
TOP_K = 8
N_GROUPS = 8
TOPK_GROUPS = 4
GROUP_SIZE = N_EXPERTS // N_GROUPS
ROUTED_SCALE = 2.5

NEG_BIG = -1e30
INT_MIN = -(2 ** 31)

VMEM_LIMIT = 56 * 1024 * 1024


def _cparams(sem):
    return pltpu.CompilerParams(dimension_semantics=sem, vmem_limit_bytes=VMEM_LIMIT)


def _dot(a, b):
    return jnp.dot(a, b, preferred_element_type=F32)


def _dot_nt(a, b):
    return lax.dot_general(a, b, (((1,), (1,)), ((), ())), preferred_element_type=F32)


def _layer_norm(x, g, b):
    mu = jnp.mean(x, -1, keepdims=True)
    xc = x - mu
    var = jnp.mean(xc * xc, -1, keepdims=True)
    return xc * lax.rsqrt(var + LN_EPS) * g + b


def _inproj_kernel(x_ref, wr_ref, wn_ref, tab_ref,
                   daq_ref, dak_ref, dsq_ref, ixq_ref, kv_ref, vk_ref, ikk_ref,
                   dav_ref, memq_ref, ixw_ref):
    xb = x_ref[...].astype(BF16)

    def rope(t, kind):
        c = tab_ref[3 * kind]
        sa = tab_ref[3 * kind + 1]
        sb = tab_ref[3 * kind + 2]
        return t * c + pltpu.roll(t, 96, 1) * sa + pltpu.roll(t, 32, 1) * sb

    groups = ((daq_ref, 0.125), (dak_ref, 1.0), (dsq_ref, 0.125), (ixq_ref, 1.0))
    for gi, (ref, scale) in enumerate(groups):
        y = _dot(xb, wr_ref[:, gi * 512:(gi + 1) * 512])
        for c in range(4):
            r = rope(y[:, c * LANES:(c + 1) * LANES], 0)
            if scale != 1.0:
                r = r * scale
            ref[:, c * LANES:(c + 1) * LANES] = r.astype(BF16)
    y = _dot(xb, wr_ref[:, 2048:2432])
    kv_ref[...] = rope(y[:, 0:128], 1).astype(BF16)
    vk_ref[...] = rope(y[:, 128:256], 2).astype(BF16)
    ikk_ref[...] = rope(y[:, 256:384], 0).astype(BF16)
    y = _dot(xb, wn_ref[...])
    dav_ref[...] = y[:, 0:512].astype(BF16)
    memq_ref[...] = y[:, 512:1024].astype(BF16)
    ixw_ref[...] = y[:, 1024:1152]


def _rope_tables(seq):
    dim = 64
    inv = 1.0 / (ROPE_THETA ** (jnp.arange(0, dim, 2, dtype=F32) / dim))
    ang = jnp.arange(seq, dtype=F32)[:, None] * inv[None, :]
    ang = jnp.concatenate([ang, ang], -1)
    cos, sin = jnp.cos(ang), jnp.sin(ang)
    local = jnp.arange(dim)[None, :]
    sa = jnp.where(local < dim // 2, -sin, 0.0)
    sb = jnp.where(local >= dim // 2, sin, 0.0)
    one, zero = jnp.ones_like(cos), jnp.zeros_like(cos)
    cat = lambda a, b: jnp.concatenate([a, b], -1)
    return jnp.stack([cat(cos, cos), cat(sa, sa), cat(sb, sb),
                      cat(cos, one), cat(sa, zero), cat(sb, zero),
                      cat(one, cos), cat(zero, sa), cat(zero, sb)], 0)


def _in_projection(x2d, w_in, seq, tm):
    T, D = x2d.shape
    o = np.cumsum([0, 512, 512, 512, 512, 64, 64, 512, 64, 8, 512]).tolist()
    col = lambda i: w_in[:, o[i]:o[i + 1]]
    da_q, da_k, da_v, ds_q, ds_k, ds_v, ix_q, ix_k, ix_w, mem_q = [col(i) for i in range(10)]
    ds_q = ds_q.reshape(D, DSA_HEADS // 2, 2, DSA_HEAD_DIM)[:, :, ::-1, :].reshape(D, 512)
    w_rope = jnp.concatenate([da_q, da_k, ds_q, ix_q, ds_k, ds_v, ds_v, ds_k, ix_k, ix_k], 1).astype(BF16)
    w_plain = jnp.concatenate([da_v, mem_q, ix_w, jnp.zeros((D, LANES - IDX_HEADS), w_in.dtype)], 1).astype(BF16)
    tabs = _rope_tables(seq)
    nseq = seq // tm
    bf = lambda n: jax.ShapeDtypeStruct((T, n), BF16)
    row = lambda n: pl.BlockSpec((tm, n), lambda i: (i, 0))
    return pl.pallas_call(
        _inproj_kernel,
        grid=(T // tm,),
        in_specs=[row(D),
                  pl.BlockSpec(w_rope.shape, lambda i: (0, 0)),
                  pl.BlockSpec(w_plain.shape, lambda i: (0, 0)),
                  pl.BlockSpec((9, tm, LANES), lambda i: (0, i % nseq, 0))],
        out_specs=[row(512), row(512), row(512), row(512), row(128), row(128), row(128),
                   row(512), row(512), row(128)],
        out_shape=[bf(512), bf(512), bf(512), bf(512), bf(128), bf(128), bf(128),
                   bf(512), bf(512), jax.ShapeDtypeStruct((T, LANES), F32)],
        compiler_params=_cparams(("parallel",)),
        name="in_projection",
    )(x2d, w_rope, w_plain, tabs)


def _diffattn_kernel(lam_ref, q_ref, k_ref, v_ref, g_ref, o_ref, *, tq, lam_init):
    i = pl.program_id(2)
    lam = lam_ref[0]
    q = q_ref[...]
    lane = lax.broadcasted_iota(I32, q.shape, 1)
    zero = jnp.zeros_like(q)
    qs = (jnp.where(lane < DA_HEAD_DIM, q, zero), jnp.where(lane >= DA_HEAD_DIM, q, zero))
    row = lax.broadcasted_iota(I32, (tq, tq), 0)
    col = lax.broadcasted_iota(I32, (tq, tq), 1)

    def step(j, carry, diagonal):
        off = pl.multiple_of(j * tq, tq)
        kj = k_ref[pl.ds(off, tq), :]
        vj = v_ref[pl.ds(off, tq), :]
        out = []
        for mp in range(2):
            m, l, a = carry[3 * mp:3 * mp + 3]
            s = _dot_nt(qs[mp], kj)
            if diagonal:
                s = jnp.where(col <= row, s, -jnp.inf)
            mn = jnp.maximum(m, jnp.max(s, -1, keepdims=True))
            p = jnp.exp(s - mn)
            al = jnp.exp(m - mn)
            l = al * l + jnp.sum(p, -1, keepdims=True)
            a = al * a + _dot(p.astype(BF16), vj)
            out += [mn, l, a]
        return tuple(out)

    init = (jnp.full((tq, 1), -jnp.inf, F32), jnp.zeros((tq, 1), F32), jnp.zeros((tq, LANES), F32)) * 2
    carry = lax.fori_loop(0, i, lambda j, c: step(j, c, False), init)
    m1, l1, a1, m2, l2, a2 = step(i, carry, True)
    o = a1 / l1 - lam * (a2 / l2)
    y = o * lax.rsqrt(jnp.mean(o * o, -1, keepdims=True) + LN_EPS) * g_ref[...] * (1.0 - lam_init)
    o_ref[...] = y.astype(BF16)


def _diff_attention(daq, dak, dav, lam, subln_g, batch, seq, lam_init, tq):
    T = batch * seq
    nq = seq // tq
    return pl.pallas_call(
        functools.partial(_diffattn_kernel, tq=tq, lam_init=lam_init),
        grid=(batch, DA_HEADS, nq),
        in_specs=[pl.BlockSpec(memory_space=pltpu.SMEM),
                  pl.BlockSpec((tq, LANES), lambda b, h, i: (b * nq + i, h)),
                  pl.BlockSpec((seq, LANES), lambda b, h, i: (b, h)),
                  pl.BlockSpec((seq, LANES), lambda b, h, i: (b, h)),
                  pl.BlockSpec((1, LANES), lambda b, h, i: (0, 0))],
        out_specs=pl.BlockSpec((tq, LANES), lambda b, h, i: (b * nq + i, h)),
        out_shape=jax.ShapeDtypeStruct((T, DA_HEADS * LANES), BF16),
        compiler_params=_cparams(("parallel", "parallel", "arbitrary")),
        name="diff_attention",
    )(lam.reshape(1), daq, dak, dav, subln_g.reshape(1, LANES).astype(F32))


def _dsa_kernel(ixq_ref, ixw_ref, dsq_ref, ikk_ref, kv_ref, vk_ref, o_ref,
                qi_s, qd_s, sc_s, mx_s, l_s, acc_s, *, tq, topk):
    i = pl.program_id(1)
    nblk = i + 1
    lane = lax.broadcasted_iota(I32, (tq, LANES), 1)
    lower = lane < DSA_HEAD_DIM
    for h in range(IDX_HEADS):
        sl = slice((h // 2) * LANES, (h // 2 + 1) * LANES)
        keep = lower if h % 2 == 0 else jnp.logical_not(lower)
        blk = ixq_ref[:, sl]
        qi_s[h] = jnp.where(keep, blk, jnp.zeros_like(blk))
        blk = dsq_ref[:, sl]
        qd_s[h] = jnp.where(keep, blk, jnp.zeros_like(blk))
    row = lax.broadcasted_iota(I32, (tq, tq), 0)
    col = lax.broadcasted_iota(I32, (tq, tq), 1)

    def score_chunk(j, _):
        off = pl.multiple_of(j * tq, tq)
        ik = ikk_ref[pl.ds(off, tq), :]
        sc = jnp.zeros((tq, tq), F32)
        for h in range(IDX_HEADS):
            sc = sc + ixw_ref[:, h:h + 1] * jnp.maximum(_dot_nt(qi_s[h], ik), 0.0)
        sc_s[j] = jnp.where(j * tq + col <= i * tq + row, sc, -jnp.inf)
        return 0

    lax.fori_loop(0, nblk, score_chunk, 0)

    half = tq // 2
    lane_h = lax.broadcasted_iota(I32, (half, LANES), 1)

    def count(pred):
        outs = []
        for r0 in range(0, tq, half):
            def body(j, acc, r0=r0):
                for c in range(tq // LANES):
                    kc = sc_s[j, r0:r0 + half, c * LANES:(c + 1) * LANES]
                    acc = acc + pred(kc, j * tq + c * LANES + lane_h, r0).astype(F32)
                return acc
            acc = lax.fori_loop(0, nblk, body, jnp.zeros((half, LANES), F32))
            outs.append(jnp.sum(acc, -1, keepdims=True))
        return jnp.concatenate(outs, 0)

    def bcast(v):
        return jnp.broadcast_to(v, (v.shape[0], LANES))

    def halves(v):
        return {r0: bcast(v[r0:r0 + half]) for r0 in range(0, tq, half)}

    def as_score(key):
        return pltpu.bitcast(key ^ ((key >> 31) & 0x7FFFFFFF), F32)

    kf = float(topk)
    zero_i = jnp.zeros((tq, 1), I32)
    c0 = count(lambda s, idx, r0: s >= 0.0)
    tau = jnp.where(c0 >= kf, zero_i, jnp.full((tq, 1), INT_MIN, I32))

    def tau_bit(b, tau):
        cand = tau | (jnp.int32(1) << (30 - b))
        cb = halves(as_score(cand))
        c = count(lambda s, idx, r0: s >= cb[r0])
        return jnp.where(c >= kf, cand, tau)

    tau = lax.fori_loop(0, 31, tau_bit, tau)
    tau_f = jnp.where(tau == INT_MIN, -jnp.inf, as_score(tau))
    taub = bcast(tau_f)
    tauh = halves(tau_f)
    c_ge = count(lambda s, idx, r0: s >= tauh[r0])

    def tie_cut():
        need = kf - count(lambda s, idx, r0: s > tauh[r0])

        def cut_bit(b, cut):
            cand = cut | (jnp.int32(1) << (12 - b))
            cb = halves(cand)
            c = count(lambda s, idx, r0: (s == tauh[r0]) & (idx < cb[r0]))
            return jnp.where(c <= need, cand, cut)

        return lax.fori_loop(0, 13, cut_bit, zero_i)

    cut = lax.cond(jnp.max(c_ge) > kf, tie_cut, lambda: jnp.full((tq, 1), 2 ** 13, I32))
    cutb = bcast(cut)

    mx_s[...] = jnp.full(mx_s.shape, NEG_BIG, F32)
    l_s[...] = jnp.zeros(l_s.shape, F32)
    acc_s[...] = jnp.zeros(acc_s.shape, F32)
    nlt = tq // LANES

    def lane_tiles(a):
        return [a[:, c * LANES:(c + 1) * LANES] for c in range(nlt)]

    def row_max(j, _):
        off = pl.multiple_of(j * tq, tq)
        kj = sc_s[j]
        biases = []
        for c, kc in enumerate(lane_tiles(kj)):
            idx = j * tq + c * LANES + lane
            sel = (kc > taub) | ((kc == taub) & (idx < cutb))
            sel = sel & (idx <= i * tq + row[:, :LANES])
            biases.append(jnp.where(sel, 0.0, NEG_BIG))
        bias = jnp.concatenate(biases, 1)
        sc_s[j] = bias
        kvj = kv_ref[pl.ds(off, tq), :]
        vkj = vk_ref[pl.ds(off, tq), :]
        for h in range(DSA_HEADS):
            s = _dot_nt(qd_s[h], kvj if h % 2 == 0 else vkj) + bias
            mx_s[h] = functools.reduce(jnp.maximum, lane_tiles(s), mx_s[h])
        return 0

    lax.fori_loop(0, nblk, row_max, 0)
    for h in range(DSA_HEADS):
        mx_s[h] = bcast(jnp.max(mx_s[h], -1, keepdims=True))

    def attend(j, _):
        off = pl.multiple_of(j * tq, tq)
        bias = sc_s[j]
        kvj = kv_ref[pl.ds(off, tq), :]
        vkj = vk_ref[pl.ds(off, tq), :]
        for h in range(DSA_HEADS):
            kk = kvj if h % 2 == 0 else vkj
            s = _dot_nt(qd_s[h], kk) + bias
            m = mx_s[h]
            ps = [jnp.exp(t - m) for t in lane_tiles(s)]
            l_s[h] = functools.reduce(jnp.add, ps, l_s[h])
            acc_s[h] = acc_s[h] + _dot(jnp.concatenate(ps, 1).astype(BF16), kk)
        return 0

    lax.fori_loop(0, nblk, attend, 0)

    def out(h):
        return acc_s[h] / jnp.sum(l_s[h], -1, keepdims=True)

    for pr in range(DSA_HEADS // 2):
        o_ref[:, pr * LANES:(pr + 1) * LANES] = jnp.where(lower, out(2 * pr + 1), out(2 * pr)).astype(BF16)


def _sparse_attention(ixq, ixw, dsq, ikk, kv, vk, batch, seq, tq):
    T = batch * seq
    nq = seq // tq
    topk = min(DSA_TOPK_MAX, seq // 4)
    tile = lambda n: pl.BlockSpec((tq, n), lambda b, i: (b * nq + i, 0))
    full = pl.BlockSpec((seq, LANES), lambda b, i: (b, 0))
    return pl.pallas_call(
        functools.partial(_dsa_kernel, tq=tq, topk=topk),
        grid=(batch, nq),
        in_specs=[tile(512), tile(LANES), tile(512), full, full, full],
        out_specs=tile(512),
        out_shape=jax.ShapeDtypeStruct((T, 512), BF16),
        scratch_shapes=[pltpu.VMEM((IDX_HEADS, tq, LANES), BF16),
                        pltpu.VMEM((DSA_HEADS, tq, LANES), BF16),
                        pltpu.VMEM((nq, tq, tq), F32),
                        pltpu.VMEM((DSA_HEADS, tq, LANES), F32),
                        pltpu.VMEM((DSA_HEADS, tq, LANES), F32),
                        pltpu.VMEM((DSA_HEADS, tq, LANES), F32)],
        compiler_params=_cparams(("parallel", "arbitrary")),
        name="sparse_attention",
    )(ixq, ixw, dsq, ikk, kv, vk)


def _matmul_kernel(x_ref, w_ref, o_ref):
    o_ref[...] = _dot(x_ref[...].astype(BF16), w_ref[...]).astype(o_ref.dtype)


def _matmul(x, w, tm, out_dtype):
    M, K = x.shape
    N = w.shape[1]
    return pl.pallas_call(
        _matmul_kernel,
        grid=(M // tm,),
        in_specs=[pl.BlockSpec((tm, K), lambda i: (i, 0)), pl.BlockSpec((K, N), lambda i: (0, 0))],
        out_specs=pl.BlockSpec((tm, N), lambda i: (i, 0)),
        out_shape=jax.ShapeDtypeStruct((M, N), out_dtype),
        compiler_params=_cparams(("parallel",)),
        name="matmul",
    )(x, w)


def _memattn_kernel(q_ref, kv_ref, o_ref):
    scale = MEM_HEAD_DIM ** -0.5
    width = MEM_HEADS * MEM_HEAD_DIM
    for h in range(MEM_HEADS):
        sl = slice(h * MEM_HEAD_DIM, (h + 1) * MEM_HEAD_DIM)
        k = kv_ref[:, sl]
        v = kv_ref[:, width + h * MEM_HEAD_DIM:width + (h + 1) * MEM_HEAD_DIM]
        s = _dot_nt(q_ref[:, sl], k) * scale
        p = jnp.exp(s - jnp.max(s, -1, keepdims=True))
        o = _dot(p.astype(BF16), v) / jnp.sum(p, -1, keepdims=True)
        o_ref[:, sl] = o.astype(BF16)


def _memory_attention(memq, mkv, batch, seq, mem_len, tq):
    T = batch * seq
    nq = seq // tq
    return pl.pallas_call(
        _memattn_kernel,
        grid=(batch, nq),
        in_specs=[pl.BlockSpec((tq, 512), lambda b, i: (b * nq + i, 0)),
                  pl.BlockSpec((mem_len, 1024), lambda b, i: (b, 0))],
        out_specs=pl.BlockSpec((tq, 512), lambda b, i: (b * nq + i, 0)),
        out_shape=jax.ShapeDtypeStruct((T, 512), BF16),
        compiler_params=_cparams(("parallel", "parallel")),
        name="memory_attention",
    )(memq, mkv)


def _merge_kernel(x_ref, ya_ref, yb_ref, yc_ref, wg_ref, wa_ref, wb_ref, wc_ref, wo_ref,
                  g_ref, b_ref, h_ref, *, alpha):
    x = x_ref[...]
    xb = x.astype(BF16)
    d = x.shape[1]
    merged = None
    for br, (y_ref, w_ref) in enumerate(((ya_ref, wa_ref), (yb_ref, wb_ref), (yc_ref, wc_ref))):
        gate = jax.nn.sigmoid(_dot(xb, wg_ref[:, br * d:(br + 1) * d]))
        term = gate * _dot(y_ref[...], w_ref[...])
        merged = term if merged is None else merged + term
    mix = _dot(merged.astype(BF16), wo_ref[...])
    h_ref[...] = _layer_norm(alpha * x + mix, g_ref[...], b_ref[...])


def _merge_project_norm(x2d, ya, yb, yc, w_gates, w_ba, w_bb, w_bc, w_o, ln_g, ln_b, alpha, tm):
    T, D = x2d.shape
    row = lambda n: pl.BlockSpec((tm, n), lambda i: (i, 0))
    whole = lambda a: pl.BlockSpec(a.shape, lambda i: (0, 0))
    ws = [w.astype(BF16) for w in (w_gates, w_ba, w_bb, w_bc, w_o)]
    vec = [v.reshape(1, D).astype(F32) for v in (ln_g, ln_b)]
    return pl.pallas_call(
        functools.partial(_merge_kernel, alpha=alpha),
        grid=(T // tm,),
        in_specs=[row(D), row(512), row(512), row(512)] + [whole(w) for w in ws] + [whole(v) for v in vec],
        out_specs=row(D),
        out_shape=jax.ShapeDtypeStruct((T, D), F32),
        compiler_params=_cparams(("parallel",)),
        name="merge_project_norm",
    )(x2d, ya, yb, yc, *ws, *vec)


def _router_kernel(h_ref, wr_ref, bias_ref, eidx_ref, gate_ref, rank_ref, cnt_ref, carry_s, *, tm):
    @pl.when(pl.program_id(0) == 0)
    def _():
        carry_s[...] = jnp.zeros(carry_s.shape, F32)

    neg = -jnp.inf
    logits = _dot_nt(wr_ref[...], h_ref[...].astype(BF16))
    scores = jax.nn.sigmoid(logits)
    biased = scores + bias_ref[...]
    erow = lax.broadcasted_iota(I32, (N_EXPERTS, tm), 0)
    big = jnp.int32(2 ** 30)

    def top1(vals, ids):
        mx = jnp.max(vals, 0, keepdims=True)
        am = jnp.min(jnp.where(vals == mx, ids, big), 0, keepdims=True)
        return mx, am

    gs = []
    ids = lax.broadcasted_iota(I32, (GROUP_SIZE, tm), 0)
    for g in range(N_GROUPS):
        v = biased[g * GROUP_SIZE:(g + 1) * GROUP_SIZE]
        m1, a1 = top1(v, ids)
        m2, _ = top1(jnp.where(ids == a1, neg, v), ids)
        gs.append(m1 + m2)
    gscore = jnp.concatenate(gs, 0)
    grow = lax.broadcasted_iota(I32, (N_GROUPS, tm), 0)
    gsel = jnp.zeros((N_GROUPS, tm), F32)
    for _ in range(TOPK_GROUPS):
        _, ga = top1(gscore, grow)
        hit = grow == ga
        gsel = jnp.where(hit, 1.0, gsel)
        gscore = jnp.where(hit, neg, gscore)
    masked = jnp.concatenate(
        [jnp.where(gsel[g:g + 1] > 0.0, biased[g * GROUP_SIZE:(g + 1) * GROUP_SIZE], neg)
         for g in range(N_GROUPS)], 0)

    eids, ws = [], []
    hot = jnp.zeros((N_EXPERTS, tm), F32)
    for _ in range(TOP_K):
        _, ea = top1(masked, erow)
        hit = erow == ea
        eids.append(ea)
        ws.append(jnp.sum(jnp.where(hit, scores, 0.0), 0, keepdims=True))
        hot = jnp.where(hit, 1.0, hot)
        masked = jnp.where(hit, neg, masked)
    w = jnp.concatenate(ws, 0)
    eidx_ref[...] = jnp.concatenate(eids, 0)
    gate_ref[...] = w / jnp.sum(w, 0, keepdims=True) * ROUTED_SCALE

    r_i =lax.broadcasted_iota(I32, (tm, tm), 0)
    c_i = lax.broadcasted_iota(I32, (tm, tm), 1)
    before = jnp.where(r_i < c_i, 1.0, 0.0).astype(BF16)
    pos = carry_s[...] + _dot(hot.astype(BF16), before)
    rank_ref[...] = jnp.concatenate(
        [jnp.sum(jnp.where(erow == e, pos, 0.0), 0, keepdims=True) for e in eids], 0).astype(I32)
    carry_s[...] = carry_s[...] + jnp.sum(hot, 1, keepdims=True)
    cnt_ref[...] = carry_s[...].astype(I32)


def _route(h2d, w_router, router_bias, tm):
    T, D = h2d.shape
    out = lambda dt: jax.ShapeDtypeStruct((TOP_K, T), dt)
    blk = pl.BlockSpec((TOP_K, tm), lambda i: (0, i))
    return pl.pallas_call(
        functools.partial(_router_kernel, tm=tm),
        grid=(T // tm,),
        in_specs=[pl.BlockSpec((tm, D), lambda i: (i, 0)),
                  pl.BlockSpec((N_EXPERTS, D), lambda i: (0, 0)),
                  pl.BlockSpec((N_EXPERTS, 1), lambda i: (0, 0))],
        out_specs=[blk, blk, blk, pl.BlockSpec((N_EXPERTS, 1), lambda i: (0, 0))],
        out_shape=[out(I32), out(F32), out(I32), jax.ShapeDtypeStruct((N_EXPERTS, 1), I32)],
        scratch_shapes=[pltpu.VMEM((N_EXPERTS, 1), F32)],
        compiler_params=_cparams(("arbitrary",)),
        name="router",
    )(h2d, w_router.T.astype(BF16), router_bias.reshape(N_EXPERTS, 1).astype(F32))


def _dest_kernel(pstart_ref, eidx_ref, rank_ref, dest_ref):
    eidx = eidx_ref[...]

    def body(e, acc):
        return acc + jnp.where(eidx == e, pstart_ref[e], 0)

    dest_ref[...] = lax.fori_loop(0, N_EXPERTS, body, rank_ref[...])


def _destinations(pstart, eidx, rank, tm):
    T = eidx.shape[1]
    blk = lambda: pl.BlockSpec((TOP_K, tm), lambda i, ps: (0, i))
    return pl.pallas_call(
        _dest_kernel,
        grid_spec=pltpu.PrefetchScalarGridSpec(
            num_scalar_prefetch=1, grid=(T // tm,), in_specs=[blk(), blk()], out_specs=blk()),
        out_shape=jax.ShapeDtypeStruct((TOP_K, T), I32),
        compiler_params=_cparams(("parallel",)),
        name="destinations",
    )(pstart, eidx, rank)


def _dispatch_kernel(zfill_ref, dest_hbm, h_ref, xs_hbm, dest_s, hbuf, zbuf, dest_sem, row_sem, z_sem,
                     *, tm, rows):
    i = pl.program_id(0)
    n = pl.num_programs(0)
    nz = zfill_ref.shape[0]

    def dest_copy(blk, slot):
        return pltpu.make_async_copy(dest_hbm.at[blk], dest_s.at[slot], dest_sem.at[slot])

    def rows_done(slot):
        for _ in range(TOP_K):
            pltpu.make_async_copy(hbuf.at[slot], xs_hbm.at[pl.ds(0, tm)], row_sem.at[slot]).wait()

    @pl.when(i == 0)
    def _():
        zbuf[...] = jnp.zeros(zbuf.shape, F32)

        def zcopy(b):
            return pltpu.make_async_copy(zbuf, xs_hbm.at[pl.ds(zfill_ref[b] * rows, rows)], z_sem)

        def start(b, _):
            @pl.when(zfill_ref[b] >= 0)
            def _():
                zcopy(b).start()
            return 0

        def wait(b, _):
            @pl.when(zfill_ref[b] >= 0)
            def _():
                zcopy(b).wait()
            return 0

        lax.fori_loop(0, nz, start, 0)
        lax.fori_loop(0, nz, wait, 0)
        dest_copy(0, 0).start()

    slot = i % 2
    dest_copy(i, slot).wait()

    @pl.when(i + 1 < n)
    def _():
        dest_copy(i + 1, 1 - slot).start()

    @pl.when(i >= 2)
    def _():
        rows_done(slot)

    hbuf[slot] = h_ref[...]

    def body(r, _):
        for k in range(TOP_K):
            d = dest_s[slot, k * tm + r]
            pltpu.make_async_copy(hbuf.at[slot, pl.ds(r, 1)], xs_hbm.at[pl.ds(d, 1)], row_sem.at[slot]).start()
        return 0

    lax.fori_loop(0, tm, body, 0)

    @pl.when(i == n - 1)
    def _():
        rows_done(slot)

        @pl.when(n > 1)
        def _():
            rows_done(1 - slot)


def _token_tiles(a, tm):
    k, T = a.shape
    return a.reshape(k, T // tm, tm).transpose(1, 0, 2).reshape(T // tm, k * tm)


def _dispatch(h2d, dest, zfill, n_blocks, rows, tm):
    T, D = h2d.shape
    return pl.pallas_call(
        functools.partial(_dispatch_kernel, tm=tm, rows=rows),
        grid_spec=pltpu.PrefetchScalarGridSpec(
            num_scalar_prefetch=1,
            grid=(T // tm,),
            in_specs=[pl.BlockSpec(memory_space=pl.ANY),
                      pl.BlockSpec((tm, D), lambda i, z: (i, 0))],
            out_specs=pl.BlockSpec(memory_space=pl.ANY),
            scratch_shapes=[pltpu.SMEM((2, TOP_K * tm), I32),
                            pltpu.VMEM((2, tm, D), F32),
                            pltpu.VMEM((rows, D), F32),
                            pltpu.SemaphoreType.DMA((2,)),
                            pltpu.SemaphoreType.DMA((2,)),
                            pltpu.SemaphoreType.DMA]),
        out_shape=jax.ShapeDtypeStruct((n_blocks * rows, D), F32),
        compiler_params=_cparams(("arbitrary",)),
        name="dispatch",
    )(zfill, _token_tiles(dest, tm), h2d)


def _experts_kernel(blk_e_ref, x_ref, wg_ref, wu_ref, wd_ref, y_ref, wgb, wub, wdb):
    i = pl.program_id(0)
    e = blk_e_ref[i]
    prev = blk_e_ref[jnp.maximum(i - 1, 0)]

    @pl.when((i == 0) | (e != prev))
    def _():
        wgb[...] = wg_ref[0].astype(BF16)
        wub[...] = wu_ref[0].astype(BF16)
        wdb[...] = wd_ref[0].astype(BF16)

    x = x_ref[...].astype(BF16)
    hidden = jax.nn.silu(_dot(x, wgb[...])) * _dot(x, wub[...])
    y_ref[...] = _dot(hidden.astype(BF16), wdb[...])


def _routed_experts(xs, blk_e, w_eg, w_eu, w_ed, rows):
    P, D = xs.shape
    ff = w_eg.shape[-1]
    return pl.pallas_call(
        _experts_kernel,
        grid_spec=pltpu.PrefetchScalarGridSpec(
            num_scalar_prefetch=1,
            grid=(P // rows,),
            in_specs=[pl.BlockSpec((rows, D), lambda i, be: (i, 0)),
                      pl.BlockSpec((1, D, ff), lambda i, be: (be[i], 0, 0)),
                      pl.BlockSpec((1, D, ff), lambda i, be: (be[i], 0, 0)),
                      pl.BlockSpec((1, ff, D), lambda i, be: (be[i], 0, 0))],
            out_specs=pl.BlockSpec((rows, D), lambda i, be: (i, 0)),
            scratch_shapes=[pltpu.VMEM((D, ff), BF16),
                            pltpu.VMEM((D, ff), BF16),
                            pltpu.VMEM((ff, D), BF16)]),
        out_shape=jax.ShapeDtypeStruct((P, D), F32),
        compiler_params=_cparams(("arbitrary",)),
        name="routed_experts",
    )(blk_e, xs, w_eg, w_eu, w_ed)


def _combine_kernel(dest_hbm, ys_hbm, h_ref, gate_ref, wsg_ref, wsu_ref, wsd_ref, g_ref, b_ref, o_ref,
                    dest_s, buf, dest_sem, row_sem, *, tm, alpha):
    i = pl.program_id(0)
    n = pl.num_programs(0)

    def dest_copy(blk, slot):
        return pltpu.make_async_copy(dest_hbm.at[blk], dest_s.at[slot], dest_sem.at[slot])

    def issue_rows(slot):
        def body(r, _):
            for k in range(TOP_K):
                d = dest_s[slot, k * tm + r]
                pltpu.make_async_copy(ys_hbm.at[pl.ds(d, 1)], buf.at[slot, k, pl.ds(r, 1)],
                                      row_sem.at[slot]).start()
            return 0
        lax.fori_loop(0, tm, body, 0)

    @pl.when(i == 0)
    def _():
        dest_copy(0, 0).start()
        dest_copy(0, 0).wait()
        issue_rows(0)

        @pl.when(n > 1)
        def _():
            dest_copy(1, 1).start()

    nxt = (i + 1) % 2

    @pl.when(i + 1 < n)
    def _():
        dest_copy(i + 1, nxt).wait()
        issue_rows(nxt)

    @pl.when(i + 2 < n)
    def _():
        dest_copy(i + 2, i % 2).start()

    h = h_ref[...]
    hb = h.astype(BF16)
    shared = _dot((jax.nn.silu(_dot(hb, wsg_ref[...])) * _dot(hb, wsu_ref[...])).astype(BF16), wsd_ref[...])
    slot = i % 2
    total = alpha * h + shared
    for k in range(TOP_K):
        pltpu.make_async_copy(ys_hbm.at[pl.ds(0, tm)], buf.at[slot, k], row_sem.at[slot]).wait()
    for k in range(TOP_K):
        total = total + gate_ref[:, k:k + 1] * buf[slot, k]
    o_ref[...] = _layer_norm(total, g_ref[...], b_ref[...])


def _combine_shared_norm(h2d, ys, dest, gate, w_sg, w_su, w_sd, ln_g, ln_b, alpha, tm):
    T, D = h2d.shape
    nt = T // tm
    ws = [w.astype(BF16) for w in (w_sg, w_su, w_sd)]
    vec = [v.reshape(1, D).astype(F32) for v in (ln_g, ln_b)]
    whole = lambda a: pl.BlockSpec(a.shape, lambda i: (0, 0))
    return pl.pallas_call(
        functools.partial(_combine_kernel, tm=tm, alpha=alpha),
        grid=(nt,),
        in_specs=[pl.BlockSpec(memory_space=pl.ANY), pl.BlockSpec(memory_space=pl.ANY),
                  pl.BlockSpec((tm, D), lambda i: (i, 0)),
                  pl.BlockSpec((tm, TOP_K), lambda i: (i, 0))] + [whole(w) for w in ws] + [whole(v) for v in vec],
        out_specs=pl.BlockSpec((tm, D), lambda i: (i, 0)),
        out_shape=jax.ShapeDtypeStruct((T, D), F32),
        scratch_shapes=[pltpu.SMEM((2, TOP_K * tm), I32),
                        pltpu.VMEM((2, TOP_K, tm, D), F32),
                        pltpu.SemaphoreType.DMA((2,)),
                        pltpu.SemaphoreType.DMA((2,))],
        compiler_params=_cparams(("arbitrary",)),
        name="combine_shared_norm",
    )(_token_tiles(dest, tm), ys, h2d, gate.T, *ws, *vec)


MOE_ROWS = 256


def _dispatch_plan(counts, n_tokens, rows):
    counts = counts.reshape(N_EXPERTS)
    padded = (counts + rows - 1) // rows * rows
    pend = jnp.cumsum(padded)
    pstart = pend - padded
    n_blocks = (n_tokens * TOP_K + N_EXPERTS * (rows - 1) + rows - 1) // rows
    blk = jnp.arange(n_blocks, dtype=I32)
    blk_e = jnp.minimum(jnp.sum((blk[:, None] * rows >= pend[None, :]).astype(I32), 1), N_EXPERTS - 1)
    last_blk = jnp.where(padded > counts, pend // rows - 1, -1)
    tail_blk = jnp.where(blk * rows >= pend[-1], blk, -1)
    zfill = jnp.concatenate([last_blk, tail_blk]).astype(I32)
    return pstart.astype(I32), blk_e, zfill, n_blocks


def _moe(h1, p, l, alpha):
    eidx, gate, rank, counts = _route(h1, p['w_router'][l], p['router_bias'][l], tm=512)
    pstart, blk_e, zfill, n_blocks = _dispatch_plan(counts, h1.shape[0], MOE_ROWS)
    dest = _destinations(pstart, eidx, rank, tm=2048)
    xs = _dispatch(h1, dest, zfill, n_blocks, MOE_ROWS, tm=256)
    ys = _routed_experts(xs, blk_e, p['w_eg'][l], p['w_eu'][l], p['w_ed'][l], MOE_ROWS)
    return _combine_shared_norm(h1, ys, dest, gate, p['w_sg'][l], p['w_su'][l], p['w_sd'][l],
                                p['ln2_g'][l], p['ln2_b'][l], alpha, tm=128)


def _layer(h2d, mem2d, p, l, depth, batch, seq, mem_len):
    alpha = (2 * depth) ** 0.25
    lam_init = 0.8 - 0.6 * math.exp(-0.3 * l)
    w_in = p['w_in'][l]
    daq, dak, dsq, ixq, kv, vk, ikk, dav, memq, ixw = _in_projection(h2d, w_in, seq, tm=min(512, seq))
    lam = (jnp.exp(jnp.sum(p['lq1'][l].astype(F32) * p['lk1'][l].astype(F32)))
           - jnp.exp(jnp.sum(p['lq2'][l].astype(F32) * p['lk2'][l].astype(F32))) + lam_init)
    ya = _diff_attention(daq, dak, dav, lam, p['subln_g'][l], batch, seq, lam_init, tq=min(512, seq))
    yb = _sparse_attention(ixq, ixw, dsq, ikk, kv, vk, batch, seq, tq=256)
    mkv = _matmul(mem2d, p['w_mem_kv'][l].astype(BF16), tm=mem_len, out_dtype=BF16)
    yc = _memory_attention(memq, mkv, batch, seq, mem_len, tq=min(512, seq))
    h1 = _merge_project_norm(h2d, ya, yb, yc, w_in[:, 3272:], p['w_ba'][l], p['w_bb'][l], p['w_bc'][l],
                             p['w_o'][l], p['ln1_g'][l], p['ln1_b'][l], alpha, tm=256)
    return _moe(h1, p, l, alpha)


def kernel(x, mem, w_in, da_lambda_q1, da_lambda_k1, da_lambda_q2, da_lambda_k2, da_subln_g, w_mem_kv, w_branch_a, w_branch_b, w_branch_c, w_out, ln1_g, ln1_b, w_router, router_bias, w_exp_gate, w_exp_up, w_exp_down, w_sh_gate, w_sh_up, w_sh_down, ln2_g, ln2_b):
    batch, seq, d = x.shape
    mem_len = mem.shape[1]
    depth = w_in.shape[0]
    p = dict(w_in=w_in, lq1=da_lambda_q1, lk1=da_lambda_k1, lq2=da_lambda_q2, lk2=da_lambda_k2,
             subln_g=da_subln_g, w_mem_kv=w_mem_kv, w_ba=w_branch_a, w_bb=w_branch_b, w_bc=w_branch_c,
             w_o=w_out, ln1_g=ln1_g, ln1_b=ln1_b, w_router=w_router, router_bias=router_bias,
             w_eg=w_exp_gate, w_eu=w_exp_up, w_ed=w_exp_down, w_sg=w_sh_gate, w_su=w_sh_up,
             w_sd=w_sh_down, ln2_g=ln2_g, ln2_b=ln2_b)
    h = x.reshape(batch * seq, d)
    mem2d = mem.reshape(batch * mem_len, d)
    for l in range(depth):
        h = _layer(h, mem2d, p, l, depth, batch, seq, mem_len)
    return h.reshape(batch, seq, d)
```

```python
import functools
import math

import jax
import jax.numpy as jnp
import numpy as np
from jax import lax
from jax.experimental import pallas as pl
from jax.experimental.pallas import tpu as pltpu

F32 = jnp.float32
BF16 = jnp.bfloat16
I32 = jnp.int32

LANES = 128
ROPE_THETA = 10000.0
LN_EPS = 1e-5

DA_HEADS = 4
DA_HEAD_DIM = 64
DSA_HEADS = 8
DSA_HEAD_DIM = 64
IDX_HEADS = 8
IDX_ROWS = 16
DSA_TOPK_MAX = 256
MEM_HEADS = 4
MEM_HEAD_DIM = 128
N_BRANCHES = 3

N_EXPERTS = 256
TOP_K = 8
N_GROUPS = 8
TOPK_GROUPS = 4
GROUP_SIZE = N_EXPERTS // N_GROUPS
ROUTED_SCALE = 2.5

NEG_BIG = -1e30
INT_MIN = -(2 ** 31)

VMEM_LIMIT = 56 * 1024 * 1024


def _cparams(sem):
    return pltpu.CompilerParams(dimension_semantics=sem, vmem_limit_bytes=VMEM_LIMIT)


def _dot(a, b):
    return jnp.dot(a, b, preferred_element_type=F32)


def _dot_nt(a, b):
    return lax.dot_general(a, b, (((1,), (1,)), ((), ())), preferred_element_type=F32)


def _layer_norm(x, g, b):
    mu = jnp.mean(x, -1, keepdims=True)
    xc = x - mu
    var = jnp.mean(xc * xc, -1, keepdims=True)
    return xc * lax.rsqrt(var + LN_EPS) * g + b


def _inproj_kernel(x_ref, wr_ref, wn_ref, wixt_ref, tab_ref,
                   daq_ref, dak_ref, dsq_ref, ixq_ref, kv_ref, vk_ref, ikk_ref,
                   dav_ref, memq_ref, ixwt_ref):
    xb = x_ref[...].astype(BF16)

    def rope(t, kind):
        c = tab_ref[3 * kind]
        sa = tab_ref[3 * kind + 1]
        sb = tab_ref[3 * kind + 2]
        return t * c + pltpu.roll(t, 96, 1) * sa + pltpu.roll(t, 32, 1) * sb

    groups = ((daq_ref, 0.125), (dak_ref, 1.0), (dsq_ref, 0.125), (ixq_ref, 1.0))
    for gi, (ref, scale) in enumerate(groups):
        y = _dot(xb, wr_ref[:, gi * 512:(gi + 1) * 512])
        for c in range(4):
            r = rope(y[:, c * LANES:(c + 1) * LANES], 0)
            if scale != 1.0:
                r = r * scale
            ref[:, c * LANES:(c + 1) * LANES] = r.astype(BF16)
    y = _dot(xb, wr_ref[:, 2048:2432])
    kv_ref[...] = rope(y[:, 0:128], 1).astype(BF16)
    vk_ref[...] = rope(y[:, 128:256], 2).astype(BF16)
    ikk_ref[...] = rope(y[:, 256:384], 0).astype(BF16)
    y = _dot(xb, wn_ref[...])
    dav_ref[...] = y[:, 0:512].astype(BF16)
    memq_ref[...] = y[:, 512:1024].astype(BF16)
    ixwt_ref[...] = _dot_nt(wixt_ref[...], xb)


def _rope_tables(seq):
    dim = 64
    inv = 1.0 / (ROPE_THETA ** (jnp.arange(0, dim, 2, dtype=F32) / dim))
    ang = jnp.arange(seq, dtype=F32)[:, None] * inv[None, :]
    ang = jnp.concatenate([ang, ang], -1)
    cos, sin = jnp.cos(ang), jnp.sin(ang)
    local = jnp.arange(dim)[None, :]
    sa = jnp.where(local < dim // 2, -sin, 0.0)
    sb = jnp.where(local >= dim // 2, sin, 0.0)
    one, zero = jnp.ones_like(cos), jnp.zeros_like(cos)
    cat = lambda a, b: jnp.concatenate([a, b], -1)
    return jnp.stack([cat(cos, cos), cat(sa, sa), cat(sb, sb),
                      cat(cos, one), cat(sa, zero), cat(sb, zero),
                      cat(one, cos), cat(zero, sa), cat(zero, sb)], 0)


def _in_projection(x2d, w_in, seq, tm):
    T, D = x2d.shape
    o = np.cumsum([0, 512, 512, 512, 512, 64, 64, 512, 64, 8, 512]).tolist()
    col = lambda i: w_in[:, o[i]:o[i + 1]]
    da_q, da_k, da_v, ds_q, ds_k, ds_v, ix_q, ix_k, ix_w, mem_q = [col(i) for i in range(10)]
    ds_q = ds_q.reshape(D, DSA_HEADS // 2, 2, DSA_HEAD_DIM)[:, :, ::-1, :].reshape(D, 512)
    w_rope = jnp.concatenate([da_q, da_k, ds_q, ix_q, ds_k, ds_v, ds_v, ds_k, ix_k, ix_k], 1).astype(BF16)
    w_plain = jnp.concatenate([da_v, mem_q], 1).astype(BF16)
    w_ixt = jnp.concatenate([ix_w.T, jnp.zeros((IDX_ROWS - IDX_HEADS, D), w_in.dtype)], 0).astype(BF16)
    tabs = _rope_tables(seq)
    nseq = seq // tm
    bf = lambda n: jax.ShapeDtypeStruct((T, n), BF16)
    row = lambda n: pl.BlockSpec((tm, n), lambda i: (i, 0))
    return pl.pallas_call(
        _inproj_kernel,
        grid=(T // tm,),
        in_specs=[row(D),
                  pl.BlockSpec(w_rope.shape, lambda i: (0, 0)),
                  pl.BlockSpec(w_plain.shape, lambda i: (0, 0)),
                  pl.BlockSpec(w_ixt.shape, lambda i: (0, 0)),
                  pl.BlockSpec((9, tm, LANES), lambda i: (0, i % nseq, 0))],
        out_specs=[row(512), row(512), row(512), row(512), row(128), row(128), row(128),
                   row(512), row(512), pl.BlockSpec((IDX_ROWS, tm), lambda i: (0, i))],
        out_shape=[bf(512), bf(512), bf(512), bf(512), bf(128), bf(128), bf(128),
                   bf(512), bf(512), jax.ShapeDtypeStruct((IDX_ROWS, T), F32)],
        compiler_params=_cparams(("parallel",)),
        name="in_projection",
    )(x2d, w_rope, w_plain, w_ixt, tabs)


def _diffattn_kernel(lam_ref, q_ref, k_ref, v_ref, g_ref, o_ref, *, tq, lam_init):
    i = pl.program_id(2)
    lam = lam_ref[0]
    q = q_ref[...]
    lane = lax.broadcasted_iota(I32, q.shape, 1)
    zero = jnp.zeros_like(q)
    qs = (jnp.where(lane < DA_HEAD_DIM, q, zero), jnp.where(lane >= DA_HEAD_DIM, q, zero))
    row = lax.broadcasted_iota(I32, (tq, tq), 0)
    col = lax.broadcasted_iota(I32, (tq, tq), 1)

    def step(j, carry, diagonal):
        off = pl.multiple_of(j * tq, tq)
        kj = k_ref[pl.ds(off, tq), :]
        vj = v_ref[pl.ds(off, tq), :]
        out = []
        for mp in range(2):
            m, l, a = carry[3 * mp:3 * mp + 3]
            s = _dot_nt(qs[mp], kj)
            if diagonal:
                s = jnp.where(col <= row, s, -jnp.inf)
            mn = jnp.maximum(m, jnp.max(s, -1, keepdims=True))
            p = jnp.exp(s - mn)
            al = jnp.exp(m - mn)
            l = al * l + jnp.sum(p, -1, keepdims=True)
            a = al * a + _dot(p.astype(BF16), vj)
            out += [mn, l, a]
        return tuple(out)

    init = (jnp.full((tq, 1), -jnp.inf, F32), jnp.zeros((tq, 1), F32), jnp.zeros((tq, LANES), F32)) * 2
    carry = lax.fori_loop(0, i, lambda j, c: step(j, c, False), init)
    m1, l1, a1, m2, l2, a2 = step(i, carry, True)
    o = a1 / l1 - lam * (a2 / l2)
    y = o * lax.rsqrt(jnp.mean(o * o, -1, keepdims=True) + LN_EPS) * g_ref[...] * (1.0 - lam_init)
    o_ref[...] = y.astype(BF16)


def _diff_attention(daq, dak, dav, lam, subln_g, batch, seq, lam_init, tq):
    T = batch * seq
    nq = seq // tq
    return pl.pallas_call(
        functools.partial(_diffattn_kernel, tq=tq, lam_init=lam_init),
        grid=(batch, DA_HEADS, nq),
        in_specs=[pl.BlockSpec(memory_space=pltpu.SMEM),
                  pl.BlockSpec((tq, LANES), lambda b, h, i: (b * nq + i, h)),
                  pl.BlockSpec((seq, LANES), lambda b, h, i: (b, h)),
                  pl.BlockSpec((seq, LANES), lambda b, h, i: (b, h)),
                  pl.BlockSpec((1, LANES), lambda b, h, i: (0, 0))],
        out_specs=pl.BlockSpec((tq, LANES), lambda b, h, i: (b * nq + i, h)),
        out_shape=jax.ShapeDtypeStruct((T, DA_HEADS * LANES), BF16),
        compiler_params=_cparams(("parallel", "parallel", "arbitrary")),
        name="diff_attention",
    )(lam.reshape(1), daq, dak, dav, subln_g.reshape(1, LANES).astype(F32))


def _tree(op, xs):
    xs = list(xs)
    while len(xs) > 1:
        xs = [op(xs[k], xs[k + 1]) if k + 1 < len(xs) else xs[k] for k in range(0, len(xs), 2)]
    return xs[0]


def _dsa_kernel(ixq_ref, ixwt_ref, dsq_ref, ikk_ref, kv_ref, vk_ref, o_ref,
                qi_s, qd_s, sc_s, mx_s, l_s, acc_s, *, tq, topk):
    i = pl.program_id(1)
    nblk = i + 1
    tk = tq
    nslab = tk // 8
    lane = lax.broadcasted_iota(I32, (tq, LANES), 1)
    lower = lane < DSA_HEAD_DIM
    for h in range(IDX_HEADS):
        sl = slice((h // 2) * LANES, (h // 2 + 1) * LANES)
        keep = lower if h % 2 == 0 else jnp.logical_not(lower)
        blk = ixq_ref[:, sl]
        qi_s[h] = jnp.where(keep, blk, jnp.zeros_like(blk))
        blk = dsq_ref[:, sl]
        qd_s[h] = jnp.where(keep, blk, jnp.zeros_like(blk))
    krow = lax.broadcasted_iota(I32, (tk, tq), 0)
    qcol = lax.broadcasted_iota(I32, (tk, tq), 1)
    sub8 = lax.broadcasted_iota(I32, (8, tq), 0)

    def score_chunk(j, _):
        off = pl.multiple_of(j * tk, tk)
        ik = ikk_ref[pl.ds(off, tk), :]
        sc = jnp.zeros((tk, tq), F32)
        for h in range(IDX_HEADS):
            sc = sc + ixwt_ref[h:h + 1, :] * jnp.maximum(_dot_nt(ik, qi_s[h]), 0.0)
        sc_s[j] = jnp.where(j * tk + krow <= i * tq + qcol, sc, -jnp.inf)
        return 0

    lax.fori_loop(0, nblk, score_chunk, 0)

    n_acc = 4

    def count(pred):
        def body(j, accs):
            accs = list(accs)
            for r in range(nslab):
                hit = pred(sc_s[j, r * 8:(r + 1) * 8, :], j * tk + r * 8 + sub8)
                accs[r % n_acc] = accs[r % n_acc] + hit.astype(F32)
            return tuple(accs)
        accs = lax.fori_loop(0, nblk, body, tuple(jnp.zeros((8, tq), F32) for _ in range(n_acc)))
        return jnp.sum(_tree(jnp.add, accs), 0, keepdims=True)

    def rows8(v):
        return jnp.broadcast_to(v, (8, tq))

    def as_score(key):
        return pltpu.bitcast(key ^ ((key >> 31) & 0x7FFFFFFF), F32)

    kf = float(topk)
    zero_i = jnp.zeros((1, tq), I32)
    c0 = count(lambda s, kpos: s >= 0.0)
    tau = jnp.where(c0 >= kf, zero_i, jnp.full((1, tq), INT_MIN, I32))

    def tau_bit(b, tau):
        cand = tau | (jnp.int32(1) << (30 - b))
        cb = rows8(as_score(cand))
        c = count(lambda s, kpos: s >= cb)
        return jnp.where(c >= kf, cand, tau)

    tau = lax.fori_loop(0, 31, tau_bit, tau)
    tau_f = jnp.where(tau == INT_MIN, -jnp.inf, as_score(tau))
    tau8 = rows8(tau_f)
    c_ge = count(lambda s, kpos: s >= tau8)

    def tie_cut():
        need = kf - count(lambda s, kpos: s > tau8)

        def cut_bit(b, cut):
            cand = cut | (jnp.int32(1) << (12 - b))
            cb = rows8(cand)
            c = count(lambda s, kpos: (s == tau8) & (kpos < cb))
            return jnp.where(c <= need, cand, cut)

        return lax.fori_loop(0, 13, cut_bit, zero_i)

    cut = lax.cond(jnp.max(c_ge) > kf, tie_cut, lambda: jnp.full((1, tq), 2 ** 13, I32))

    mx_s[...] = jnp.full(mx_s.shape, NEG_BIG, F32)
    l_s[...] = jnp.zeros(l_s.shape, F32)
    acc_s[...] = jnp.zeros(acc_s.shape, F32)
    nlt = tk // LANES

    def lane_tiles(a):
        return [a[:, c * LANES:(c + 1) * LANES] for c in range(nlt)]

    def bcast(v):
        return jnp.broadcast_to(v, (v.shape[0], LANES))

    def row_max(j, _):
        off = pl.multiple_of(j * tk, tk)
        sc = sc_s[j]
        kpos = j * tk + krow
        sel = (sc > tau_f) | ((sc == tau_f) & (kpos < cut))
        sel = sel & (kpos <= i * tq + qcol)
        bias = jnp.where(sel, 0.0, NEG_BIG).T
        sc_s[j] = bias
        kvj = kv_ref[pl.ds(off, tk), :]
        vkj = vk_ref[pl.ds(off, tk), :]
        for h in range(DSA_HEADS):
            s = _dot_nt(qd_s[h], kvj if h % 2 == 0 else vkj) + bias
            mx_s[h] = functools.reduce(jnp.maximum, lane_tiles(s), mx_s[h])
        return 0

    lax.fori_loop(0, nblk, row_max, 0)
    for h in range(DSA_HEADS):
        mx_s[h] = bcast(jnp.max(mx_s[h], -1, keepdims=True))

    def attend(j, _):
        off = pl.multiple_of(j * tk, tk)
        bias = sc_s[j]
        kvj = kv_ref[pl.ds(off, tk), :]
        vkj = vk_ref[pl.ds(off, tk), :]
        for h in range(DSA_HEADS):
            kk = kvj if h % 2 == 0 else vkj
            s = _dot_nt(qd_s[h], kk) + bias
            m = mx_s[h]
            ps = [jnp.exp(t - m) for t in lane_tiles(s)]
            l_s[h] = functools.reduce(jnp.add, ps, l_s[h])
            acc_s[h] = acc_s[h] + _dot(jnp.concatenate(ps, 1).astype(BF16), kk)
        return 0

    lax.fori_loop(0, nblk, attend, 0)

    def out(h):
        return acc_s[h] / jnp.sum(l_s[h], -1, keepdims=True)

    for pr in range(DSA_HEADS // 2):
        o_ref[:, pr * LANES:(pr + 1) * LANES] = jnp.where(lower, out(2 * pr + 1), out(2 * pr)).astype(BF16)


def _sparse_attention(ixq, ixwt, dsq, ikk, kv, vk, batch, seq, tq):
    T = batch * seq
    nq = seq // tq
    topk = min(DSA_TOPK_MAX, seq // 4)
    tile = lambda n: pl.BlockSpec((tq, n), lambda b, i: (b * nq + i, 0))
    full = pl.BlockSpec((seq, LANES), lambda b, i: (b, 0))
    return pl.pallas_call(
        functools.partial(_dsa_kernel, tq=tq, topk=topk),
        grid=(batch, nq),
        in_specs=[tile(512), pl.BlockSpec((ixwt.shape[0], tq), lambda b, i: (0, b * nq + i)), tile(512),
                  full, full, full],
        out_specs=tile(512),
        out_shape=jax.ShapeDtypeStruct((T, 512), BF16),
        scratch_shapes=[pltpu.VMEM((IDX_HEADS, tq, LANES), BF16),
                        pltpu.VMEM((DSA_HEADS, tq, LANES), BF16),
                        pltpu.VMEM((nq, tq, tq), F32),
                        pltpu.VMEM((DSA_HEADS, tq, LANES), F32),
                        pltpu.VMEM((DSA_HEADS, tq, LANES), F32),
                        pltpu.VMEM((DSA_HEADS, tq, LANES), F32)],
        compiler_params=_cparams(("parallel", "arbitrary")),
        name="sparse_attention",
    )(ixq, ixwt, dsq, ikk, kv, vk)


def _matmul_kernel(x_ref, w_ref, o_ref):
    o_ref[...] = _dot(x_ref[...].astype(BF16), w_ref[...]).astype(o_ref.dtype)


def _matmul(x, w, tm, out_dtype):
    M, K = x.shape
    N = w.shape[1]
    return pl.pallas_call(
        _matmul_kernel,
        grid=(M // tm,),
        in_specs=[pl.BlockSpec((tm, K), lambda i: (i, 0)), pl.BlockSpec((K, N), lambda i: (0, 0))],
        out_specs=pl.BlockSpec((tm, N), lambda i: (i, 0)),
        out_shape=jax.ShapeDtypeStruct((M, N), out_dtype),
        compiler_params=_cparams(("parallel",)),
        name="matmul",
    )(x, w)


def _memattn_kernel(q_ref, kv_ref, o_ref):
    scale = MEM_HEAD_DIM ** -0.5
    width = MEM_HEADS * MEM_HEAD_DIM
    for h in range(MEM_HEADS):
        sl = slice(h * MEM_HEAD_DIM, (h + 1) * MEM_HEAD_DIM)
        k = kv_ref[:, sl]
        v = kv_ref[:, width + h * MEM_HEAD_DIM:width + (h + 1) * MEM_HEAD_DIM]
        s = _dot_nt(q_ref[:, sl], k) * scale
        p = jnp.exp(s - jnp.max(s, -1, keepdims=True))
        o = _dot(p.astype(BF16), v) / jnp.sum(p, -1, keepdims=True)
        o_ref[:, sl] = o.astype(BF16)


def _memory_attention(memq, mkv, batch, seq, mem_len, tq):
    T = batch * seq
    nq = seq // tq
    return pl.pallas_call(
        _memattn_kernel,
        grid=(batch, nq),
        in_specs=[pl.BlockSpec((tq, 512), lambda b, i: (b * nq + i, 0)),
                  pl.BlockSpec((mem_len, 1024), lambda b, i: (b, 0))],
        out_specs=pl.BlockSpec((tq, 512), lambda b, i: (b * nq + i, 0)),
        out_shape=jax.ShapeDtypeStruct((T, 512), BF16),
        compiler_params=_cparams(("parallel", "parallel")),
        name="memory_attention",
    )(memq, mkv)


def _merge_kernel(x_ref, ya_ref, yb_ref, yc_ref, wg_ref, wa_ref, wb_ref, wc_ref, wo_ref,
                  g_ref, b_ref, h_ref, *, alpha):
    x = x_ref[...]
    xb = x.astype(BF16)
    d = x.shape[1]
    merged = None
    for br, (y_ref, w_ref) in enumerate(((ya_ref, wa_ref), (yb_ref, wb_ref), (yc_ref, wc_ref))):
        gate = jax.nn.sigmoid(_dot(xb, wg_ref[:, br * d:(br + 1) * d]))
        term = gate * _dot(y_ref[...], w_ref[...])
        merged = term if merged is None else merged + term
    mix = _dot(merged.astype(BF16), wo_ref[...])
    h_ref[...] = _layer_norm(alpha * x + mix, g_ref[...], b_ref[...])


def _merge_project_norm(x2d, ya, yb, yc, w_gates, w_ba, w_bb, w_bc, w_o, ln_g, ln_b, alpha, tm):
    T, D = x2d.shape
    row = lambda n: pl.BlockSpec((tm, n), lambda i: (i, 0))
    whole = lambda a: pl.BlockSpec(a.shape, lambda i: (0, 0))
    ws = [w.astype(BF16) for w in (w_gates, w_ba, w_bb, w_bc, w_o)]
    vec = [v.reshape(1, D).astype(F32) for v in (ln_g, ln_b)]
    return pl.pallas_call(
        functools.partial(_merge_kernel, alpha=alpha),
        grid=(T // tm,),
        in_specs=[row(D), row(512), row(512), row(512)] + [whole(w) for w in ws] + [whole(v) for v in vec],
        out_specs=row(D),
        out_shape=jax.ShapeDtypeStruct((T, D), F32),
        compiler_params=_cparams(("parallel",)),
        name="merge_project_norm",
    )(x2d, ya, yb, yc, *ws, *vec)


def _router_kernel(h_ref, wr_ref, bias_ref, eidx_ref, gate_ref, rank_ref, cnt_ref, carry_s, *, tm):
    @pl.when(pl.program_id(0) == 0)
    def _():
        carry_s[...] = jnp.zeros(carry_s.shape, F32)

    neg = -jnp.inf
    logits = _dot_nt(wr_ref[...], h_ref[...].astype(BF16))
    scores = jax.nn.sigmoid(logits)
    biased = scores + bias_ref[...]
    erow = lax.broadcasted_iota(I32, (N_EXPERTS, tm), 0)
    big = jnp.int32(2 ** 30)

    def top1(vals, ids):
        mx = jnp.max(vals, 0, keepdims=True)
        am = jnp.min(jnp.where(vals == mx, ids, big), 0, keepdims=True)
        return mx, am

    gs = []
    ids = lax.broadcasted_iota(I32, (GROUP_SIZE, tm), 0)
    for g in range(N_GROUPS):
        v = biased[g * GROUP_SIZE:(g + 1) * GROUP_SIZE]
        m1, a1 = top1(v, ids)
        m2, _ = top1(jnp.where(ids == a1, neg, v), ids)
        gs.append(m1 + m2)
    gscore = jnp.concatenate(gs, 0)
    grow = lax.broadcasted_iota(I32, (N_GROUPS, tm), 0)
    gsel = jnp.zeros((N_GROUPS, tm), F32)
    for _ in range(TOPK_GROUPS):
        _, ga = top1(gscore, grow)
        hit = grow == ga
        gsel = jnp.where(hit, 1.0, gsel)
        gscore = jnp.where(hit, neg, gscore)
    masked = jnp.concatenate(
        [jnp.where(gsel[g:g + 1] > 0.0, biased[g * GROUP_SIZE:(g + 1) * GROUP_SIZE], neg)
         for g in range(N_GROUPS)], 0)

    eids, ws = [], []
    hot = jnp.zeros((N_EXPERTS, tm), F32)
    for _ in range(TOP_K):
        _, ea = top1(masked, erow)
        hit = erow == ea
        eids.append(ea)
        ws.append(jnp.sum(jnp.where(hit, scores, 0.0), 0, keepdims=True))
        hot = jnp.where(hit, 1.0, hot)
        masked = jnp.where(hit, neg, masked)
    w = jnp.concatenate(ws, 0)
    eidx_ref[...] = jnp.concatenate(eids, 0)
    gate_ref[...] = w / jnp.sum(w, 0, keepdims=True) * ROUTED_SCALE

    r_i =lax.broadcasted_iota(I32, (tm, tm), 0)
    c_i = lax.broadcasted_iota(I32, (tm, tm), 1)
    before = jnp.where(r_i < c_i, 1.0, 0.0).astype(BF16)
    pos = carry_s[...] + _dot(hot.astype(BF16), before)
    rank_ref[...] = jnp.concatenate(
        [jnp.sum(jnp.where(erow == e, pos, 0.0), 0, keepdims=True) for e in eids], 0).astype(I32)
    carry_s[...] = carry_s[...] + jnp.sum(hot, 1, keepdims=True)
    cnt_ref[...] = carry_s[...].astype(I32)


def _route(h2d, w_router, router_bias, tm):
    T, D = h2d.shape
    out = lambda dt: jax.ShapeDtypeStruct((TOP_K, T), dt)
    blk = pl.BlockSpec((TOP_K, tm), lambda i: (0, i))
    return pl.pallas_call(
        functools.partial(_router_kernel, tm=tm),
        grid=(T // tm,),
        in_specs=[pl.BlockSpec((tm, D), lambda i: (i, 0)),
                  pl.BlockSpec((N_EXPERTS, D), lambda i: (0, 0)),
                  pl.BlockSpec((N_EXPERTS, 1), lambda i: (0, 0))],
        out_specs=[blk, blk, blk, pl.BlockSpec((N_EXPERTS, 1), lambda i: (0, 0))],
        out_shape=[out(I32), out(F32), out(I32), jax.ShapeDtypeStruct((N_EXPERTS, 1), I32)],
        scratch_shapes=[pltpu.VMEM((N_EXPERTS, 1), F32)],
        compiler_params=_cparams(("arbitrary",)),
        name="router",
    )(h2d, w_router.T.astype(BF16), router_bias.reshape(N_EXPERTS, 1).astype(F32))


def _dest_kernel(pstart_ref, eidx_ref, rank_ref, dest_ref):
    eidx = eidx_ref[...]

    def body(e, acc):
        return acc + jnp.where(eidx == e, pstart_ref[e], 0)

    dest_ref[...] = lax.fori_loop(0, N_EXPERTS, body, rank_ref[...])


def _destinations(pstart, eidx, rank, tm):
    T = eidx.shape[1]
    blk = lambda: pl.BlockSpec((TOP_K, tm), lambda i, ps: (0, i))
    return pl.pallas_call(
        _dest_kernel,
        grid_spec=pltpu.PrefetchScalarGridSpec(
            num_scalar_prefetch=1, grid=(T // tm,), in_specs=[blk(), blk()], out_specs=blk()),
        out_shape=jax.ShapeDtypeStruct((TOP_K, T), I32),
        compiler_params=_cparams(("parallel",)),
        name="destinations",
    )(pstart, eidx, rank)


def _dispatch_kernel(zfill_ref, dest_hbm, h_ref, xs_hbm, dest_s, hbuf, zbuf, dest_sem, row_sem, z_sem,
                     *, tm, rows):
    i = pl.program_id(0)
    n = pl.num_programs(0)
    nz = zfill_ref.shape[0]

    def dest_copy(blk, slot):
        return pltpu.make_async_copy(dest_hbm.at[blk], dest_s.at[slot], dest_sem.at[slot])

    def rows_done(slot):
        for _ in range(TOP_K):
            pltpu.make_async_copy(hbuf.at[slot], xs_hbm.at[pl.ds(0, tm)], row_sem.at[slot]).wait()

    @pl.when(i == 0)
    def _():
        zbuf[...] = jnp.zeros(zbuf.shape, F32)

        def zcopy(b):
            return pltpu.make_async_copy(zbuf, xs_hbm.at[pl.ds(zfill_ref[b] * rows, rows)], z_sem)

        def start(b, _):
            @pl.when(zfill_ref[b] >= 0)
            def _():
                zcopy(b).start()
            return 0

        def wait(b, _):
            @pl.when(zfill_ref[b] >= 0)
            def _():
                zcopy(b).wait()
            return 0

        lax.fori_loop(0, nz, start, 0)
        lax.fori_loop(0, nz, wait, 0)
        dest_copy(0, 0).start()

    slot = i % 2
    dest_copy(i, slot).wait()

    @pl.when(i + 1 < n)
    def _():
        dest_copy(i + 1, 1 - slot).start()

    @pl.when(i >= 2)
    def _():
        rows_done(slot)

    hbuf[slot] = h_ref[...]

    def body(r, _):
        for k in range(TOP_K):
            d = dest_s[slot, k * tm + r]
            pltpu.make_async_copy(hbuf.at[slot, pl.ds(r, 1)], xs_hbm.at[pl.ds(d, 1)], row_sem.at[slot]).start()
        return 0

    lax.fori_loop(0, tm, body, 0)

    @pl.when(i == n - 1)
    def _():
        rows_done(slot)

        @pl.when(n > 1)
        def _():
            rows_done(1 - slot)


def _token_tiles(a, tm):
    k, T = a.shape
    return a.reshape(k, T // tm, tm).transpose(1, 0, 2).reshape(T // tm, k * tm)


def _dispatch(h2d, dest, zfill, n_blocks, rows, tm):
    T, D = h2d.shape
    return pl.pallas_call(
        functools.partial(_dispatch_kernel, tm=tm, rows=rows),
        grid_spec=pltpu.PrefetchScalarGridSpec(
            num_scalar_prefetch=1,
            grid=(T // tm,),
            in_specs=[pl.BlockSpec(memory_space=pl.ANY),
                      pl.BlockSpec((tm, D), lambda i, z: (i, 0))],
            out_specs=pl.BlockSpec(memory_space=pl.ANY),
            scratch_shapes=[pltpu.SMEM((2, TOP_K * tm), I32),
                            pltpu.VMEM((2, tm, D), F32),
                            pltpu.VMEM((rows, D), F32),
                            pltpu.SemaphoreType.DMA((2,)),
                            pltpu.SemaphoreType.DMA((2,)),
                            pltpu.SemaphoreType.DMA]),
        out_shape=jax.ShapeDtypeStruct((n_blocks * rows, D), F32),
        compiler_params=_cparams(("arbitrary",)),
        name="dispatch",
    )(zfill, _token_tiles(dest, tm), h2d)


def _experts_kernel(blk_e_ref, x_ref, wg_ref, wu_ref, wd_ref, y_ref, wgb, wub, wdb):
    i = pl.program_id(0)
    e = blk_e_ref[i]
    prev = blk_e_ref[jnp.maximum(i - 1, 0)]

    @pl.when((i == 0) | (e != prev))
    def _():
        wgb[...] = wg_ref[0].astype(BF16)
        wub[...] = wu_ref[0].astype(BF16)
        wdb[...] = wd_ref[0].astype(BF16)

    x = x_ref[...].astype(BF16)
    hidden = jax.nn.silu(_dot(x, wgb[...])) * _dot(x, wub[...])
    y_ref[...] = _dot(hidden.astype(BF16), wdb[...])


def _routed_experts(xs, blk_e, w_eg, w_eu, w_ed, rows):
    P, D = xs.shape
    ff = w_eg.shape[-1]
    return pl.pallas_call(
        _experts_kernel,
        grid_spec=pltpu.PrefetchScalarGridSpec(
            num_scalar_prefetch=1,
            grid=(P // rows,),
            in_specs=[pl.BlockSpec((rows, D), lambda i, be: (i, 0)),
                      pl.BlockSpec((1, D, ff), lambda i, be: (be[i], 0, 0)),
                      pl.BlockSpec((1, D, ff), lambda i, be: (be[i], 0, 0)),
                      pl.BlockSpec((1, ff, D), lambda i, be: (be[i], 0, 0))],
            out_specs=pl.BlockSpec((rows, D), lambda i, be: (i, 0)),
            scratch_shapes=[pltpu.VMEM((D, ff), BF16),
                            pltpu.VMEM((D, ff), BF16),
                            pltpu.VMEM((ff, D), BF16)]),
        out_shape=jax.ShapeDtypeStruct((P, D), F32),
        compiler_params=_cparams(("arbitrary",)),
        name="routed_experts",
    )(blk_e, xs, w_eg, w_eu, w_ed)


def _combine_kernel(dest_hbm, ys_hbm, h_ref, gate_ref, wsg_ref, wsu_ref, wsd_ref, g_ref, b_ref, o_ref,
                    dest_s, buf, dest_sem, row_sem, *, tm, alpha):
    i = pl.program_id(0)
    n = pl.num_programs(0)

    def dest_copy(blk, slot):
        return pltpu.make_async_copy(dest_hbm.at[blk], dest_s.at[slot], dest_sem.at[slot])

    def issue_rows(slot):
        def body(r, _):
            for k in range(TOP_K):
                d = dest_s[slot, k * tm + r]
                pltpu.make_async_copy(ys_hbm.at[pl.ds(d, 1)], buf.at[slot, k, pl.ds(r, 1)],
                                      row_sem.at[slot]).start()
            return 0
        lax.fori_loop(0, tm, body, 0)

    @pl.when(i == 0)
    def _():
        dest_copy(0, 0).start()
        dest_copy(0, 0).wait()
        issue_rows(0)

        @pl.when(n > 1)
        def _():
            dest_copy(1, 1).start()

    nxt = (i + 1) % 2

    @pl.when(i + 1 < n)
    def _():
        dest_copy(i + 1, nxt).wait()
        issue_rows(nxt)

    @pl.when(i + 2 < n)
    def _():
        dest_copy(i + 2, i % 2).start()

    h = h_ref[...]
    hb = h.astype(BF16)
    shared = _dot((jax.nn.silu(_dot(hb, wsg_ref[...])) * _dot(hb, wsu_ref[...])).astype(BF16), wsd_ref[...])
    slot = i % 2
    total = alpha * h + shared
    for k in range(TOP_K):
        pltpu.make_async_copy(ys_hbm.at[pl.ds(0, tm)], buf.at[slot, k], row_sem.at[slot]).wait()
    for k in range(TOP_K):
        total = total + gate_ref[:, k:k + 1] * buf[slot, k]
    o_ref[...] = _layer_norm(total, g_ref[...], b_ref[...])


def _combine_shared_norm(h2d, ys, dest, gate, w_sg, w_su, w_sd, ln_g, ln_b, alpha, tm):
    T, D = h2d.shape
    nt = T // tm
    ws = [w.astype(BF16) for w in (w_sg, w_su, w_sd)]
    vec = [v.reshape(1, D).astype(F32) for v in (ln_g, ln_b)]
    whole = lambda a: pl.BlockSpec(a.shape, lambda i: (0, 0))
    return pl.pallas_call(
        functools.partial(_combine_kernel, tm=tm, alpha=alpha),
        grid=(nt,),
        in_specs=[pl.BlockSpec(memory_space=pl.ANY), pl.BlockSpec(memory_space=pl.ANY),
                  pl.BlockSpec((tm, D), lambda i: (i, 0)),
                  pl.BlockSpec((tm, TOP_K), lambda i: (i, 0))] + [whole(w) for w in ws] + [whole(v) for v in vec],
        out_specs=pl.BlockSpec((tm, D), lambda i: (i, 0)),
        out_shape=jax.ShapeDtypeStruct((T, D), F32),
        scratch_shapes=[pltpu.SMEM((2, TOP_K * tm), I32),
                        pltpu.VMEM((2, TOP_K, tm, D), F32),
                        pltpu.SemaphoreType.DMA((2,)),
                        pltpu.SemaphoreType.DMA((2,))],
        compiler_params=_cparams(("arbitrary",)),
        name="combine_shared_norm",
    )(_token_tiles(dest, tm), ys, h2d, gate.T, *ws, *vec)


MOE_ROWS = 256


def _dispatch_plan(counts, n_tokens, rows):
    counts = counts.reshape(N_EXPERTS)
    padded = (counts + rows - 1) // rows * rows
    pend = jnp.cumsum(padded)
    pstart = pend - padded
    n_blocks = (n_tokens * TOP_K + N_EXPERTS * (rows - 1) + rows - 1) // rows
    blk = jnp.arange(n_blocks, dtype=I32)
    blk_e = jnp.minimum(jnp.sum((blk[:, None] * rows >= pend[None, :]).astype(I32), 1), N_EXPERTS - 1)
    last_blk = jnp.where(padded > counts, pend // rows - 1, -1)
    tail_blk = jnp.where(blk * rows >= pend[-1], blk, -1)
    zfill = jnp.concatenate([last_blk, tail_blk]).astype(I32)
    return pstart.astype(I32), blk_e, zfill, n_blocks


def _moe(h1, p, l, alpha):
    eidx, gate, rank, counts = _route(h1, p['w_router'][l], p['router_bias'][l], tm=512)
    pstart, blk_e, zfill, n_blocks = _dispatch_plan(counts, h1.shape[0], MOE_ROWS)
    dest = _destinations(pstart, eidx, rank, tm=2048)
    xs = _dispatch(h1, dest, zfill, n_blocks, MOE_ROWS, tm=256)
    ys = _routed_experts(xs, blk_e, p['w_eg'][l], p['w_eu'][l], p['w_ed'][l], MOE_ROWS)
    return _combine_shared_norm(h1, ys, dest, gate, p['w_sg'][l], p['w_su'][l], p['w_sd'][l],
                                p['ln2_g'][l], p['ln2_b'][l], alpha, tm=128)


def _layer(h2d, mem2d, p, l, depth, batch, seq, mem_len):
    alpha = (2 * depth) ** 0.25
    lam_init = 0.8 - 0.6 * math.exp(-0.3 * l)
    w_in = p['w_in'][l]
    daq, dak, dsq, ixq, kv, vk, ikk, dav, memq, ixw = _in_projection(h2d, w_in, seq, tm=min(512, seq))
    lam = (jnp.exp(jnp.sum(p['lq1'][l].astype(F32) * p['lk1'][l].astype(F32)))
           - jnp.exp(jnp.sum(p['lq2'][l].astype(F32) * p['lk2'][l].astype(F32))) + lam_init)
    ya = _diff_attention(daq, dak, dav, lam, p['subln_g'][l], batch, seq, lam_init, tq=min(512, seq))
    yb = _sparse_attention(ixq, ixw, dsq, ikk, kv, vk, batch, seq, tq=256)
    mkv = _matmul(mem2d, p['w_mem_kv'][l].astype(BF16), tm=mem_len, out_dtype=BF16)
    yc = _memory_attention(memq, mkv, batch, seq, mem_len, tq=min(512, seq))
    h1 = _merge_project_norm(h2d, ya, yb, yc, w_in[:, 3272:], p['w_ba'][l], p['w_bb'][l], p['w_bc'][l],
                             p['w_o'][l], p['ln1_g'][l], p['ln1_b'][l], alpha, tm=256)
    return _moe(h1, p, l, alpha)


def kernel(x, mem, w_in, da_lambda_q1, da_lambda_k1, da_lambda_q2, da_lambda_k2, da_subln_g, w_mem_kv, w_branch_a, w_branch_b, w_branch_c, w_out, ln1_g, ln1_b, w_router, router_bias, w_exp_gate, w_exp_up, w_exp_down, w_sh_gate, w_sh_up, w_sh_down, ln2_g, ln2_b):
    batch, seq, d = x.shape
    mem_len = mem.shape[1]
    depth = w_in.shape[0]
    p = dict(w_in=w_in, lq1=da_lambda_q1, lk1=da_lambda_k1, lq2=da_lambda_q2, lk2=da_lambda_k2,
             subln_g=da_subln_g, w_mem_kv=w_mem_kv, w_ba=w_branch_a, w_bb=w_branch_b, w_bc=w_branch_c,
             w_o=w_out, ln1_g=ln1_g, ln1_b=ln1_b, w_router=w_router, router_bias=router_bias,
             w_eg=w_exp_gate, w_eu=w_exp_up, w_ed=w_exp_down, w_sg=w_sh_gate, w_su=w_sh_up,
             w_sd=w_sh_down, ln2_g=ln2_g, ln2_b=ln2_b)
    h = x.reshape(batch * seq, d)
    mem2d = mem.reshape(batch * mem_len, d)
    for l in range(depth):
        h = _layer(h, mem2d, p, l, depth, batch, seq, mem_len)
    return h.reshape(batch, seq, d)
```

```python
import functools
import math

import jax
import jax.numpy as jnp
import numpy as np
from jax import lax
from jax.experimental import pallas as pl
from jax.experimental.pallas import tpu as pltpu

F32 = jnp.float32
BF16 = jnp.bfloat16
I32 = jnp.int32

LANES = 128
SUBLANES = 8
ROPE_THETA = 10000.0
LN_EPS = 1e-5

DA_HEADS = 4
DA_HEAD_DIM = 64
DSA_HEADS = 8
DSA_HEAD_DIM = 64
IDX_HEADS = 8
IDX_ROWS = 16
DSA_TOPK_MAX = 256
MEM_HEADS = 4
MEM_HEAD_DIM = 128
N_BRANCHES = 3

N_EXPERTS = 256
TOP_K = 8
N_GROUPS = 8
TOPK_GROUPS = 4
GROUP_SIZE = N_EXPERTS // N_GROUPS
ROUTED_SCALE = 2.5

NEG_BIG = -1e30
INT_MIN = -(2 ** 31)

VMEM_LIMIT = 56 * 1024 * 1024


def _cparams(sem):
    return pltpu.CompilerParams(dimension_semantics=sem, vmem_limit_bytes=VMEM_LIMIT)


def _dot(a, b):
    return jnp.dot(a, b, preferred_element_type=F32)


def _dot_nt(a, b):
    return lax.dot_general(a, b, (((1,), (1,)), ((), ())), preferred_element_type=F32)


def _tree(op, xs):
    xs = list(xs)
    while len(xs) > 1:
        xs = [op(xs[k], xs[k + 1]) if k + 1 < len(xs) else xs[k] for k in range(0, len(xs), 2)]
    return xs[0]


def _layer_norm(x, g, b):
    mu = jnp.mean(x, -1, keepdims=True)
    xc = x - mu
    var = jnp.mean(xc * xc, -1, keepdims=True)
    return xc * lax.rsqrt(var + LN_EPS) * g + b


def _inproj_kernel(x_ref, wr_ref, wn_ref, wixt_ref, tab_ref,
                   daq_ref, dak_ref, dsq_ref, ixq_ref, kv_ref, vk_ref, ikk_ref,
                   dav_ref, memq_ref, ixwt_ref):
    xb = x_ref[...].astype(BF16)

    def rope(t, kind):
        c = tab_ref[3 * kind]
        sa = tab_ref[3 * kind + 1]
        sb = tab_ref[3 * kind + 2]
        return t * c + pltpu.roll(t, 96, 1) * sa + pltpu.roll(t, 32, 1) * sb

    groups = ((daq_ref, 0.125), (dak_ref, 1.0), (dsq_ref, 0.125), (ixq_ref, 1.0))
    for gi, (ref, scale) in enumerate(groups):
        y = _dot(xb, wr_ref[:, gi * 512:(gi + 1) * 512])
        for c in range(4):
            r = rope(y[:, c * LANES:(c + 1) * LANES], 0)
            if scale != 1.0:
                r = r * scale
            ref[:, c * LANES:(c + 1) * LANES] = r.astype(BF16)
    y = _dot(xb, wr_ref[:, 2048:2432])
    kv_ref[...] = rope(y[:, 0:128], 1).astype(BF16)
    vk_ref[...] = rope(y[:, 128:256], 2).astype(BF16)
    ikk_ref[...] = rope(y[:, 256:384], 0).astype(BF16)
    y = _dot(xb, wn_ref[...])
    dav_ref[...] = y[:, 0:512].astype(BF16)
    memq_ref[...] = y[:, 512:1024].astype(BF16)
    ixwt_ref[...] = _dot_nt(wixt_ref[...], xb)


def _rope_tables(seq):
    dim = 64
    inv = 1.0 / (ROPE_THETA ** (jnp.arange(0, dim, 2, dtype=F32) / dim))
    ang = jnp.arange(seq, dtype=F32)[:, None] * inv[None, :]
    ang = jnp.concatenate([ang, ang], -1)
    cos, sin = jnp.cos(ang), jnp.sin(ang)
    local = jnp.arange(dim)[None, :]
    sa = jnp.where(local < dim // 2, -sin, 0.0)
    sb = jnp.where(local >= dim // 2, sin, 0.0)
    one, zero = jnp.ones_like(cos), jnp.zeros_like(cos)
    cat = lambda a, b: jnp.concatenate([a, b], -1)
    return jnp.stack([cat(cos, cos), cat(sa, sa), cat(sb, sb),
                      cat(cos, one), cat(sa, zero), cat(sb, zero),
                      cat(one, cos), cat(zero, sa), cat(zero, sb)], 0)


def _in_projection(x2d, w_in, seq, tm):
    T, D = x2d.shape
    o = np.cumsum([0, 512, 512, 512, 512, 64, 64, 512, 64, 8, 512]).tolist()
    col = lambda i: w_in[:, o[i]:o[i + 1]]
    da_q, da_k, da_v, ds_q, ds_k, ds_v, ix_q, ix_k, ix_w, mem_q = [col(i) for i in range(10)]
    ds_q = ds_q.reshape(D, DSA_HEADS // 2, 2, DSA_HEAD_DIM)[:, :, ::-1, :].reshape(D, 512)
    w_rope = jnp.concatenate([da_q, da_k, ds_q, ix_q, ds_k, ds_v, ds_v, ds_k, ix_k, ix_k], 1).astype(BF16)
    w_plain = jnp.concatenate([da_v, mem_q], 1).astype(BF16)
    w_ixt = jnp.concatenate([ix_w.T, jnp.zeros((IDX_ROWS - IDX_HEADS, D), w_in.dtype)], 0).astype(BF16)
    tabs = _rope_tables(seq)
    nseq = seq // tm
    bf = lambda n: jax.ShapeDtypeStruct((T, n), BF16)
    row = lambda n: pl.BlockSpec((tm, n), lambda i: (i, 0))
    return pl.pallas_call(
        _inproj_kernel,
        grid=(T // tm,),
        in_specs=[row(D),
                  pl.BlockSpec(w_rope.shape, lambda i: (0, 0)),
                  pl.BlockSpec(w_plain.shape, lambda i: (0, 0)),
                  pl.BlockSpec(w_ixt.shape, lambda i: (0, 0)),
                  pl.BlockSpec((9, tm, LANES), lambda i: (0, i % nseq, 0))],
        out_specs=[row(512), row(512), row(512), row(512), row(128), row(128), row(128),
                   row(512), row(512), pl.BlockSpec((IDX_ROWS, tm), lambda i: (0, i))],
        out_shape=[bf(512), bf(512), bf(512), bf(512), bf(128), bf(128), bf(128),
                   bf(512), bf(512), jax.ShapeDtypeStruct((IDX_ROWS, T), F32)],
        compiler_params=_cparams(("parallel",)),
        name="in_projection",
    )(x2d, w_rope, w_plain, w_ixt, tabs)


def _diffattn_kernel(lam_ref, q_ref, k_ref, v_ref, g_ref, o_ref, m_s, l_s, a_s, *, tq, tk, lam_init):
    i = pl.program_id(2)
    lam = lam_ref[0]
    q = q_ref[...]
    lane = lax.broadcasted_iota(I32, q.shape, 1)
    zero = jnp.zeros_like(q)
    qs = (jnp.where(lane < DA_HEAD_DIM, q, zero), jnp.where(lane >= DA_HEAD_DIM, q, zero))
    row = lax.broadcasted_iota(I32, (tq, tk), 0)
    col = lax.broadcasted_iota(I32, (tq, tk), 1)
    m_s[...] = jnp.full(m_s.shape, -jnp.inf, F32)
    l_s[...] = jnp.zeros(l_s.shape, F32)
    a_s[...] = jnp.zeros(a_s.shape, F32)

    def step(j, masked):
        off = pl.multiple_of(j * tk, tk)
        kj = k_ref[pl.ds(off, tk), :]
        vj = v_ref[pl.ds(off, tk), :]
        for mp in range(2):
            s = _dot_nt(qs[mp], kj)
            if masked:
                s = jnp.where(j * tk + col <= i * tq + row, s, -jnp.inf)
            tiles = [s[:, c * LANES:(c + 1) * LANES] for c in range(tk // LANES)]
            m_old = m_s[mp]
            m_blk = jnp.max(_tree(jnp.maximum, tiles), -1, keepdims=True)
            m_new = jnp.maximum(m_old, jnp.broadcast_to(m_blk, (tq, LANES)))
            ps = [jnp.exp(t - m_new) for t in tiles]
            al = jnp.exp(m_old - m_new)
            l_s[mp] = al * l_s[mp] + _tree(jnp.add, ps)
            a_s[mp] = al * a_s[mp] + _dot(jnp.concatenate(ps, 1).astype(BF16), vj)
            m_s[mp] = m_new
        return 0

    per_q = tq // tk
    lax.fori_loop(0, i * per_q, lambda j, c: step(j, False), 0)
    lax.fori_loop(i * per_q, (i + 1) * per_q, lambda j, c: step(j, True), 0)
    o1 = a_s[0] / jnp.sum(l_s[0], -1, keepdims=True)
    o2 = a_s[1] / jnp.sum(l_s[1], -1, keepdims=True)
    o = o1 - lam * o2
    y = o * lax.rsqrt(jnp.mean(o * o, -1, keepdims=True) + LN_EPS) * g_ref[...] * (1.0 - lam_init)
    o_ref[...] = y.astype(BF16)


def _diff_attention(daq, dak, dav, lam, subln_g, batch, seq, lam_init, tq, tk):
    T = batch * seq
    nq = seq // tq
    state = pltpu.VMEM((2, tq, LANES), F32)
    return pl.pallas_call(
        functools.partial(_diffattn_kernel, tq=tq, tk=tk, lam_init=lam_init),
        grid=(batch, DA_HEADS, nq),
        in_specs=[pl.BlockSpec(memory_space=pltpu.SMEM),
                  pl.BlockSpec((tq, LANES), lambda b, h, i: (b * nq + i, h)),
                  pl.BlockSpec((seq, LANES), lambda b, h, i: (b, h)),
                  pl.BlockSpec((seq, LANES), lambda b, h, i: (b, h)),
                  pl.BlockSpec((1, LANES), lambda b, h, i: (0, 0))],
        out_specs=pl.BlockSpec((tq, LANES), lambda b, h, i: (b * nq + i, h)),
        out_shape=jax.ShapeDtypeStruct((T, DA_HEADS * LANES), BF16),
        scratch_shapes=[state, state, state],
        compiler_params=_cparams(("parallel", "parallel", "arbitrary")),
        name="diff_attention",
    )(lam.reshape(1), daq, dak, dav, subln_g.reshape(1, LANES).astype(F32))


def _dsa_kernel(ixq_ref, ixwt_ref, dsq_ref, ikk_ref, kv_ref, vk_ref, o_ref,
                qi_s, qd_s, sc_s, mx_s, l_s, acc_s, *, tq, topk):
    i = pl.program_id(1)
    nblk = i + 1
    tk = tq
    nslab = tk // 8
    lane = lax.broadcasted_iota(I32, (tq, LANES), 1)
    lower = lane < DSA_HEAD_DIM
    npair = DSA_HEADS // 2
    for h in range(IDX_HEADS):
        sl = slice((h // 2) * LANES, (h // 2 + 1) * LANES)
        keep = lower if h % 2 == 0 else jnp.logical_not(lower)
        blk = ixq_ref[:, sl]
        qi_s[h * tq:(h + 1) * tq, :] = jnp.where(keep, blk, jnp.zeros_like(blk))
        blk = dsq_ref[:, sl]
        qd_s[h % 2, (h // 2) * tq:(h // 2 + 1) * tq, :] = jnp.where(keep, blk, jnp.zeros_like(blk))
    krow = lax.broadcasted_iota(I32, (tk, tq), 0)
    qcol = lax.broadcasted_iota(I32, (tk, tq), 1)
    sub8 = lax.broadcasted_iota(I32, (8, tq), 0)

    def score_chunk(j, _):
        off = pl.multiple_of(j * tk, tk)
        ik = ikk_ref[pl.ds(off, tk), :]
        logits = _dot_nt(ik, qi_s[...])
        sc = jnp.zeros((tk, tq), F32)
        for h in range(IDX_HEADS):
            sc = sc + ixwt_ref[h:h + 1, :] * jnp.maximum(logits[:, h * tq:(h + 1) * tq], 0.0)
        sc_s[j] = jnp.where(j * tk + krow <= i * tq + qcol, sc, -jnp.inf)
        return 0

    lax.fori_loop(0, nblk, score_chunk, 0)

    n_acc = 4

    def count(pred):
        def body(j, accs):
            accs = list(accs)
            for r in range(nslab):
                hit = pred(sc_s[j, r * 8:(r + 1) * 8, :], j * tk + r * 8 + sub8)
                accs[r % n_acc] = accs[r % n_acc] + hit.astype(F32)
            return tuple(accs)
        accs = lax.fori_loop(0, nblk, body, tuple(jnp.zeros((8, tq), F32) for _ in range(n_acc)))
        return jnp.sum(_tree(jnp.add, accs), 0, keepdims=True)

    def rows8(v):
        return jnp.broadcast_to(v, (8, tq))

    def as_score(key):
        return pltpu.bitcast(key ^ ((key >> 31) & 0x7FFFFFFF), F32)

    kf = float(topk)
    zero_i = jnp.zeros((1, tq), I32)
    c0 = count(lambda s, kpos: s >= 0.0)
    tau = jnp.where(c0 >= kf, zero_i, jnp.full((1, tq), INT_MIN, I32))

    def tau_bit(b, tau):
        cand = tau | (jnp.int32(1) << (30 - b))
        cb = rows8(as_score(cand))
        c = count(lambda s, kpos: s >= cb)
        return jnp.where(c >= kf, cand, tau)

    tau = lax.fori_loop(0, 31, tau_bit, tau)
    tau_f = jnp.where(tau == INT_MIN, -jnp.inf, as_score(tau))
    tau8 = rows8(tau_f)
    c_ge = count(lambda s, kpos: s >= tau8)

    def tie_cut():
        need = kf - count(lambda s, kpos: s > tau8)

        def cut_bit(b, cut):
            cand = cut | (jnp.int32(1) << (12 - b))
            cb = rows8(cand)
            c = count(lambda s, kpos: (s == tau8) & (kpos < cb))
            return jnp.where(c <= need, cand, cut)

        return lax.fori_loop(0, 13, cut_bit, zero_i)

    cut = lax.cond(jnp.max(c_ge) > kf, tie_cut, lambda: jnp.full((1, tq), 2 ** 13, I32))

    mx_s[...] = jnp.full(mx_s.shape, NEG_BIG, F32)
    l_s[...] = jnp.zeros(l_s.shape, F32)
    acc_s[...] = jnp.zeros(acc_s.shape, F32)
    nlt = tk // LANES

    def lane_tiles(a):
        return [a[:, c * LANES:(c + 1) * LANES] for c in range(nlt)]

    def bcast(v):
        return jnp.broadcast_to(v, (v.shape[0], LANES))

    def row_max(j, _):
        off = pl.multiple_of(j * tk, tk)
        sc = sc_s[j]
        kpos = j * tk + krow
        sel = (sc > tau_f) | ((sc == tau_f) & (kpos < cut))
        sel = sel & (kpos <= i * tq + qcol)
        bias = jnp.where(sel, 0.0, NEG_BIG).T
        sc_s[j] = bias
        kvj = kv_ref[pl.ds(off, tk), :]
        vkj = vk_ref[pl.ds(off, tk), :]
        for par, kk in enumerate((kvj, vkj)):
            s = (_dot_nt(qd_s[par], kk).reshape(npair, tq, tk) + bias[None]).reshape(npair * tq, tk)
            mx_s[par] = functools.reduce(jnp.maximum, lane_tiles(s), mx_s[par])
        return 0

    lax.fori_loop(0, nblk, row_max, 0)
    for par in range(2):
        mx_s[par] = bcast(jnp.max(mx_s[par], -1, keepdims=True))

    def attend(j, _):
        off = pl.multiple_of(j * tk, tk)
        bias = sc_s[j]
        kvj = kv_ref[pl.ds(off, tk), :]
        vkj = vk_ref[pl.ds(off, tk), :]
        for par, kk in enumerate((kvj, vkj)):
            s = (_dot_nt(qd_s[par], kk).reshape(npair, tq, tk) + bias[None]).reshape(npair * tq, tk)
            m = mx_s[par]
            ps = [jnp.exp(t - m) for t in lane_tiles(s)]
            l_s[par] = functools.reduce(jnp.add, ps, l_s[par])
            acc_s[par] = acc_s[par] + _dot(jnp.concatenate(ps, 1).astype(BF16), kk)
        return 0

    lax.fori_loop(0, nblk, attend, 0)

    def out(h):
        rows = slice((h // 2) * tq, (h // 2 + 1) * tq)
        return acc_s[h % 2, rows, :] / jnp.sum(l_s[h % 2, rows, :], -1, keepdims=True)

    for pr in range(DSA_HEADS // 2):
        o_ref[:, pr * LANES:(pr + 1) * LANES] = jnp.where(lower, out(2 * pr + 1), out(2 * pr)).astype(BF16)


def _sparse_attention(ixq, ixwt, dsq, ikk, kv, vk, batch, seq, tq):
    T = batch * seq
    nq = seq // tq
    topk = min(DSA_TOPK_MAX, seq // 4)
    tile = lambda n: pl.BlockSpec((tq, n), lambda b, i: (b * nq + i, 0))
    full = pl.BlockSpec((seq, LANES), lambda b, i: (b, 0))
    return pl.pallas_call(
        functools.partial(_dsa_kernel, tq=tq, topk=topk),
        grid=(batch, nq),
        in_specs=[tile(512), pl.BlockSpec((ixwt.shape[0], tq), lambda b, i: (0, b * nq + i)), tile(512),
                  full, full, full],
        out_specs=tile(512),
        out_shape=jax.ShapeDtypeStruct((T, 512), BF16),
        scratch_shapes=[pltpu.VMEM((IDX_HEADS * tq, LANES), BF16),
                        pltpu.VMEM((2, DSA_HEADS // 2 * tq, LANES), BF16),
                        pltpu.VMEM((nq, tq, tq), F32),
                        pltpu.VMEM((2, DSA_HEADS // 2 * tq, LANES), F32),
                        pltpu.VMEM((2, DSA_HEADS // 2 * tq, LANES), F32),
                        pltpu.VMEM((2, DSA_HEADS // 2 * tq, LANES), F32)],
        compiler_params=_cparams(("parallel", "arbitrary")),
        name="sparse_attention",
    )(ixq, ixwt, dsq, ikk, kv, vk)


def _matmul_kernel(x_ref, w_ref, o_ref):
    o_ref[...] = _dot(x_ref[...].astype(BF16), w_ref[...]).astype(o_ref.dtype)


def _matmul(x, w, tm, out_dtype):
    M, K = x.shape
    N = w.shape[1]
    return pl.pallas_call(
        _matmul_kernel,
        grid=(M // tm,),
        in_specs=[pl.BlockSpec((tm, K), lambda i: (i, 0)), pl.BlockSpec((K, N), lambda i: (0, 0))],
        out_specs=pl.BlockSpec((tm, N), lambda i: (i, 0)),
        out_shape=jax.ShapeDtypeStruct((M, N), out_dtype),
        compiler_params=_cparams(("parallel",)),
        name="matmul",
    )(x, w)


def _memattn_kernel(q_ref, kv_ref, o_ref):
    scale = MEM_HEAD_DIM ** -0.5
    width = MEM_HEADS * MEM_HEAD_DIM
    for h in range(MEM_HEADS):
        sl = slice(h * MEM_HEAD_DIM, (h + 1) * MEM_HEAD_DIM)
        k = kv_ref[:, sl]
        v = kv_ref[:, width + h * MEM_HEAD_DIM:width + (h + 1) * MEM_HEAD_DIM]
        s = _dot_nt(q_ref[:, sl], k) * scale
        p = jnp.exp(s - jnp.max(s, -1, keepdims=True))
        o = _dot(p.astype(BF16), v) / jnp.sum(p, -1, keepdims=True)
        o_ref[:, sl] = o.astype(BF16)


def _memory_attention(memq, mkv, batch, seq, mem_len, tq):
    T = batch * seq
    nq = seq // tq
    return pl.pallas_call(
        _memattn_kernel,
        grid=(batch, nq),
        in_specs=[pl.BlockSpec((tq, 512), lambda b, i: (b * nq + i, 0)),
                  pl.BlockSpec((mem_len, 1024), lambda b, i: (b, 0))],
        out_specs=pl.BlockSpec((tq, 512), lambda b, i: (b * nq + i, 0)),
        out_shape=jax.ShapeDtypeStruct((T, 512), BF16),
        compiler_params=_cparams(("parallel", "parallel")),
        name="memory_attention",
    )(memq, mkv)


def _merge_kernel(x_ref, ya_ref, yb_ref, yc_ref, wg_ref, wa_ref, wb_ref, wc_ref, wo_ref,
                  g_ref, b_ref, h_ref, *, alpha):
    x = x_ref[...]
    xb = x.astype(BF16)
    d = x.shape[1]
    merged = None
    for br, (y_ref, w_ref) in enumerate(((ya_ref, wa_ref), (yb_ref, wb_ref), (yc_ref, wc_ref))):
        gate = jax.nn.sigmoid(_dot(xb, wg_ref[:, br * d:(br + 1) * d]))
        term = gate * _dot(y_ref[...], w_ref[...])
        merged = term if merged is None else merged + term
    mix = _dot(merged.astype(BF16), wo_ref[...])
    h_ref[...] = _layer_norm(alpha * x + mix, g_ref[...], b_ref[...])


def _merge_project_norm(x2d, ya, yb, yc, w_gates, w_ba, w_bb, w_bc, w_o, ln_g, ln_b, alpha, tm):
    T, D = x2d.shape
    row = lambda n: pl.BlockSpec((tm, n), lambda i: (i, 0))
    whole = lambda a: pl.BlockSpec(a.shape, lambda i: (0, 0))
    ws = [w.astype(BF16) for w in (w_gates, w_ba, w_bb, w_bc, w_o)]
    vec = [v.reshape(1, D).astype(F32) for v in (ln_g, ln_b)]
    return pl.pallas_call(
        functools.partial(_merge_kernel, alpha=alpha),
        grid=(T // tm,),
        in_specs=[row(D), row(512), row(512), row(512)] + [whole(w) for w in ws] + [whole(v) for v in vec],
        out_specs=row(D),
        out_shape=jax.ShapeDtypeStruct((T, D), F32),
        compiler_params=_cparams(("parallel",)),
        name="merge_project_norm",
    )(x2d, ya, yb, yc, *ws, *vec)


def _router_kernel(h_ref, wr_ref, bias_ref, eidx_ref, gate_ref, rank_ref, cnt_ref, carry_s, *, tm):
    @pl.when(pl.program_id(0) == 0)
    def _():
        carry_s[...] = jnp.zeros(carry_s.shape, F32)

    neg = -jnp.inf
    logits = _dot_nt(wr_ref[...], h_ref[...].astype(BF16))
    scores = jax.nn.sigmoid(logits)
    biased = scores + bias_ref[...]
    erow = lax.broadcasted_iota(I32, (N_EXPERTS, tm), 0)
    big = jnp.int32(2 ** 30)

    def top1(vals, ids):
        mx = jnp.max(vals, 0, keepdims=True)
        am = jnp.min(jnp.where(vals == mx, ids, big), 0, keepdims=True)
        return mx, am

    gs = []
    ids = lax.broadcasted_iota(I32, (GROUP_SIZE, tm), 0)
    for g in range(N_GROUPS):
        v = biased[g * GROUP_SIZE:(g + 1) * GROUP_SIZE]
        m1, a1 = top1(v, ids)
        m2, _ = top1(jnp.where(ids == a1, neg, v), ids)
        gs.append(m1 + m2)
    gscore = jnp.concatenate(gs, 0)
    grow = lax.broadcasted_iota(I32, (N_GROUPS, tm), 0)
    gsel = jnp.zeros((N_GROUPS, tm), F32)
    for _ in range(TOPK_GROUPS):
        _, ga = top1(gscore, grow)
        hit = grow == ga
        gsel = jnp.where(hit, 1.0, gsel)
        gscore = jnp.where(hit, neg, gscore)
    masked = jnp.concatenate(
        [jnp.where(gsel[g:g + 1] > 0.0, biased[g * GROUP_SIZE:(g + 1) * GROUP_SIZE], neg)
         for g in range(N_GROUPS)], 0)

    eids, ws = [], []
    hot = jnp.zeros((N_EXPERTS, tm), F32)
    for _ in range(TOP_K):
        _, ea = top1(masked, erow)
        hit = erow == ea
        eids.append(ea)
        ws.append(jnp.sum(jnp.where(hit, scores, 0.0), 0, keepdims=True))
        hot = jnp.where(hit, 1.0, hot)
        masked = jnp.where(hit, neg, masked)
    w = jnp.concatenate(ws, 0)
    eidx_ref[...] = jnp.concatenate(eids, 0)
    gate_ref[...] = w / jnp.sum(w, 0, keepdims=True) * ROUTED_SCALE

    r_i =lax.broadcasted_iota(I32, (tm, tm), 0)
    c_i = lax.broadcasted_iota(I32, (tm, tm), 1)
    before = jnp.where(r_i < c_i, 1.0, 0.0).astype(BF16)
    pos = carry_s[...] + _dot(hot.astype(BF16), before)
    rank_ref[...] = jnp.concatenate(
        [jnp.sum(jnp.where(erow == e, pos, 0.0), 0, keepdims=True) for e in eids], 0).astype(I32)
    carry_s[...] = carry_s[...] + jnp.sum(hot, 1, keepdims=True)
    cnt_ref[...] = carry_s[...].astype(I32)


def _route(h2d, w_router, router_bias, tm):
    T, D = h2d.shape
    out = lambda dt: jax.ShapeDtypeStruct((TOP_K, T), dt)
    blk = pl.BlockSpec((TOP_K, tm), lambda i: (0, i))
    return pl.pallas_call(
        functools.partial(_router_kernel, tm=tm),
        grid=(T // tm,),
        in_specs=[pl.BlockSpec((tm, D), lambda i: (i, 0)),
                  pl.BlockSpec((N_EXPERTS, D), lambda i: (0, 0)),
                  pl.BlockSpec((N_EXPERTS, 1), lambda i: (0, 0))],
        out_specs=[blk, blk, blk, pl.BlockSpec((N_EXPERTS, 1), lambda i: (0, 0))],
        out_shape=[out(I32), out(F32), out(I32), jax.ShapeDtypeStruct((N_EXPERTS, 1), I32)],
        scratch_shapes=[pltpu.VMEM((N_EXPERTS, 1), F32)],
        compiler_params=_cparams(("arbitrary",)),
        name="router",
    )(h2d, w_router.T.astype(BF16), router_bias.reshape(N_EXPERTS, 1).astype(F32))


def _dest_kernel(pstart_ref, eidx_ref, rank_ref, dest_ref):
    eidx = eidx_ref[...]

    def body(e, acc):
        return acc + jnp.where(eidx == e, pstart_ref[e], 0)

    dest_ref[...] = lax.fori_loop(0, N_EXPERTS, body, rank_ref[...])


def _destinations(pstart, eidx, rank, tm):
    T = eidx.shape[1]
    blk = lambda: pl.BlockSpec((TOP_K, tm), lambda i, ps: (0, i))
    return pl.pallas_call(
        _dest_kernel,
        grid_spec=pltpu.PrefetchScalarGridSpec(
            num_scalar_prefetch=1, grid=(T // tm,), in_specs=[blk(), blk()], out_specs=blk()),
        out_shape=jax.ShapeDtypeStruct((TOP_K, T), I32),
        compiler_params=_cparams(("parallel",)),
        name="destinations",
    )(pstart, eidx, rank)


def _dispatch_kernel(zfill_ref, dest_hbm, h_ref, xs_hbm, dest_s, hbuf, zbuf, dest_sem, row_sem, z_sem,
                     *, tm, rows):
    i = pl.program_id(0)
    n = pl.num_programs(0)
    nz = zfill_ref.shape[0]

    def dest_copy(blk, slot):
        return pltpu.make_async_copy(dest_hbm.at[blk], dest_s.at[pl.ds(slot * (TOP_K * tm), TOP_K * tm)],
                                     dest_sem.at[slot])

    def rows_done(slot):
        for _ in range(TOP_K):
            pltpu.make_async_copy(hbuf.at[slot], xs_hbm.at[pl.ds(0, tm)], row_sem.at[slot]).wait()

    @pl.when(i == 0)
    def _():
        zbuf[...] = jnp.zeros(zbuf.shape, F32)

        def zcopy(b):
            return pltpu.make_async_copy(zbuf, xs_hbm.at[pl.ds(zfill_ref[b] * rows, rows)], z_sem)

        def start(b, _):
            @pl.when(zfill_ref[b] >= 0)
            def _():
                zcopy(b).start()
            return 0

        def wait(b, _):
            @pl.when(zfill_ref[b] >= 0)
            def _():
                zcopy(b).wait()
            return 0

        lax.fori_loop(0, nz, start, 0)
        lax.fori_loop(0, nz, wait, 0)
        dest_copy(0, 0).start()

    slot = i % 2
    dest_copy(i, slot).wait()

    @pl.when(i + 1 < n)
    def _():
        dest_copy(i + 1, 1 - slot).start()

    @pl.when(i >= 2)
    def _():
        rows_done(slot)

    hbuf[slot] = h_ref[...]

    base = slot * (TOP_K * tm)

    def body(g, _):
        for rr in range(SUBLANES):
            r = g * SUBLANES + rr
            for k in range(TOP_K):
                d = dest_s[base + k * tm + r]
                pltpu.make_async_copy(hbuf.at[slot, pl.ds(r, 1)], xs_hbm.at[pl.ds(d, 1)],
                                      row_sem.at[slot]).start()
        return 0

    lax.fori_loop(0, tm // SUBLANES, body, 0)

    @pl.when(i == n - 1)
    def _():
        rows_done(slot)

        @pl.when(n > 1)
        def _():
            rows_done(1 - slot)


def _token_tiles(a, tm):
    k, T = a.shape
    return a.reshape(k, T // tm, tm).transpose(1, 0, 2).reshape(T // tm, k * tm)


def _dispatch(h2d, dest, zfill, n_blocks, rows, tm):
    T, D = h2d.shape
    return pl.pallas_call(
        functools.partial(_dispatch_kernel, tm=tm, rows=rows),
        grid_spec=pltpu.PrefetchScalarGridSpec(
            num_scalar_prefetch=1,
            grid=(T // tm,),
            in_specs=[pl.BlockSpec(memory_space=pl.ANY),
                      pl.BlockSpec((tm, D), lambda i, z: (i, 0))],
            out_specs=pl.BlockSpec(memory_space=pl.ANY),
            scratch_shapes=[pltpu.SMEM((2 * TOP_K * tm,), I32),
                            pltpu.VMEM((2, tm, D), F32),
                            pltpu.VMEM((rows, D), F32),
                            pltpu.SemaphoreType.DMA((2,)),
                            pltpu.SemaphoreType.DMA((2,)),
                            pltpu.SemaphoreType.DMA]),
        out_shape=jax.ShapeDtypeStruct((n_blocks * rows, D), F32),
        compiler_params=_cparams(("arbitrary",)),
        name="dispatch",
    )(zfill, _token_tiles(dest, tm), h2d)


def _experts_kernel(blk_e_ref, used_ref, x_ref, wg_ref, wu_ref, wd_ref, y_ref, wgb, wub, wdb):
    i = pl.program_id(0)
    e = blk_e_ref[i]
    prev = blk_e_ref[jnp.maximum(i - 1, 0)]

    @pl.when((i == 0) | (e != prev))
    def _():
        wgb[...] = wg_ref[0].astype(BF16)
        wub[...] = wu_ref[0].astype(BF16)
        wdb[...] = wd_ref[0].astype(BF16)

    @pl.when(i < used_ref[0])
    def _():
        x = x_ref[...].astype(BF16)
        hidden = jax.nn.silu(_dot(x, wgb[...])) * _dot(x, wub[...])
        y_ref[...] = _dot(hidden.astype(BF16), wdb[...])

    @pl.when(i >= used_ref[0])
    def _():
        y_ref[...] = jnp.zeros(y_ref.shape, F32)


def _routed_experts(xs, blk_e, n_used, w_eg, w_eu, w_ed, rows):
    P, D = xs.shape
    ff = w_eg.shape[-1]
    row_blk = lambda i, be, nu: (jnp.minimum(i, nu[0] - 1), 0)
    return pl.pallas_call(
        _experts_kernel,
        grid_spec=pltpu.PrefetchScalarGridSpec(
            num_scalar_prefetch=2,
            grid=(P // rows,),
            in_specs=[pl.BlockSpec((rows, D), row_blk),
                      pl.BlockSpec((1, D, ff), lambda i, be, nu: (be[i], 0, 0)),
                      pl.BlockSpec((1, D, ff), lambda i, be, nu: (be[i], 0, 0)),
                      pl.BlockSpec((1, ff, D), lambda i, be, nu: (be[i], 0, 0))],
            out_specs=pl.BlockSpec((rows, D), lambda i, be, nu: (i, 0)),
            scratch_shapes=[pltpu.VMEM((D, ff), BF16),
                            pltpu.VMEM((D, ff), BF16),
                            pltpu.VMEM((ff, D), BF16)]),
        out_shape=jax.ShapeDtypeStruct((P, D), F32),
        compiler_params=_cparams(("arbitrary",)),
        name="routed_experts",
    )(blk_e, n_used, xs, w_eg, w_eu, w_ed)


def _combine_kernel(dest_hbm, ys_hbm, h_ref, gate_ref, wsg_ref, wsu_ref, wsd_ref, g_ref, b_ref, o_ref,
                    dest_s, buf, dest_sem, row_sem, *, tm, alpha):
    i = pl.program_id(0)
    n = pl.num_programs(0)

    def dest_copy(blk, slot):
        return pltpu.make_async_copy(dest_hbm.at[blk], dest_s.at[pl.ds(slot * (TOP_K * tm), TOP_K * tm)],
                                     dest_sem.at[slot])

    def issue_rows(slot):
        base = slot * (TOP_K * tm)

        def body(g, _):
            for rr in range(SUBLANES):
                r = g * SUBLANES + rr
                for k in range(TOP_K):
                    d = dest_s[base + k * tm + r]
                    pltpu.make_async_copy(ys_hbm.at[pl.ds(d, 1)], buf.at[slot, k, pl.ds(r, 1)],
                                          row_sem.at[slot]).start()
            return 0
        lax.fori_loop(0, tm // SUBLANES, body, 0)

    @pl.when(i == 0)
    def _():
        dest_copy(0, 0).start()
        dest_copy(0, 0).wait()
        issue_rows(0)

        @pl.when(n > 1)
        def _():
            dest_copy(1, 1).start()

    nxt = (i + 1) % 2

    @pl.when(i + 1 < n)
    def _():
        dest_copy(i + 1, nxt).wait()
        issue_rows(nxt)

    @pl.when(i + 2 < n)
    def _():
        dest_copy(i + 2, i % 2).start()

    h = h_ref[...]
    hb = h.astype(BF16)
    shared = _dot((jax.nn.silu(_dot(hb, wsg_ref[...])) * _dot(hb, wsu_ref[...])).astype(BF16), wsd_ref[...])
    slot = i % 2
    total = alpha * h + shared
    for k in range(TOP_K):
        pltpu.make_async_copy(ys_hbm.at[pl.ds(0, tm)], buf.at[slot, k], row_sem.at[slot]).wait()
    for k in range(TOP_K):
        total = total + gate_ref[:, k:k + 1] * buf[slot, k]
    o_ref[...] = _layer_norm(total, g_ref[...], b_ref[...])


def _combine_shared_norm(h2d, ys, dest, gate, w_sg, w_su, w_sd, ln_g, ln_b, alpha, tm):
    T, D = h2d.shape
    nt = T // tm
    ws = [w.astype(BF16) for w in (w_sg, w_su, w_sd)]
    vec = [v.reshape(1, D).astype(F32) for v in (ln_g, ln_b)]
    whole = lambda a: pl.BlockSpec(a.shape, lambda i: (0, 0))
    return pl.pallas_call(
        functools.partial(_combine_kernel, tm=tm, alpha=alpha),
        grid=(nt,),
        in_specs=[pl.BlockSpec(memory_space=pl.ANY), pl.BlockSpec(memory_space=pl.ANY),
                  pl.BlockSpec((tm, D), lambda i: (i, 0)),
                  pl.BlockSpec((tm, TOP_K), lambda i: (i, 0))] + [whole(w) for w in ws] + [whole(v) for v in vec],
        out_specs=pl.BlockSpec((tm, D), lambda i: (i, 0)),
        out_shape=jax.ShapeDtypeStruct((T, D), F32),
        scratch_shapes=[pltpu.SMEM((2 * TOP_K * tm,), I32),
                        pltpu.VMEM((2, TOP_K, tm, D), F32),
                        pltpu.SemaphoreType.DMA((2,)),
                        pltpu.SemaphoreType.DMA((2,))],
        compiler_params=_cparams(("arbitrary",)),
        name="combine_shared_norm",
    )(_token_tiles(dest, tm), ys, h2d, gate.T, *ws, *vec)


DA_TQ, DA_TK = 1024, 512
MOE_ROWS = 256


def _dispatch_plan(counts, n_tokens, rows):
    counts = counts.reshape(N_EXPERTS)
    padded = (counts + rows - 1) // rows * rows
    pend = jnp.cumsum(padded)
    pstart = pend - padded
    n_blocks = (n_tokens * TOP_K + N_EXPERTS * (rows - 1) + rows - 1) // rows
    blk = jnp.arange(n_blocks, dtype=I32)
    n_used = pend[-1] // rows
    blk_e = jnp.sum((jnp.minimum(blk, n_used - 1)[:, None] * rows >= pend[None, :]).astype(I32), 1)
    last_blk = jnp.where(padded > counts, pend // rows - 1, -1)
    tail_blk = jnp.where(blk >= n_used, blk, -1)
    zfill = jnp.concatenate([last_blk, tail_blk]).astype(I32)
    return pstart.astype(I32), blk_e, n_used.astype(I32).reshape(1), zfill, n_blocks


def _moe(h1, p, l, alpha):
    eidx, gate, rank, counts = _route(h1, p['w_router'][l], p['router_bias'][l], tm=512)
    pstart, blk_e, n_used, zfill, n_blocks = _dispatch_plan(counts, h1.shape[0], MOE_ROWS)
    dest = _destinations(pstart, eidx, rank, tm=2048)
    xs = _dispatch(h1, dest, zfill, n_blocks, MOE_ROWS, tm=256)
    ys = _routed_experts(xs, blk_e, n_used, p['w_eg'][l], p['w_eu'][l], p['w_ed'][l], MOE_ROWS)
    return _combine_shared_norm(h1, ys, dest, gate, p['w_sg'][l], p['w_su'][l], p['w_sd'][l],
                                p['ln2_g'][l], p['ln2_b'][l], alpha, tm=128)


def _layer(h2d, mem2d, p, l, depth, batch, seq, mem_len):
    alpha = (2 * depth) ** 0.25
    lam_init = 0.8 - 0.6 * math.exp(-0.3 * l)
    w_in = p['w_in'][l]
    daq, dak, dsq, ixq, kv, vk, ikk, dav, memq, ixw = _in_projection(h2d, w_in, seq, tm=min(512, seq))
    lam = (jnp.exp(jnp.sum(p['lq1'][l].astype(F32) * p['lk1'][l].astype(F32)))
           - jnp.exp(jnp.sum(p['lq2'][l].astype(F32) * p['lk2'][l].astype(F32))) + lam_init)
    ya = _diff_attention(daq, dak, dav, lam, p['subln_g'][l], batch, seq, lam_init, tq=min(DA_TQ, seq), tk=min(DA_TK, seq))
    yb = _sparse_attention(ixq, ixw, dsq, ikk, kv, vk, batch, seq, tq=256)
    mkv = _matmul(mem2d, p['w_mem_kv'][l].astype(BF16), tm=mem_len, out_dtype=BF16)
    yc = _memory_attention(memq, mkv, batch, seq, mem_len, tq=min(512, seq))
    h1 = _merge_project_norm(h2d, ya, yb, yc, w_in[:, 3272:], p['w_ba'][l], p['w_bb'][l], p['w_bc'][l],
                             p['w_o'][l], p['ln1_g'][l], p['ln1_b'][l], alpha, tm=256)
    return _moe(h1, p, l, alpha)


def kernel(x, mem, w_in, da_lambda_q1, da_lambda_k1, da_lambda_q2, da_lambda_k2, da_subln_g, w_mem_kv, w_branch_a, w_branch_b, w_branch_c, w_out, ln1_g, ln1_b, w_router, router_bias, w_exp_gate, w_exp_up, w_exp_down, w_sh_gate, w_sh_up, w_sh_down, ln2_g, ln2_b):
    batch, seq, d = x.shape
    mem_len = mem.shape[1]
    depth = w_in.shape[0]
    p = dict(w_in=w_in, lq1=da_lambda_q1, lk1=da_lambda_k1, lq2=da_lambda_q2, lk2=da_lambda_k2,
             subln_g=da_subln_g, w_mem_kv=w_mem_kv, w_ba=w_branch_a, w_bb=w_branch_b, w_bc=w_branch_c,
             w_o=w_out, ln1_g=ln1_g, ln1_b=ln1_b, w_router=w_router, router_bias=router_bias,
             w_eg=w_exp_gate, w_eu=w_exp_up, w_ed=w_exp_down, w_sg=w_sh_gate, w_su=w_sh_up,
             w_sd=w_sh_down, ln2_g=ln2_g, ln2_b=ln2_b)
    h = x.reshape(batch * seq, d)
    mem2d = mem.reshape(batch * mem_len, d)
    for l in range(depth):
        h = _layer(h, mem2d, p, l, depth, batch, seq, mem_len)
    return h.reshape(batch, seq, d)
```

```python
import functools
import math

import jax
import jax.numpy as jnp
import numpy as np
from jax import lax
from jax.experimental import pallas as pl
from jax.experimental.pallas import tpu as pltpu

F32 = jnp.float32
BF16 = jnp.bfloat16
I32 = jnp.int32

LANES = 128
SUBLANES = 8
ROPE_THETA = 10000.0
LN_EPS = 1e-5

DA_HEADS = 4
DA_HEAD_DIM = 64
DSA_HEADS = 8
DSA_HEAD_DIM = 64
IDX_HEADS = 8
IDX_ROWS = 16
DSA_TOPK_MAX = 256
MEM_HEADS = 4
MEM_HEAD_DIM = 128

N_EXPERTS = 256
TOP_K = 8
N_GROUPS = 8
TOPK_GROUPS = 4
GROUP_SIZE = N_EXPERTS // N_GROUPS
ROUTED_SCALE = 2.5

NEG_BIG = -1e30
INT_MIN = -(2 ** 31)

VMEM_LIMIT = 56 * 1024 * 1024


def _cparams(sem):
    return pltpu.CompilerParams(dimension_semantics=sem, vmem_limit_bytes=VMEM_LIMIT)


def _dot(a, b):
    return jnp.dot(a, b, preferred_element_type=F32)


def _dot_nt(a, b):
    return lax.dot_general(a, b, (((1,), (1,)), ((), ())), preferred_element_type=F32)


def _tree(op, xs):
    xs = list(xs)
    while len(xs) > 1:
        xs = [op(xs[k], xs[k + 1]) if k + 1 < len(xs) else xs[k] for k in range(0, len(xs), 2)]
    return xs[0]


def _layer_norm(x, g, b):
    mu = jnp.mean(x, -1, keepdims=True)
    xc = x - mu
    var = jnp.mean(xc * xc, -1, keepdims=True)
    return xc * lax.rsqrt(var + LN_EPS) * g + b


def _inproj_kernel(x_ref, wr_ref, wn_ref, wixt_ref, tab_ref,
                   daq_ref, dak_ref, dsq_ref, ixq_ref, kv_ref, vk_ref, ikk_ref,
                   dav_ref, memq_ref, ixwt_ref):
    xb = x_ref[...].astype(BF16)

    def rope(t, kind):
        c = tab_ref[3 * kind]
        sa = tab_ref[3 * kind + 1]
        sb = tab_ref[3 * kind + 2]
        return t * c + pltpu.roll(t, 96, 1) * sa + pltpu.roll(t, 32, 1) * sb

    groups = ((daq_ref, 0.125), (dak_ref, 1.0), (dsq_ref, 0.125), (ixq_ref, 1.0))
    for gi, (ref, scale) in enumerate(groups):
        y = _dot(xb, wr_ref[:, gi * 512:(gi + 1) * 512])
        for c in range(4):
            r = rope(y[:, c * LANES:(c + 1) * LANES], 0)
            if scale != 1.0:
                r = r * scale
            ref[:, c * LANES:(c + 1) * LANES] = r.astype(BF16)
    y = _dot(xb, wr_ref[:, 2048:2432])
    kv_ref[...] = rope(y[:, 0:128], 1).astype(BF16)
    vk_ref[...] = rope(y[:, 128:256], 2).astype(BF16)
    ikk_ref[...] = rope(y[:, 256:384], 0).astype(BF16)
    y = _dot(xb, wn_ref[...])
    dav_ref[...] = y[:, 0:512].astype(BF16)
    memq_ref[...] = y[:, 512:1024].astype(BF16)
    ixwt_ref[...] = _dot_nt(wixt_ref[...], xb)


def _rope_tables(seq):
    dim = 64
    inv = 1.0 / (ROPE_THETA ** (jnp.arange(0, dim, 2, dtype=F32) / dim))
    ang = jnp.arange(seq, dtype=F32)[:, None] * inv[None, :]
    ang = jnp.concatenate([ang, ang], -1)
    cos, sin = jnp.cos(ang), jnp.sin(ang)
    local = jnp.arange(dim)[None, :]
    sa = jnp.where(local < dim // 2, -sin, 0.0)
    sb = jnp.where(local >= dim // 2, sin, 0.0)
    one, zero = jnp.ones_like(cos), jnp.zeros_like(cos)
    cat = lambda a, b: jnp.concatenate([a, b], -1)
    return jnp.stack([cat(cos, cos), cat(sa, sa), cat(sb, sb),
                      cat(cos, one), cat(sa, zero), cat(sb, zero),
                      cat(one, cos), cat(zero, sa), cat(zero, sb)], 0)


def _in_projection(x2d, w_in, seq, tm):
    T, D = x2d.shape
    o = np.cumsum([0, 512, 512, 512, 512, 64, 64, 512, 64, 8, 512]).tolist()
    col = lambda i: w_in[:, o[i]:o[i + 1]]
    da_q, da_k, da_v, ds_q, ds_k, ds_v, ix_q, ix_k, ix_w, mem_q = [col(i) for i in range(10)]
    ds_q = ds_q.reshape(D, DSA_HEADS // 2, 2, DSA_HEAD_DIM)[:, :, ::-1, :].reshape(D, 512)
    w_rope = jnp.concatenate([da_q, da_k, ds_q, ix_q, ds_k, ds_v, ds_v, ds_k, ix_k, ix_k], 1).astype(BF16)
    w_plain = jnp.concatenate([da_v, mem_q], 1).astype(BF16)
    w_ixt = jnp.concatenate([ix_w.T, jnp.zeros((IDX_ROWS - IDX_HEADS, D), w_in.dtype)], 0).astype(BF16)
    tabs = _rope_tables(seq)
    nseq = seq // tm
    bf = lambda n: jax.ShapeDtypeStruct((T, n), BF16)
    row = lambda n: pl.BlockSpec((tm, n), lambda i: (i, 0))
    return pl.pallas_call(
        _inproj_kernel,
        grid=(T // tm,),
        in_specs=[row(D),
                  pl.BlockSpec(w_rope.shape, lambda i: (0, 0)),
                  pl.BlockSpec(w_plain.shape, lambda i: (0, 0)),
                  pl.BlockSpec(w_ixt.shape, lambda i: (0, 0)),
                  pl.BlockSpec((9, tm, LANES), lambda i: (0, i % nseq, 0))],
        out_specs=[row(512), row(512), row(512), row(512), row(128), row(128), row(128),
                   row(512), row(512), pl.BlockSpec((IDX_ROWS, tm), lambda i: (0, i))],
        out_shape=[bf(512), bf(512), bf(512), bf(512), bf(128), bf(128), bf(128),
                   bf(512), bf(512), jax.ShapeDtypeStruct((IDX_ROWS, T), F32)],
        compiler_params=_cparams(("parallel",)),
        name="in_projection",
    )(x2d, w_rope, w_plain, w_ixt, tabs)


def _diffattn_kernel(lam_ref, q_ref, k_ref, v_ref, g_ref, o_ref, m_s, l_s, a_s, *, tq, tk, lam_init):
    i = pl.program_id(2)
    lam = lam_ref[0]
    q = q_ref[...]
    lane = lax.broadcasted_iota(I32, q.shape, 1)
    zero = jnp.zeros_like(q)
    qs = (jnp.where(lane < DA_HEAD_DIM, q, zero), jnp.where(lane >= DA_HEAD_DIM, q, zero))
    m_s[...] = jnp.full(m_s.shape, -jnp.inf, F32)
    l_s[...] = jnp.zeros(l_s.shape, F32)
    a_s[...] = jnp.zeros(a_s.shape, F32)

    def step(j, masked, r0=0):
        off = pl.multiple_of(j * tk, tk)
        kj = k_ref[pl.ds(off, tk), :]
        vj = v_ref[pl.ds(off, tk), :]
        for mp in range(2):
            s = _dot_nt(qs[mp][r0:], kj)
            if masked:
                row = lax.broadcasted_iota(I32, (tq - r0, tk), 0) + r0
                col = lax.broadcasted_iota(I32, (tq - r0, tk), 1)
                s = jnp.where(j * tk + col <= i * tq + row, s, -jnp.inf)
            tiles = [s[:, c * LANES:(c + 1) * LANES] for c in range(tk // LANES)]
            m_old = m_s[mp, r0:]
            m_blk = jnp.max(_tree(jnp.maximum, tiles), -1, keepdims=True)
            m_new = jnp.maximum(m_old, jnp.broadcast_to(m_blk, (tq - r0, LANES)))
            ps = [jnp.exp(t - m_new) for t in tiles]
            al = jnp.exp(m_old - m_new)
            l_s[mp, r0:] = al * l_s[mp, r0:] + _tree(jnp.add, ps)
            a_s[mp, r0:] = al * a_s[mp, r0:] + _dot(jnp.concatenate(ps, 1).astype(BF16), vj)
            m_s[mp, r0:] = m_new
        return 0

    per_q = tq // tk

    def fully_visible(t, _):
        for jj in range(per_q):
            step(t * per_q + jj, False)
        return 0

    lax.fori_loop(0, i, fully_visible, 0)
    for jj in range(per_q):
        step(i * per_q + jj, True, jj * tk)
    o1 = a_s[0] / jnp.sum(l_s[0], -1, keepdims=True)
    o2 = a_s[1] / jnp.sum(l_s[1], -1, keepdims=True)
    o = o1 - lam * o2
    y = o * lax.rsqrt(jnp.mean(o * o, -1, keepdims=True) + LN_EPS) * g_ref[...] * (1.0 - lam_init)
    o_ref[...] = y.astype(BF16)


def _diff_attention(daq, dak, dav, lam, subln_g, batch, seq, lam_init, tq, tk):
    T = batch * seq
    nq = seq // tq
    state = pltpu.VMEM((2, tq, LANES), F32)
    return pl.pallas_call(
        functools.partial(_diffattn_kernel, tq=tq, tk=tk, lam_init=lam_init),
        grid=(batch, DA_HEADS, nq),
        in_specs=[pl.BlockSpec(memory_space=pltpu.SMEM),
                  pl.BlockSpec((tq, LANES), lambda b, h, i: (b * nq + i, h)),
                  pl.BlockSpec((seq, LANES), lambda b, h, i: (b, h)),
                  pl.BlockSpec((seq, LANES), lambda b, h, i: (b, h)),
                  pl.BlockSpec((1, LANES), lambda b, h, i: (0, 0))],
        out_specs=pl.BlockSpec((tq, LANES), lambda b, h, i: (b * nq + i, h)),
        out_shape=jax.ShapeDtypeStruct((T, DA_HEADS * LANES), BF16),
        scratch_shapes=[state, state, state],
        compiler_params=_cparams(("parallel", "parallel", "arbitrary")),
        name="diff_attention",
    )(lam.reshape(1), daq, dak, dav, subln_g.reshape(1, LANES).astype(F32))


def _dsa_kernel(ixq_ref, ixwt_ref, dsq_ref, ikk_ref, kv_ref, vk_ref, o_ref,
                qi_s, qd_s, sc_s, g_s, mx_s, l_s, acc_s, *, tq, topk):
    i = pl.program_id(1)
    nblk = i + 1
    tk = tq
    nslab = tk // 8
    lane = lax.broadcasted_iota(I32, (tq, LANES), 1)
    lower = lane < DSA_HEAD_DIM
    npair = DSA_HEADS // 2
    for h in range(IDX_HEADS):
        sl = slice((h // 2) * LANES, (h // 2 + 1) * LANES)
        keep = lower if h % 2 == 0 else jnp.logical_not(lower)
        blk = ixq_ref[:, sl]
        qi_s[h * tq:(h + 1) * tq, :] = jnp.where(keep, blk, jnp.zeros_like(blk))
        blk = dsq_ref[:, sl]
        qd_s[h % 2, (h // 2) * tq:(h // 2 + 1) * tq, :] = jnp.where(keep, blk, jnp.zeros_like(blk))
    krow = lax.broadcasted_iota(I32, (tk, tq), 0)
    qcol = lax.broadcasted_iota(I32, (tk, tq), 1)
    sub8 = lax.broadcasted_iota(I32, (8, tq), 0)

    def score_chunk(j, _):
        off = pl.multiple_of(j * tk, tk)
        ik = ikk_ref[pl.ds(off, tk), :]
        logits = _dot_nt(ik, qi_s[...])
        sc = jnp.zeros((tk, tq), F32)
        for h in range(IDX_HEADS):
            sc = sc + ixwt_ref[h:h + 1, :] * jnp.maximum(logits[:, h * tq:(h + 1) * tq], 0.0)
        sc = jnp.where(j * tk + krow <= i * tq + qcol, sc, -jnp.inf)
        sc_s[j] = sc
        g_s[j] = sc.astype(BF16)
        return 0

    def over_chunks(body, width=4):
        def group(jg, _):
            for u in range(width):
                body(width * jg + u)
            return 0
        lax.fori_loop(0, nblk // width, group, 0)
        lax.fori_loop(nblk // width * width, nblk, lambda j, c: (body(j), 0)[1], 0)

    over_chunks(lambda j: score_chunk(j, 0))

    n_acc = 4

    def count(pred):
        def body(j, accs):
            accs = list(accs)
            for r in range(nslab):
                hit = pred(sc_s[j, r * 8:(r + 1) * 8, :], j * tk + r * 8 + sub8)
                accs[r % n_acc] = accs[r % n_acc] + hit.astype(F32)
            return tuple(accs)
        accs = lax.fori_loop(0, nblk, body, tuple(jnp.zeros((8, tq), F32) for _ in range(n_acc)))
        return jnp.sum(_tree(jnp.add, accs), 0, keepdims=True)

    def rows8(v):
        return jnp.broadcast_to(v, (8, tq))

    def as_score(key):
        return pltpu.bitcast(key ^ ((key >> 31) & 0x7FFFFFFF), F32)

    def count_coarse(cand):
        cb = jnp.broadcast_to(cand, (16, tq))
        one, nil = jnp.ones((16, tq), BF16), jnp.zeros((16, tq), BF16)

        def body(j, accs):
            accs = list(accs)
            for r in range(tk // 16):
                hit = g_s[j, r * 16:(r + 1) * 16, :] >= cb
                accs[r % n_acc] = accs[r % n_acc] + jnp.where(hit, one, nil)
            return tuple(accs)
        accs = lax.fori_loop(0, nblk, body, tuple(nil for _ in range(n_acc)))
        return jnp.sum(_tree(jnp.add, [a.astype(F32) for a in accs]), 0, keepdims=True)

    def as_coarse(key16):
        bits16 = key16 ^ ((key16 >> 15) & 0x7FFF)
        return pltpu.bitcast(bits16 << 16, F32).astype(BF16)

    kf = float(topk)
    zero_i = jnp.zeros((1, tq), I32)

    def coarse_bit(b, hi):
        cand = jnp.where(b == 0, zero_i, hi | (jnp.int32(1) << (15 - b)))
        c = count_coarse(as_coarse(cand))
        return jnp.where(c >= kf, cand, hi)

    hi16 = lax.fori_loop(0, 16, coarse_bit, jnp.full((1, tq), -(2 ** 15), I32))
    found = hi16 != -(2 ** 15)
    key_t = jnp.where(found, (hi16 << 16) + jnp.where(hi16 < 0, 0xFFFF, 0), 0)
    span = (1 << 16) + (1 << 15) + 2

    def bisect(_, bounds):
        lo, width = bounds
        step = (width + 1) >> 1
        cand = lo + step
        cb = rows8(as_score(cand))
        ok = count(lambda s, kpos: s >= cb) >= kf
        return jnp.where(ok, cand, lo), jnp.where(ok, width - step, step - 1)

    lo, _ = lax.fori_loop(0, 17, bisect, (key_t - ((1 << 15) + 1), jnp.full((1, tq), span, I32)))
    tau = jnp.where(found, lo, INT_MIN)
    tau_f = jnp.where(tau == INT_MIN, -jnp.inf, as_score(tau))
    tau8 = rows8(tau_f)
    c_ge = count(lambda s, kpos: s >= tau8)

    def tie_cut():
        need = kf - count(lambda s, kpos: s > tau8)

        def cut_bit(b, cut):
            cand = cut | (jnp.int32(1) << (12 - b))
            cb = rows8(cand)
            c = count(lambda s, kpos: (s == tau8) & (kpos < cb))
            return jnp.where(c <= need, cand, cut)

        return lax.fori_loop(0, 13, cut_bit, zero_i)

    cut = lax.cond(jnp.max(c_ge) > kf, tie_cut, lambda: jnp.full((1, tq), 2 ** 13, I32))

    mx_s[...] = jnp.full(mx_s.shape, NEG_BIG, F32)
    l_s[...] = jnp.zeros(l_s.shape, F32)
    acc_s[...] = jnp.zeros(acc_s.shape, F32)
    nlt = tk // LANES

    def lane_tiles(a):
        return [a[:, c * LANES:(c + 1) * LANES] for c in range(nlt)]

    def bcast(v):
        return jnp.broadcast_to(v, (v.shape[0], LANES))

    def row_max(j, _):
        off = pl.multiple_of(j * tk, tk)
        sc = sc_s[j]
        kpos = j * tk + krow
        sel = (sc > tau_f) | ((sc == tau_f) & (kpos < cut))
        sel = sel & (kpos <= i * tq + qcol)
        bias = jnp.where(sel, 0.0, NEG_BIG).T
        sc_s[j] = bias
        kvj = kv_ref[pl.ds(off, tk), :]
        vkj = vk_ref[pl.ds(off, tk), :]
        for par, kk in enumerate((kvj, vkj)):
            s = (_dot_nt(qd_s[par], kk).reshape(npair, tq, tk) + bias[None]).reshape(npair * tq, tk)
            mx_s[par] = functools.reduce(jnp.maximum, lane_tiles(s), mx_s[par])
        return 0

    over_chunks(lambda j: row_max(j, 0))
    for par in range(2):
        mx_s[par] = bcast(jnp.max(mx_s[par], -1, keepdims=True))

    def attend(j, _):
        off = pl.multiple_of(j * tk, tk)
        bias = sc_s[j]
        kvj = kv_ref[pl.ds(off, tk), :]
        vkj = vk_ref[pl.ds(off, tk), :]
        for par, kk in enumerate((kvj, vkj)):
            s = (_dot_nt(qd_s[par], kk).reshape(npair, tq, tk) + bias[None]).reshape(npair * tq, tk)
            m = mx_s[par]
            ps = [jnp.exp(t - m) for t in lane_tiles(s)]
            l_s[par] = functools.reduce(jnp.add, ps, l_s[par])
            acc_s[par] = acc_s[par] + _dot(jnp.concatenate(ps, 1).astype(BF16), kk)
        return 0

    over_chunks(lambda j: attend(j, 0))

    def out(h):
        rows = slice((h // 2) * tq, (h // 2 + 1) * tq)
        return acc_s[h % 2, rows, :] / jnp.sum(l_s[h % 2, rows, :], -1, keepdims=True)

    for pr in range(DSA_HEADS // 2):
        o_ref[:, pr * LANES:(pr + 1) * LANES] = jnp.where(lower, out(2 * pr + 1), out(2 * pr)).astype(BF16)


def _sparse_attention(ixq, ixwt, dsq, ikk, kv, vk, batch, seq, tq):
    T = batch * seq
    nq = seq // tq
    topk = min(DSA_TOPK_MAX, seq // 4)
    tile = lambda n: pl.BlockSpec((tq, n), lambda b, i: (b * nq + i, 0))
    full = pl.BlockSpec((seq, LANES), lambda b, i: (b, 0))
    return pl.pallas_call(
        functools.partial(_dsa_kernel, tq=tq, topk=topk),
        grid=(batch, nq),
        in_specs=[tile(512), pl.BlockSpec((ixwt.shape[0], tq), lambda b, i: (0, b * nq + i)), tile(512),
                  full, full, full],
        out_specs=tile(512),
        out_shape=jax.ShapeDtypeStruct((T, 512), BF16),
        scratch_shapes=[pltpu.VMEM((IDX_HEADS * tq, LANES), BF16),
                        pltpu.VMEM((2, DSA_HEADS // 2 * tq, LANES), BF16),
                        pltpu.VMEM((nq, tq, tq), F32),
                        pltpu.VMEM((nq, tq, tq), BF16),
                        pltpu.VMEM((2, DSA_HEADS // 2 * tq, LANES), F32),
                        pltpu.VMEM((2, DSA_HEADS // 2 * tq, LANES), F32),
                        pltpu.VMEM((2, DSA_HEADS // 2 * tq, LANES), F32)],
        compiler_params=_cparams(("parallel", "arbitrary")),
        name="sparse_attention",
    )(ixq, ixwt, dsq, ikk, kv, vk)


def _matmul_kernel(x_ref, w_ref, o_ref):
    o_ref[...] = _dot(x_ref[...].astype(BF16), w_ref[...]).astype(o_ref.dtype)


def _matmul(x, w, tm, out_dtype):
    M, K = x.shape
    N = w.shape[1]
    return pl.pallas_call(
        _matmul_kernel,
        grid=(M // tm,),
        in_specs=[pl.BlockSpec((tm, K), lambda i: (i, 0)), pl.BlockSpec((K, N), lambda i: (0, 0))],
        out_specs=pl.BlockSpec((tm, N), lambda i: (i, 0)),
        out_shape=jax.ShapeDtypeStruct((M, N), out_dtype),
        compiler_params=_cparams(("parallel",)),
        name="matmul",
    )(x, w)


def _memattn_kernel(q_ref, kv_ref, o_ref):
    scale = MEM_HEAD_DIM ** -0.5
    width = MEM_HEADS * MEM_HEAD_DIM
    for h in range(MEM_HEADS):
        sl = slice(h * MEM_HEAD_DIM, (h + 1) * MEM_HEAD_DIM)
        k = kv_ref[:, sl]
        v = kv_ref[:, width + h * MEM_HEAD_DIM:width + (h + 1) * MEM_HEAD_DIM]
        s = _dot_nt(q_ref[:, sl], k) * scale
        p = jnp.exp(s - jnp.max(s, -1, keepdims=True))
        o = _dot(p.astype(BF16), v) / jnp.sum(p, -1, keepdims=True)
        o_ref[:, sl] = o.astype(BF16)


def _memory_attention(memq, mkv, batch, seq, mem_len, tq):
    T = batch * seq
    nq = seq // tq
    return pl.pallas_call(
        _memattn_kernel,
        grid=(batch, nq),
        in_specs=[pl.BlockSpec((tq, 512), lambda b, i: (b * nq + i, 0)),
                  pl.BlockSpec((mem_len, 1024), lambda b, i: (b, 0))],
        out_specs=pl.BlockSpec((tq, 512), lambda b, i: (b * nq + i, 0)),
        out_shape=jax.ShapeDtypeStruct((T, 512), BF16),
        compiler_params=_cparams(("parallel", "parallel")),
        name="memory_attention",
    )(memq, mkv)


def _merge_kernel(x_ref, ya_ref, yb_ref, yc_ref, wg_ref, wa_ref, wb_ref, wc_ref, wo_ref,
                  g_ref, b_ref, h_ref, *, alpha):
    x = x_ref[...]
    xb = x.astype(BF16)
    d = x.shape[1]
    merged = None
    for br, (y_ref, w_ref) in enumerate(((ya_ref, wa_ref), (yb_ref, wb_ref), (yc_ref, wc_ref))):
        gate = jax.nn.sigmoid(_dot(xb, wg_ref[:, br * d:(br + 1) * d]))
        term = gate * _dot(y_ref[...], w_ref[...])
        merged = term if merged is None else merged + term
    mix = _dot(merged.astype(BF16), wo_ref[...])
    h_ref[...] = _layer_norm(alpha * x + mix, g_ref[...], b_ref[...])


def _merge_project_norm(x2d, ya, yb, yc, w_gates, w_ba, w_bb, w_bc, w_o, ln_g, ln_b, alpha, tm):
    T, D = x2d.shape
    row = lambda n: pl.BlockSpec((tm, n), lambda i: (i, 0))
    whole = lambda a: pl.BlockSpec(a.shape, lambda i: (0, 0))
    ws = [w.astype(BF16) for w in (w_gates, w_ba, w_bb, w_bc, w_o)]
    vec = [v.reshape(1, D).astype(F32) for v in (ln_g, ln_b)]
    return pl.pallas_call(
        functools.partial(_merge_kernel, alpha=alpha),
        grid=(T // tm,),
        in_specs=[row(D), row(512), row(512), row(512)] + [whole(w) for w in ws] + [whole(v) for v in vec],
        out_specs=row(D),
        out_shape=jax.ShapeDtypeStruct((T, D), F32),
        compiler_params=_cparams(("parallel",)),
        name="merge_project_norm",
    )(x2d, ya, yb, yc, *ws, *vec)


def _router_kernel(h_ref, wr_ref, bias_ref, eidx_ref, gate_ref, rank_ref, cnt_ref, carry_s, *, tm):
    @pl.when(pl.program_id(0) == 0)
    def _():
        carry_s[...] = jnp.zeros(carry_s.shape, F32)

    neg = -jnp.inf
    logits = _dot_nt(wr_ref[...], h_ref[...].astype(BF16))
    scores = jax.nn.sigmoid(logits)
    biased = scores + bias_ref[...]
    erow = lax.broadcasted_iota(I32, (N_EXPERTS, tm), 0)
    big = jnp.int32(2 ** 30)

    def top1(vals, ids):
        mx = jnp.max(vals, 0, keepdims=True)
        am = jnp.min(jnp.where(vals == mx, ids, big), 0, keepdims=True)
        return mx, am

    gs = []
    ids = lax.broadcasted_iota(I32, (GROUP_SIZE, tm), 0)
    for g in range(N_GROUPS):
        v = biased[g * GROUP_SIZE:(g + 1) * GROUP_SIZE]
        m1, a1 = top1(v, ids)
        m2, _ = top1(jnp.where(ids == a1, neg, v), ids)
        gs.append(m1 + m2)
    gscore = jnp.concatenate(gs, 0)
    grow = lax.broadcasted_iota(I32, (N_GROUPS, tm), 0)
    gsel = jnp.zeros((N_GROUPS, tm), F32)
    for _ in range(TOPK_GROUPS):
        _, ga = top1(gscore, grow)
        hit = grow == ga
        gsel = jnp.where(hit, 1.0, gsel)
        gscore = jnp.where(hit, neg, gscore)
    masked = jnp.concatenate(
        [jnp.where(gsel[g:g + 1] > 0.0, biased[g * GROUP_SIZE:(g + 1) * GROUP_SIZE], neg)
         for g in range(N_GROUPS)], 0)

    eids, ws = [], []
    hot = jnp.zeros((N_EXPERTS, tm), F32)
    for _ in range(TOP_K):
        _, ea = top1(masked, erow)
        hit = erow == ea
        eids.append(ea)
        ws.append(jnp.sum(jnp.where(hit, scores, 0.0), 0, keepdims=True))
        hot = jnp.where(hit, 1.0, hot)
        masked = jnp.where(hit, neg, masked)
    w = jnp.concatenate(ws, 0)
    eidx_ref[...] = jnp.concatenate(eids, 0)
    gate_ref[...] = w / jnp.sum(w, 0, keepdims=True) * ROUTED_SCALE

    r_i =lax.broadcasted_iota(I32, (tm, tm), 0)
    c_i = lax.broadcasted_iota(I32, (tm, tm), 1)
    before = jnp.where(r_i < c_i, 1.0, 0.0).astype(BF16)
    pos = carry_s[...] + _dot(hot.astype(BF16), before)
    rank_ref[...] = jnp.concatenate(
        [jnp.sum(jnp.where(erow == e, pos, 0.0), 0, keepdims=True) for e in eids], 0).astype(I32)
    carry_s[...] = carry_s[...] + jnp.sum(hot, 1, keepdims=True)
    cnt_ref[...] = carry_s[...].astype(I32)


def _route(h2d, w_router, router_bias, tm):
    T, D = h2d.shape
    out = lambda dt: jax.ShapeDtypeStruct((TOP_K, T), dt)
    blk = pl.BlockSpec((TOP_K, tm), lambda i: (0, i))
    return pl.pallas_call(
        functools.partial(_router_kernel, tm=tm),
        grid=(T // tm,),
        in_specs=[pl.BlockSpec((tm, D), lambda i: (i, 0)),
                  pl.BlockSpec((N_EXPERTS, D), lambda i: (0, 0)),
                  pl.BlockSpec((N_EXPERTS, 1), lambda i: (0, 0))],
        out_specs=[blk, blk, blk, pl.BlockSpec((N_EXPERTS, 1), lambda i: (0, 0))],
        out_shape=[out(I32), out(F32), out(I32), jax.ShapeDtypeStruct((N_EXPERTS, 1), I32)],
        scratch_shapes=[pltpu.VMEM((N_EXPERTS, 1), F32)],
        compiler_params=_cparams(("arbitrary",)),
        name="router",
    )(h2d, w_router.T.astype(BF16), router_bias.reshape(N_EXPERTS, 1).astype(F32))


def _dest_kernel(pstart_ref, eidx_ref, rank_ref, dest_ref):
    eidx = eidx_ref[...]

    def body(e, acc):
        return acc + jnp.where(eidx == e, pstart_ref[e], 0)

    dest_ref[...] = lax.fori_loop(0, N_EXPERTS, body, rank_ref[...])


def _destinations(pstart, eidx, rank, tm):
    T = eidx.shape[1]
    blk = lambda: pl.BlockSpec((TOP_K, tm), lambda i, ps: (0, i))
    return pl.pallas_call(
        _dest_kernel,
        grid_spec=pltpu.PrefetchScalarGridSpec(
            num_scalar_prefetch=1, grid=(T // tm,), in_specs=[blk(), blk()], out_specs=blk()),
        out_shape=jax.ShapeDtypeStruct((TOP_K, T), I32),
        compiler_params=_cparams(("parallel",)),
        name="destinations",
    )(pstart, eidx, rank)


def _dispatch_kernel(zfill_ref, dest_hbm, h_ref, xs_hbm, dest_s, hbuf, zbuf, dest_sem, row_sem, z_sem,
                     *, tm, rows):
    i = pl.program_id(0)
    n = pl.num_programs(0)
    nz = zfill_ref.shape[0]

    def dest_copy(blk, slot):
        return pltpu.make_async_copy(dest_hbm.at[blk], dest_s.at[pl.ds(slot * (TOP_K * tm), TOP_K * tm)],
                                     dest_sem.at[slot])

    def rows_done(slot):
        for _ in range(TOP_K):
            pltpu.make_async_copy(hbuf.at[slot], xs_hbm.at[pl.ds(0, tm)], row_sem.at[slot]).wait()

    @pl.when(i == 0)
    def _():
        zbuf[...] = jnp.zeros(zbuf.shape, F32)

        def zcopy(b):
            return pltpu.make_async_copy(zbuf, xs_hbm.at[pl.ds(zfill_ref[b] * rows, rows)], z_sem)

        def start(b, _):
            @pl.when(zfill_ref[b] >= 0)
            def _():
                zcopy(b).start()
            return 0

        def wait(b, _):
            @pl.when(zfill_ref[b] >= 0)
            def _():
                zcopy(b).wait()
            return 0

        lax.fori_loop(0, nz, start, 0)
        lax.fori_loop(0, nz, wait, 0)
        dest_copy(0, 0).start()

    slot = i % 2
    dest_copy(i, slot).wait()

    @pl.when(i + 1 < n)
    def _():
        dest_copy(i + 1, 1 - slot).start()

    @pl.when(i >= 2)
    def _():
        rows_done(slot)

    hbuf[slot] = h_ref[...]

    base = slot * (TOP_K * tm)

    def body(g, _):
        for rr in range(SUBLANES):
            r = g * SUBLANES + rr
            for k in range(TOP_K):
                d = dest_s[base + k * tm + r]
                pltpu.make_async_copy(hbuf.at[slot, pl.ds(r, 1)], xs_hbm.at[pl.ds(d, 1)],
                                      row_sem.at[slot]).start()
        return 0

    lax.fori_loop(0, tm // SUBLANES, body, 0)

    @pl.when(i == n - 1)
    def _():
        rows_done(slot)

        @pl.when(n > 1)
        def _():
            rows_done(1 - slot)


def _token_tiles(a, tm):
    k, T = a.shape
    return a.reshape(k, T // tm, tm).transpose(1, 0, 2).reshape(T // tm, k * tm)


def _dispatch(h2d, dest, zfill, n_blocks, rows, tm):
    T, D = h2d.shape
    return pl.pallas_call(
        functools.partial(_dispatch_kernel, tm=tm, rows=rows),
        grid_spec=pltpu.PrefetchScalarGridSpec(
            num_scalar_prefetch=1,
            grid=(T // tm,),
            in_specs=[pl.BlockSpec(memory_space=pl.ANY),
                      pl.BlockSpec((tm, D), lambda i, z: (i, 0))],
            out_specs=pl.BlockSpec(memory_space=pl.ANY),
            scratch_shapes=[pltpu.SMEM((2 * TOP_K * tm,), I32),
                            pltpu.VMEM((2, tm, D), F32),
                            pltpu.VMEM((rows, D), F32),
                            pltpu.SemaphoreType.DMA((2,)),
                            pltpu.SemaphoreType.DMA((2,)),
                            pltpu.SemaphoreType.DMA]),
        out_shape=jax.ShapeDtypeStruct((n_blocks * rows, D), F32),
        compiler_params=_cparams(("arbitrary",)),
        name="dispatch",
    )(zfill, _token_tiles(dest, tm), h2d)


def _experts_kernel(blk_e_ref, used_ref, x_ref, wg_ref, wu_ref, wd_ref, y_ref, wgb, wub, wdb):
    i = pl.program_id(0)
    e = blk_e_ref[i]
    prev = blk_e_ref[jnp.maximum(i - 1, 0)]

    @pl.when((i == 0) | (e != prev))
    def _():
        wgb[...] = wg_ref[0].astype(BF16)
        wub[...] = wu_ref[0].astype(BF16)
        wdb[...] = wd_ref[0].astype(BF16)

    @pl.when(i < used_ref[0])
    def _():
        x = x_ref[...].astype(BF16)
        hidden = jax.nn.silu(_dot(x, wgb[...])) * _dot(x, wub[...])
        y_ref[...] = _dot(hidden.astype(BF16), wdb[...])

    @pl.when(i >= used_ref[0])
    def _():
        y_ref[...] = jnp.zeros(y_ref.shape, F32)


def _routed_experts(xs, blk_e, n_used, w_eg, w_eu, w_ed, rows):
    P, D = xs.shape
    ff = w_eg.shape[-1]
    row_blk = lambda i, be, nu: (jnp.minimum(i, nu[0] - 1), 0)
    return pl.pallas_call(
        _experts_kernel,
        grid_spec=pltpu.PrefetchScalarGridSpec(
            num_scalar_prefetch=2,
            grid=(P // rows,),
            in_specs=[pl.BlockSpec((rows, D), row_blk),
                      pl.BlockSpec((1, D, ff), lambda i, be, nu: (be[i], 0, 0)),
                      pl.BlockSpec((1, D, ff), lambda i, be, nu: (be[i], 0, 0)),
                      pl.BlockSpec((1, ff, D), lambda i, be, nu: (be[i], 0, 0))],
            out_specs=pl.BlockSpec((rows, D), lambda i, be, nu: (i, 0)),
            scratch_shapes=[pltpu.VMEM((D, ff), BF16),
                            pltpu.VMEM((D, ff), BF16),
                            pltpu.VMEM((ff, D), BF16)]),
        out_shape=jax.ShapeDtypeStruct((P, D), F32),
        compiler_params=_cparams(("arbitrary",)),
        name="routed_experts",
    )(blk_e, n_used, xs, w_eg, w_eu, w_ed)


def _combine_kernel(dest_hbm, ys_hbm, h_ref, gate_ref, wsg_ref, wsu_ref, wsd_ref, g_ref, b_ref, o_ref,
                    dest_s, buf, dest_sem, row_sem, *, tm, alpha):
    i = pl.program_id(0)
    n = pl.num_programs(0)

    def dest_copy(blk, slot):
        return pltpu.make_async_copy(dest_hbm.at[blk], dest_s.at[pl.ds(slot * (TOP_K * tm), TOP_K * tm)],
                                     dest_sem.at[slot])

    def issue_rows(slot, unrolled=False):
        base = slot * (TOP_K * tm)

        def group(g):
            for rr in range(SUBLANES):
                r = g * SUBLANES + rr
                for k in range(TOP_K):
                    d = dest_s[base + k * tm + r]
                    pltpu.make_async_copy(ys_hbm.at[pl.ds(d, 1)], buf.at[slot, k, pl.ds(r, 1)],
                                          row_sem.at[slot]).start()
        if unrolled:
            for g in range(tm // SUBLANES):
                group(g)
        else:
            lax.fori_loop(0, tm // SUBLANES, lambda g, c: (group(g), 0)[1], 0)

    def rows_done(slot):
        for k in range(TOP_K):
            pltpu.make_async_copy(ys_hbm.at[pl.ds(0, tm)], buf.at[slot, k], row_sem.at[slot]).wait()

    @pl.when(i == 0)
    def _():
        dest_copy(0, 0).start()
        dest_copy(0, 0).wait()
        issue_rows(0)
        dest_copy(jnp.minimum(1, n - 1), 1).start()

    slot = i % 2
    nxt = 1 - slot
    dest_copy(jnp.minimum(i + 1, n - 1), nxt).wait()
    rows_done(slot)
    issue_rows(nxt, unrolled=True)
    h = h_ref[...]
    hb = h.astype(BF16)
    shared = _dot((jax.nn.silu(_dot(hb, wsg_ref[...])) * _dot(hb, wsu_ref[...])).astype(BF16), wsd_ref[...])
    total = alpha * h + shared
    for k in range(TOP_K):
        total = total + gate_ref[:, k:k + 1] * buf[slot, k]
    o_ref[...] = _layer_norm(total, g_ref[...], b_ref[...])
    dest_copy(jnp.minimum(i + 2, n - 1), slot).start()

    @pl.when(i == n - 1)
    def _():
        rows_done(nxt)
        dest_copy(n - 1, slot).wait()


def _combine_shared_norm(h2d, ys, dest, gate, w_sg, w_su, w_sd, ln_g, ln_b, alpha, tm):
    T, D = h2d.shape
    nt = T // tm
    ws = [w.astype(BF16) for w in (w_sg, w_su, w_sd)]
    vec = [v.reshape(1, D).astype(F32) for v in (ln_g, ln_b)]
    whole = lambda a: pl.BlockSpec(a.shape, lambda i: (0, 0))
    return pl.pallas_call(
        functools.partial(_combine_kernel, tm=tm, alpha=alpha),
        grid=(nt,),
        in_specs=[pl.BlockSpec(memory_space=pl.ANY), pl.BlockSpec(memory_space=pl.ANY),
                  pl.BlockSpec((tm, D), lambda i: (i, 0)),
                  pl.BlockSpec((tm, TOP_K), lambda i: (i, 0))] + [whole(w) for w in ws] + [whole(v) for v in vec],
        out_specs=pl.BlockSpec((tm, D), lambda i: (i, 0)),
        out_shape=jax.ShapeDtypeStruct((T, D), F32),
        scratch_shapes=[pltpu.SMEM((2 * TOP_K * tm,), I32),
                        pltpu.VMEM((2, TOP_K, tm, D), F32),
                        pltpu.SemaphoreType.DMA((2,)),
                        pltpu.SemaphoreType.DMA((2,))],
        compiler_params=_cparams(("arbitrary",)),
        name="combine_shared_norm",
    )(_token_tiles(dest, tm), ys, h2d, gate.T, *ws, *vec)


PROJ_TM = 512
DA_TQ, DA_TK = 1024, 512
DSA_TQ = 256
MEM_TQ = 512
MERGE_TM = 256
ROUTER_TM = 512
DEST_TM = 2048
DISPATCH_TM = 256
COMBINE_TM = 128
MOE_ROWS = 256


def _dispatch_plan(counts, n_tokens, rows):
    counts = counts.reshape(N_EXPERTS)
    padded = (counts + rows - 1) // rows * rows
    pend = jnp.cumsum(padded)
    pstart = pend - padded
    n_blocks = (n_tokens * TOP_K + N_EXPERTS * (rows - 1) + rows - 1) // rows
    blk = jnp.arange(n_blocks, dtype=I32)
    n_used = pend[-1] // rows
    blk_e = jnp.sum((jnp.minimum(blk, n_used - 1)[:, None] * rows >= pend[None, :]).astype(I32), 1)
    last_blk = jnp.where(padded > counts, pend // rows - 1, -1)
    tail_blk = jnp.where(blk >= n_used, blk, -1)
    zfill = jnp.concatenate([last_blk, tail_blk]).astype(I32)
    return pstart.astype(I32), blk_e, n_used.astype(I32).reshape(1), zfill, n_blocks


def _moe(h1, p, l, alpha):
    eidx, gate, rank, counts = _route(h1, p['w_router'][l], p['router_bias'][l], tm=ROUTER_TM)
    pstart, blk_e, n_used, zfill, n_blocks = _dispatch_plan(counts, h1.shape[0], MOE_ROWS)
    dest = _destinations(pstart, eidx, rank, tm=DEST_TM)
    xs = _dispatch(h1, dest, zfill, n_blocks, MOE_ROWS, tm=DISPATCH_TM)
    ys = _routed_experts(xs, blk_e, n_used, p['w_eg'][l], p['w_eu'][l], p['w_ed'][l], MOE_ROWS)
    return _combine_shared_norm(h1, ys, dest, gate, p['w_sg'][l], p['w_su'][l], p['w_sd'][l],
                                p['ln2_g'][l], p['ln2_b'][l], alpha, tm=COMBINE_TM)


def _layer(h2d, mem2d, p, l, depth, batch, seq, mem_len):
    alpha = (2 * depth) ** 0.25
    lam_init = 0.8 - 0.6 * math.exp(-0.3 * l)
    w_in = p['w_in'][l]
    daq, dak, dsq, ixq, kv, vk, ikk, dav, memq, ixw = _in_projection(h2d, w_in, seq, tm=min(PROJ_TM, seq))
    lam = (jnp.exp(jnp.sum(p['lq1'][l].astype(F32) * p['lk1'][l].astype(F32)))
           - jnp.exp(jnp.sum(p['lq2'][l].astype(F32) * p['lk2'][l].astype(F32))) + lam_init)
    ya = _diff_attention(daq, dak, dav, lam, p['subln_g'][l], batch, seq, lam_init, tq=min(DA_TQ, seq), tk=min(DA_TK, seq))
    yb = _sparse_attention(ixq, ixw, dsq, ikk, kv, vk, batch, seq, tq=min(DSA_TQ, seq))
    mkv = _matmul(mem2d, p['w_mem_kv'][l].astype(BF16), tm=mem_len, out_dtype=BF16)
    yc = _memory_attention(memq, mkv, batch, seq, mem_len, tq=min(MEM_TQ, seq))
    h1 = _merge_project_norm(h2d, ya, yb, yc, w_in[:, 3272:], p['w_ba'][l], p['w_bb'][l], p['w_bc'][l],
                             p['w_o'][l], p['ln1_g'][l], p['ln1_b'][l], alpha, tm=MERGE_TM)
    return _moe(h1, p, l, alpha)


def kernel(x, mem, w_in, da_lambda_q1, da_lambda_k1, da_lambda_q2, da_lambda_k2, da_subln_g, w_mem_kv, w_branch_a, w_branch_b, w_branch_c, w_out, ln1_g, ln1_b, w_router, router_bias, w_exp_gate, w_exp_up, w_exp_down, w_sh_gate, w_sh_up, w_sh_down, ln2_g, ln2_b):
    batch, seq, d = x.shape
    mem_len = mem.shape[1]
    depth = w_in.shape[0]
    p = dict(w_in=w_in, lq1=da_lambda_q1, lk1=da_lambda_k1, lq2=da_lambda_q2, lk2=da_lambda_k2,
             subln_g=da_subln_g, w_mem_kv=w_mem_kv, w_ba=w_branch_a, w_bb=w_branch_b, w_bc=w_branch_c,
             w_o=w_out, ln1_g=ln1_g, ln1_b=ln1_b, w_router=w_router, router_bias=router_bias,
             w_eg=w_exp_gate, w_eu=w_exp_up, w_ed=w_exp_down, w_sg=w_sh_gate, w_su=w_sh_up,
             w_sd=w_sh_down, ln2_g=ln2_g, ln2_b=ln2_b)
    h = x.reshape(batch * seq, d)
    mem2d = mem.reshape(batch * mem_len, d)
    for l in range(depth):
        h = _layer(h, mem2d, p, l, depth, batch, seq, mem_len)
    return h.reshape(batch, seq, d)
```

```python
import functools
import math

import jax
import jax.numpy as jnp
import numpy as np
from jax import lax
from jax.experimental import pallas as pl
from jax.experimental.pallas import tpu as pltpu

F32 = jnp.float32
BF16 = jnp.bfloat16
I32 = jnp.int32

LANES = 128
SUBLANES = 8
ROPE_THETA = 10000.0
LN_EPS = 1e-5

DA_HEADS = 4
DA_HEAD_DIM = 64
DSA_HEADS = 8
DSA_HEAD_DIM = 64
IDX_HEADS = 8
IDX_ROWS = 16
DSA_TOPK_MAX = 256
MEM_HEADS = 4
MEM_HEAD_DIM = 128

N_EXPERTS = 256
TOP_K = 8
N_GROUPS = 8
TOPK_GROUPS = 4
GROUP_SIZE = N_EXPERTS // N_GROUPS
ROUTED_SCALE = 2.5

NEG_BIG = -1e30
INT_MIN = -(2 ** 31)

VMEM_LIMIT = 56 * 1024 * 1024


def _cparams(sem):
    return pltpu.CompilerParams(dimension_semantics=sem, vmem_limit_bytes=VMEM_LIMIT)


def _dot(a, b):
    return jnp.dot(a, b, preferred_element_type=F32)


def _dot_nt(a, b):
    return lax.dot_general(a, b, (((1,), (1,)), ((), ())), preferred_element_type=F32)


def _tree(op, xs):
    xs = list(xs)
    while len(xs) > 1:
        xs = [op(xs[k], xs[k + 1]) if k + 1 < len(xs) else xs[k] for k in range(0, len(xs), 2)]
    return xs[0]


def _layer_norm(x, g, b):
    mu = jnp.mean(x, -1, keepdims=True)
    xc = x - mu
    var = jnp.mean(xc * xc, -1, keepdims=True)
    return xc * lax.rsqrt(var + LN_EPS) * g + b


def _inproj_kernel(x_ref, wr_ref, wn_ref, wixt_ref, tab_ref,
                   daq_ref, dak_ref, dsq_ref, ixq_ref, kv_ref, vk_ref, ikk_ref,
                   dav_ref, memq_ref, ixwt_ref):
    xb = x_ref[...].astype(BF16)

    def rope(t, kind):
        c = tab_ref[3 * kind]
        sa = tab_ref[3 * kind + 1]
        sb = tab_ref[3 * kind + 2]
        return t * c + pltpu.roll(t, 96, 1) * sa + pltpu.roll(t, 32, 1) * sb

    groups = ((daq_ref, 0.125), (dak_ref, 1.0), (dsq_ref, 0.125), (ixq_ref, 1.0))
    for gi, (ref, scale) in enumerate(groups):
        y = _dot(xb, wr_ref[:, gi * 512:(gi + 1) * 512])
        for c in range(4):
            r = rope(y[:, c * LANES:(c + 1) * LANES], 0)
            if scale != 1.0:
                r = r * scale
            ref[:, c * LANES:(c + 1) * LANES] = r.astype(BF16)
    y = _dot(xb, wr_ref[:, 2048:2432])
    kv_ref[...] = rope(y[:, 0:128], 1).astype(BF16)
    vk_ref[...] = rope(y[:, 128:256], 2).astype(BF16)
    ikk_ref[...] = rope(y[:, 256:384], 0).astype(BF16)
    y = _dot(xb, wn_ref[...])
    dav_ref[...] = y[:, 0:512].astype(BF16)
    memq_ref[...] = y[:, 512:1024].astype(BF16)
    ixwt_ref[...] = _dot_nt(wixt_ref[...], xb)


def _rope_tables(seq):
    dim = 64
    inv = 1.0 / (ROPE_THETA ** (jnp.arange(0, dim, 2, dtype=F32) / dim))
    ang = jnp.arange(seq, dtype=F32)[:, None] * inv[None, :]
    ang = jnp.concatenate([ang, ang], -1)
    cos, sin = jnp.cos(ang), jnp.sin(ang)
    local = jnp.arange(dim)[None, :]
    sa = jnp.where(local < dim // 2, -sin, 0.0)
    sb = jnp.where(local >= dim // 2, sin, 0.0)
    one, zero = jnp.ones_like(cos), jnp.zeros_like(cos)
    cat = lambda a, b: jnp.concatenate([a, b], -1)
    return jnp.stack([cat(cos, cos), cat(sa, sa), cat(sb, sb),
                      cat(cos, one), cat(sa, zero), cat(sb, zero),
                      cat(one, cos), cat(zero, sa), cat(zero, sb)], 0)


def _in_projection(x2d, w_in, seq, tm):
    T, D = x2d.shape
    o = np.cumsum([0, 512, 512, 512, 512, 64, 64, 512, 64, 8, 512]).tolist()
    col = lambda i: w_in[:, o[i]:o[i + 1]]
    da_q, da_k, da_v, ds_q, ds_k, ds_v, ix_q, ix_k, ix_w, mem_q = [col(i) for i in range(10)]
    ds_q = ds_q.reshape(D, DSA_HEADS // 2, 2, DSA_HEAD_DIM)[:, :, ::-1, :].reshape(D, 512)
    w_rope = jnp.concatenate([da_q, da_k, ds_q, ix_q, ds_k, ds_v, ds_v, ds_k, ix_k, ix_k], 1).astype(BF16)
    w_plain = jnp.concatenate([da_v, mem_q], 1).astype(BF16)
    w_ixt = jnp.concatenate([ix_w.T, jnp.zeros((IDX_ROWS - IDX_HEADS, D), w_in.dtype)], 0).astype(BF16)
    tabs = _rope_tables(seq)
    nseq = seq // tm
    bf = lambda n: jax.ShapeDtypeStruct((T, n), BF16)
    row = lambda n: pl.BlockSpec((tm, n), lambda i: (i, 0))
    return pl.pallas_call(
        _inproj_kernel,
        grid=(T // tm,),
        in_specs=[row(D),
                  pl.BlockSpec(w_rope.shape, lambda i: (0, 0)),
                  pl.BlockSpec(w_plain.shape, lambda i: (0, 0)),
                  pl.BlockSpec(w_ixt.shape, lambda i: (0, 0)),
                  pl.BlockSpec((9, tm, LANES), lambda i: (0, i % nseq, 0))],
        out_specs=[row(512), row(512), row(512), row(512), row(128), row(128), row(128),
                   row(512), row(512), pl.BlockSpec((IDX_ROWS, tm), lambda i: (0, i))],
        out_shape=[bf(512), bf(512), bf(512), bf(512), bf(128), bf(128), bf(128),
                   bf(512), bf(512), jax.ShapeDtypeStruct((IDX_ROWS, T), F32)],
        compiler_params=_cparams(("parallel",)),
        name="in_projection",
    )(x2d, w_rope, w_plain, w_ixt, tabs)


def _diffattn_kernel(lam_ref, q_ref, k_ref, v_ref, g_ref, o_ref, m_s, l_s, a_s, *, tq, tk, lam_init):
    i = pl.program_id(2)
    lam = lam_ref[0]
    q = q_ref[...]
    lane = lax.broadcasted_iota(I32, q.shape, 1)
    zero = jnp.zeros_like(q)
    qs = (jnp.where(lane < DA_HEAD_DIM, q, zero), jnp.where(lane >= DA_HEAD_DIM, q, zero))
    m_s[...] = jnp.full(m_s.shape, -jnp.inf, F32)
    l_s[...] = jnp.zeros(l_s.shape, F32)
    a_s[...] = jnp.zeros(a_s.shape, F32)

    def step(j, masked, r0=0):
        off = pl.multiple_of(j * tk, tk)
        kj = k_ref[pl.ds(off, tk), :]
        vj = v_ref[pl.ds(off, tk), :]
        for mp in range(2):
            s = _dot_nt(qs[mp][r0:], kj)
            if masked:
                row = lax.broadcasted_iota(I32, (tq - r0, tk), 0) + r0
                col = lax.broadcasted_iota(I32, (tq - r0, tk), 1)
                s = jnp.where(j * tk + col <= i * tq + row, s, -jnp.inf)
            tiles = [s[:, c * LANES:(c + 1) * LANES] for c in range(tk // LANES)]
            m_old = m_s[mp, r0:]
            m_blk = jnp.max(_tree(jnp.maximum, tiles), -1, keepdims=True)
            m_new = jnp.maximum(m_old, jnp.broadcast_to(m_blk, (tq - r0, LANES)))
            ps = [jnp.exp(t - m_new) for t in tiles]
            al = jnp.exp(m_old - m_new)
            l_s[mp, r0:] = al * l_s[mp, r0:] + _tree(jnp.add, ps)
            a_s[mp, r0:] = al * a_s[mp, r0:] + _dot(jnp.concatenate(ps, 1).astype(BF16), vj)
            m_s[mp, r0:] = m_new
        return 0

    per_q = tq // tk

    def fully_visible(t, _):
        for jj in range(per_q):
            step(t * per_q + jj, False)
        return 0

    lax.fori_loop(0, i, fully_visible, 0)
    for jj in range(per_q):
        step(i * per_q + jj, True, jj * tk)
    o1 = a_s[0] / jnp.sum(l_s[0], -1, keepdims=True)
    o2 = a_s[1] / jnp.sum(l_s[1], -1, keepdims=True)
    o = o1 - lam * o2
    y = o * lax.rsqrt(jnp.mean(o * o, -1, keepdims=True) + LN_EPS) * g_ref[...] * (1.0 - lam_init)
    o_ref[...] = y.astype(BF16)


def _diff_attention(daq, dak, dav, lam, subln_g, batch, seq, lam_init, tq, tk):
    T = batch * seq
    nq = seq // tq
    state = pltpu.VMEM((2, tq, LANES), F32)
    return pl.pallas_call(
        functools.partial(_diffattn_kernel, tq=tq, tk=tk, lam_init=lam_init),
        grid=(batch, DA_HEADS, nq),
        in_specs=[pl.BlockSpec(memory_space=pltpu.SMEM),
                  pl.BlockSpec((tq, LANES), lambda b, h, i: (b * nq + i, h)),
                  pl.BlockSpec((seq, LANES), lambda b, h, i: (b, h)),
                  pl.BlockSpec((seq, LANES), lambda b, h, i: (b, h)),
                  pl.BlockSpec((1, LANES), lambda b, h, i: (0, 0))],
        out_specs=pl.BlockSpec((tq, LANES), lambda b, h, i: (b * nq + i, h)),
        out_shape=jax.ShapeDtypeStruct((T, DA_HEADS * LANES), BF16),
        scratch_shapes=[state, state, state],
        compiler_params=_cparams(("parallel", "parallel", "arbitrary")),
        name="diff_attention",
    )(lam.reshape(1), daq, dak, dav, subln_g.reshape(1, LANES).astype(F32))


def _dsa_kernel(ixq_ref, ixwt_ref, dsq_ref, ikk_ref, kv_ref, vk_ref, o_ref,
                qi_s, qd_s, sc_s, g_s, mx_s, l_s, acc_s, *, tq, topk):
    i = pl.program_id(1)
    nblk = i + 1
    tk = tq
    nslab = tk // 8
    lane = lax.broadcasted_iota(I32, (tq, LANES), 1)
    lower = lane < DSA_HEAD_DIM
    npair = DSA_HEADS // 2
    for h in range(IDX_HEADS):
        sl = slice((h // 2) * LANES, (h // 2 + 1) * LANES)
        keep = lower if h % 2 == 0 else jnp.logical_not(lower)
        blk = ixq_ref[:, sl]
        qi_s[h * tq:(h + 1) * tq, :] = jnp.where(keep, blk, jnp.zeros_like(blk))
        blk = dsq_ref[:, sl]
        qd_s[h % 2, (h // 2) * tq:(h // 2 + 1) * tq, :] = jnp.where(keep, blk, jnp.zeros_like(blk))
    krow = lax.broadcasted_iota(I32, (tk, tq), 0)
    qcol = lax.broadcasted_iota(I32, (tk, tq), 1)
    sub8 = lax.broadcasted_iota(I32, (8, tq), 0)

    def score_chunk(j, _):
        off = pl.multiple_of(j * tk, tk)
        ik = ikk_ref[pl.ds(off, tk), :]
        logits = _dot_nt(ik, qi_s[...])
        sc = jnp.zeros((tk, tq), F32)
        for h in range(IDX_HEADS):
            sc = sc + ixwt_ref[h:h + 1, :] * jnp.maximum(logits[:, h * tq:(h + 1) * tq], 0.0)
        sc = jnp.where(j * tk + krow <= i * tq + qcol, sc, -jnp.inf)
        sc_s[j] = sc
        g_s[j] = sc.astype(BF16)
        return 0

    def over_chunks(body, width=4):
        def group(jg, _):
            for u in range(width):
                body(width * jg + u)
            return 0
        lax.fori_loop(0, nblk // width, group, 0)
        lax.fori_loop(nblk // width * width, nblk, lambda j, c: (body(j), 0)[1], 0)

    over_chunks(lambda j: score_chunk(j, 0))

    n_acc = 4

    def count(pred):
        def body(j, accs):
            accs = list(accs)
            for r in range(nslab):
                hit = pred(sc_s[j, r * 8:(r + 1) * 8, :], j * tk + r * 8 + sub8)
                accs[r % n_acc] = accs[r % n_acc] + hit.astype(F32)
            return tuple(accs)
        accs = lax.fori_loop(0, nblk, body, tuple(jnp.zeros((8, tq), F32) for _ in range(n_acc)))
        return jnp.sum(_tree(jnp.add, accs), 0, keepdims=True)

    def rows8(v):
        return jnp.broadcast_to(v, (8, tq))

    def as_score(key):
        return pltpu.bitcast(key ^ ((key >> 31) & 0x7FFFFFFF), F32)

    def count_coarse(cand):
        cb = jnp.broadcast_to(cand, (16, tq))
        one, nil = jnp.ones((16, tq), BF16), jnp.zeros((16, tq), BF16)

        def body(j, accs):
            accs = list(accs)
            for r in range(tk // 16):
                hit = g_s[j, r * 16:(r + 1) * 16, :] >= cb
                accs[r % n_acc] = accs[r % n_acc] + jnp.where(hit, one, nil)
            return tuple(accs)
        accs = lax.fori_loop(0, nblk, body, tuple(nil for _ in range(n_acc)))
        return jnp.sum(_tree(jnp.add, [a.astype(F32) for a in accs]), 0, keepdims=True)

    def as_coarse(key16):
        bits16 = key16 ^ ((key16 >> 15) & 0x7FFF)
        return pltpu.bitcast(bits16 << 16, F32).astype(BF16)

    kf = float(topk)
    zero_i = jnp.zeros((1, tq), I32)

    def coarse_bit(b, hi):
        cand = jnp.where(b == 0, zero_i, hi | (jnp.int32(1) << (15 - b)))
        c = count_coarse(as_coarse(cand))
        return jnp.where(c >= kf, cand, hi)

    hi16 = lax.fori_loop(0, 16, coarse_bit, jnp.full((1, tq), -(2 ** 15), I32))
    found = hi16 != -(2 ** 15)
    key_t = jnp.where(found, (hi16 << 16) + jnp.where(hi16 < 0, 0xFFFF, 0), 0)
    span = (1 << 16) + (1 << 15) + 2

    def bisect(_, bounds):
        lo, width = bounds
        step = (width + 1) >> 1
        cand = lo + step
        cb = rows8(as_score(cand))
        ok = count(lambda s, kpos: s >= cb) >= kf
        return jnp.where(ok, cand, lo), jnp.where(ok, width - step, step - 1)

    lo, _ = lax.fori_loop(0, 17, bisect, (key_t - ((1 << 15) + 1), jnp.full((1, tq), span, I32)))
    tau = jnp.where(found, lo, INT_MIN)
    tau_f = jnp.where(tau == INT_MIN, -jnp.inf, as_score(tau))
    tau8 = rows8(tau_f)
    c_ge = count(lambda s, kpos: s >= tau8)

    def tie_cut():
        need = kf - count(lambda s, kpos: s > tau8)

        def cut_bit(b, cut):
            cand = cut | (jnp.int32(1) << (12 - b))
            cb = rows8(cand)
            c = count(lambda s, kpos: (s == tau8) & (kpos < cb))
            return jnp.where(c <= need, cand, cut)

        return lax.fori_loop(0, 13, cut_bit, zero_i)

    cut = lax.cond(jnp.max(c_ge) > kf, tie_cut, lambda: jnp.full((1, tq), 2 ** 13, I32))

    mx_s[...] = jnp.full(mx_s.shape, NEG_BIG, F32)
    l_s[...] = jnp.zeros(l_s.shape, F32)
    acc_s[...] = jnp.zeros(acc_s.shape, F32)
    nlt = tk // LANES

    def lane_tiles(a):
        return [a[:, c * LANES:(c + 1) * LANES] for c in range(nlt)]

    def bcast(v):
        return jnp.broadcast_to(v, (v.shape[0], LANES))

    def row_max(j, _):
        off = pl.multiple_of(j * tk, tk)
        sc = sc_s[j]
        kpos = j * tk + krow
        sel = (sc > tau_f) | ((sc == tau_f) & (kpos < cut))
        sel = sel & (kpos <= i * tq + qcol)
        bias = jnp.where(sel, 0.0, NEG_BIG).T
        sc_s[j] = bias
        kvj = kv_ref[pl.ds(off, tk), :]
        vkj = vk_ref[pl.ds(off, tk), :]
        for par, kk in enumerate((kvj, vkj)):
            s = (_dot_nt(qd_s[par], kk).reshape(npair, tq, tk) + bias[None]).reshape(npair * tq, tk)
            mx_s[par] = functools.reduce(jnp.maximum, lane_tiles(s), mx_s[par])
        return 0

    over_chunks(lambda j: row_max(j, 0))
    for par in range(2):
        mx_s[par] = bcast(jnp.max(mx_s[par], -1, keepdims=True))

    def attend(j, _):
        off = pl.multiple_of(j * tk, tk)
        bias = sc_s[j]
        kvj = kv_ref[pl.ds(off, tk), :]
        vkj = vk_ref[pl.ds(off, tk), :]
        for par, kk in enumerate((kvj, vkj)):
            s = (_dot_nt(qd_s[par], kk).reshape(npair, tq, tk) + bias[None]).reshape(npair * tq, tk)
            m = mx_s[par]
            ps = [jnp.exp(t - m) for t in lane_tiles(s)]
            l_s[par] = functools.reduce(jnp.add, ps, l_s[par])
            acc_s[par] = acc_s[par] + _dot(jnp.concatenate(ps, 1).astype(BF16), kk)
        return 0

    over_chunks(lambda j: attend(j, 0))

    def out(h):
        rows = slice((h // 2) * tq, (h // 2 + 1) * tq)
        return acc_s[h % 2, rows, :] / jnp.sum(l_s[h % 2, rows, :], -1, keepdims=True)

    for pr in range(DSA_HEADS // 2):
        o_ref[:, pr * LANES:(pr + 1) * LANES] = jnp.where(lower, out(2 * pr + 1), out(2 * pr)).astype(BF16)


def _sparse_attention(ixq, ixwt, dsq, ikk, kv, vk, batch, seq, tq):
    T = batch * seq
    nq = seq // tq
    topk = min(DSA_TOPK_MAX, seq // 4)
    tile = lambda n: pl.BlockSpec((tq, n), lambda b, i: (b * nq + i, 0))
    full = pl.BlockSpec((seq, LANES), lambda b, i: (b, 0))
    return pl.pallas_call(
        functools.partial(_dsa_kernel, tq=tq, topk=topk),
        grid=(batch, nq),
        in_specs=[tile(512), pl.BlockSpec((ixwt.shape[0], tq), lambda b, i: (0, b * nq + i)), tile(512),
                  full, full, full],
        out_specs=tile(512),
        out_shape=jax.ShapeDtypeStruct((T, 512), BF16),
        scratch_shapes=[pltpu.VMEM((IDX_HEADS * tq, LANES), BF16),
                        pltpu.VMEM((2, DSA_HEADS // 2 * tq, LANES), BF16),
                        pltpu.VMEM((nq, tq, tq), F32),
                        pltpu.VMEM((nq, tq, tq), BF16),
                        pltpu.VMEM((2, DSA_HEADS // 2 * tq, LANES), F32),
                        pltpu.VMEM((2, DSA_HEADS // 2 * tq, LANES), F32),
                        pltpu.VMEM((2, DSA_HEADS // 2 * tq, LANES), F32)],
        compiler_params=_cparams(("parallel", "arbitrary")),
        name="sparse_attention",
    )(ixq, ixwt, dsq, ikk, kv, vk)


def _matmul_kernel(x_ref, w_ref, o_ref):
    o_ref[...] = _dot(x_ref[...].astype(BF16), w_ref[...]).astype(o_ref.dtype)


def _matmul(x, w, tm, out_dtype):
    M, K = x.shape
    N = w.shape[1]
    return pl.pallas_call(
        _matmul_kernel,
        grid=(M // tm,),
        in_specs=[pl.BlockSpec((tm, K), lambda i: (i, 0)), pl.BlockSpec((K, N), lambda i: (0, 0))],
        out_specs=pl.BlockSpec((tm, N), lambda i: (i, 0)),
        out_shape=jax.ShapeDtypeStruct((M, N), out_dtype),
        compiler_params=_cparams(("parallel",)),
        name="matmul",
    )(x, w)


def _memattn_kernel(q_ref, kv_ref, o_ref):
    scale = MEM_HEAD_DIM ** -0.5
    width = MEM_HEADS * MEM_HEAD_DIM
    for h in range(MEM_HEADS):
        sl = slice(h * MEM_HEAD_DIM, (h + 1) * MEM_HEAD_DIM)
        k = kv_ref[:, sl]
        v = kv_ref[:, width + h * MEM_HEAD_DIM:width + (h + 1) * MEM_HEAD_DIM]
        s = _dot_nt(q_ref[:, sl], k) * scale
        p = jnp.exp(s - jnp.max(s, -1, keepdims=True))
        o = _dot(p.astype(BF16), v) / jnp.sum(p, -1, keepdims=True)
        o_ref[:, sl] = o.astype(BF16)


def _memory_attention(memq, mkv, batch, seq, mem_len, tq):
    T = batch * seq
    nq = seq // tq
    return pl.pallas_call(
        _memattn_kernel,
        grid=(batch, nq),
        in_specs=[pl.BlockSpec((tq, 512), lambda b, i: (b * nq + i, 0)),
                  pl.BlockSpec((mem_len, 1024), lambda b, i: (b, 0))],
        out_specs=pl.BlockSpec((tq, 512), lambda b, i: (b * nq + i, 0)),
        out_shape=jax.ShapeDtypeStruct((T, 512), BF16),
        compiler_params=_cparams(("parallel", "parallel")),
        name="memory_attention",
    )(memq, mkv)


def _merge_kernel(x_ref, ya_ref, yb_ref, yc_ref, wg_ref, wa_ref, wb_ref, wc_ref, wo_ref,
                  g_ref, b_ref, h_ref, *, alpha):
    x = x_ref[...]
    xb = x.astype(BF16)
    d = x.shape[1]
    merged = None
    for br, (y_ref, w_ref) in enumerate(((ya_ref, wa_ref), (yb_ref, wb_ref), (yc_ref, wc_ref))):
        gate = jax.nn.sigmoid(_dot(xb, wg_ref[:, br * d:(br + 1) * d]))
        term = gate * _dot(y_ref[...], w_ref[...])
        merged = term if merged is None else merged + term
    mix = _dot(merged.astype(BF16), wo_ref[...])
    h_ref[...] = _layer_norm(alpha * x + mix, g_ref[...], b_ref[...])


def _merge_project_norm(x2d, ya, yb, yc, w_gates, w_ba, w_bb, w_bc, w_o, ln_g, ln_b, alpha, tm):
    T, D = x2d.shape
    row = lambda n: pl.BlockSpec((tm, n), lambda i: (i, 0))
    whole = lambda a: pl.BlockSpec(a.shape, lambda i: (0, 0))
    ws = [w.astype(BF16) for w in (w_gates, w_ba, w_bb, w_bc, w_o)]
    vec = [v.reshape(1, D).astype(F32) for v in (ln_g, ln_b)]
    return pl.pallas_call(
        functools.partial(_merge_kernel, alpha=alpha),
        grid=(T // tm,),
        in_specs=[row(D), row(512), row(512), row(512)] + [whole(w) for w in ws] + [whole(v) for v in vec],
        out_specs=row(D),
        out_shape=jax.ShapeDtypeStruct((T, D), F32),
        compiler_params=_cparams(("parallel",)),
        name="merge_project_norm",
    )(x2d, ya, yb, yc, *ws, *vec)


def _router_kernel(h_ref, wr_ref, bias_ref, eidx_ref, gate_ref, rank_ref, cnt_ref, carry_s, *, tm):
    @pl.when(pl.program_id(0) == 0)
    def _():
        carry_s[...] = jnp.zeros(carry_s.shape, F32)

    neg = -jnp.inf
    logits = _dot_nt(wr_ref[...], h_ref[...].astype(BF16))
    scores = jax.nn.sigmoid(logits)
    biased = scores + bias_ref[...]
    erow = lax.broadcasted_iota(I32, (N_EXPERTS, tm), 0)
    big = jnp.int32(2 ** 30)

    def top1(vals, ids):
        mx = jnp.max(vals, 0, keepdims=True)
        am = jnp.min(jnp.where(vals == mx, ids, big), 0, keepdims=True)
        return mx, am

    gs = []
    ids = lax.broadcasted_iota(I32, (GROUP_SIZE, tm), 0)
    for g in range(N_GROUPS):
        v = biased[g * GROUP_SIZE:(g + 1) * GROUP_SIZE]
        m1, a1 = top1(v, ids)
        m2, _ = top1(jnp.where(ids == a1, neg, v), ids)
        gs.append(m1 + m2)
    gscore = jnp.concatenate(gs, 0)
    grow = lax.broadcasted_iota(I32, (N_GROUPS, tm), 0)
    gsel = jnp.zeros((N_GROUPS, tm), F32)
    for _ in range(TOPK_GROUPS):
        _, ga = top1(gscore, grow)
        hit = grow == ga
        gsel = jnp.where(hit, 1.0, gsel)
        gscore = jnp.where(hit, neg, gscore)
    masked = jnp.concatenate(
        [jnp.where(gsel[g:g + 1] > 0.0, biased[g * GROUP_SIZE:(g + 1) * GROUP_SIZE], neg)
         for g in range(N_GROUPS)], 0)

    eids, ws = [], []
    hot = jnp.zeros((N_EXPERTS, tm), F32)
    for _ in range(TOP_K):
        _, ea = top1(masked, erow)
        hit = erow == ea
        eids.append(ea)
        ws.append(jnp.sum(jnp.where(hit, scores, 0.0), 0, keepdims=True))
        hot = jnp.where(hit, 1.0, hot)
        masked = jnp.where(hit, neg, masked)
    w = jnp.concatenate(ws, 0)
    eidx_ref[...] = jnp.concatenate(eids, 0)
    gate_ref[...] = w / jnp.sum(w, 0, keepdims=True) * ROUTED_SCALE

    r_i =lax.broadcasted_iota(I32, (tm, tm), 0)
    c_i = lax.broadcasted_iota(I32, (tm, tm), 1)
    before = jnp.where(r_i < c_i, 1.0, 0.0).astype(BF16)
    pos = carry_s[...] + _dot(hot.astype(BF16), before)
    rank_ref[...] = jnp.concatenate(
        [jnp.sum(jnp.where(erow == e, pos, 0.0), 0, keepdims=True) for e in eids], 0).astype(I32)
    carry_s[...] = carry_s[...] + jnp.sum(hot, 1, keepdims=True)
    cnt_ref[...] = carry_s[...].astype(I32)


def _route(h2d, w_router, router_bias, tm):
    T, D = h2d.shape
    out = lambda dt: jax.ShapeDtypeStruct((TOP_K, T), dt)
    blk = pl.BlockSpec((TOP_K, tm), lambda i: (0, i))
    return pl.pallas_call(
        functools.partial(_router_kernel, tm=tm),
        grid=(T // tm,),
        in_specs=[pl.BlockSpec((tm, D), lambda i: (i, 0)),
                  pl.BlockSpec((N_EXPERTS, D), lambda i: (0, 0)),
                  pl.BlockSpec((N_EXPERTS, 1), lambda i: (0, 0))],
        out_specs=[blk, blk, blk, pl.BlockSpec((N_EXPERTS, 1), lambda i: (0, 0))],
        out_shape=[out(I32), out(F32), out(I32), jax.ShapeDtypeStruct((N_EXPERTS, 1), I32)],
        scratch_shapes=[pltpu.VMEM((N_EXPERTS, 1), F32)],
        compiler_params=_cparams(("arbitrary",)),
        name="router",
    )(h2d, w_router.T.astype(BF16), router_bias.reshape(N_EXPERTS, 1).astype(F32))


def _dest_kernel(pstart_ref, eidx_ref, rank_ref, dest_ref):
    eidx = eidx_ref[...]

    def body(e, acc):
        return acc + jnp.where(eidx == e, pstart_ref[e], 0)

    dest_ref[...] = lax.fori_loop(0, N_EXPERTS, body, rank_ref[...])


def _destinations(pstart, eidx, rank, tm):
    T = eidx.shape[1]
    blk = lambda: pl.BlockSpec((TOP_K, tm), lambda i, ps: (0, i))
    return pl.pallas_call(
        _dest_kernel,
        grid_spec=pltpu.PrefetchScalarGridSpec(
            num_scalar_prefetch=1, grid=(T // tm,), in_specs=[blk(), blk()], out_specs=blk()),
        out_shape=jax.ShapeDtypeStruct((TOP_K, T), I32),
        compiler_params=_cparams(("parallel",)),
        name="destinations",
    )(pstart, eidx, rank)


def _token_tiles(a, tm):
    k, T = a.shape
    return a.reshape(k, T // tm, tm).transpose(1, 0, 2).reshape(T // tm, k * tm)


TOK_ALIGN = 128


def _experts_kernel(blk_e_ref, blk_off_ref, used_ref, tok_hbm, h_hbm, wg_ref, wu_ref, wd_ref, y_ref,
                    tok_s, xbuf, wgb, wub, wdb, tok_sem, row_sem, *, rows):
    i = pl.program_id(0)
    used = used_ref[0]
    win = rows + TOK_ALIGN

    def window(blk):
        return pl.multiple_of(blk_off_ref[blk] // TOK_ALIGN * TOK_ALIGN, TOK_ALIGN)

    def tok_copy(blk, slot):
        return pltpu.make_async_copy(tok_hbm.at[pl.ds(window(blk), win)], tok_s.at[pl.ds(slot * win, win)],
                                     tok_sem.at[slot])

    def issue_rows(blk, slot, unrolled):
        base = slot * win + blk_off_ref[blk] - window(blk)

        def group(g):
            for rr in range(SUBLANES):
                r = g * SUBLANES + rr
                t = tok_s[base + r]
                pltpu.make_async_copy(h_hbm.at[pl.ds(t, 1)], xbuf.at[slot, pl.ds(r, 1)], row_sem.at[slot]).start()
        if unrolled:
            for g in range(rows // SUBLANES):
                group(g)
        else:
            lax.fori_loop(0, rows // SUBLANES, lambda g, c: (group(g), 0)[1], 0)

    def rows_done(slot):
        pltpu.make_async_copy(h_hbm.at[pl.ds(0, rows)], xbuf.at[slot], row_sem.at[slot]).wait()

    @pl.when(i == 0)
    def _():
        tok_copy(0, 0).start()
        tok_copy(0, 0).wait()
        issue_rows(0, 0, False)
        tok_copy(jnp.minimum(1, used - 1), 1).start()

    e = blk_e_ref[i]
    prev = blk_e_ref[jnp.maximum(i - 1, 0)]

    @pl.when((i == 0) | (e != prev))
    def _():
        wgb[...] = wg_ref[0].astype(BF16)
        wub[...] = wu_ref[0].astype(BF16)
        wdb[...] = wd_ref[0].astype(BF16)

    slot = i % 2
    nxt = 1 - slot

    @pl.when(i < used)
    def _():
        after = jnp.minimum(i + 1, used - 1)
        tok_copy(after, nxt).wait()
        rows_done(slot)
        issue_rows(after, nxt, True)
        x = xbuf[slot].astype(BF16)
        hidden = jax.nn.silu(_dot(x, wgb[...])) * _dot(x, wub[...])
        y_ref[...] = _dot(hidden.astype(BF16), wdb[...])
        tok_copy(jnp.minimum(i + 2, used - 1), slot).start()

    @pl.when(i == used - 1)
    def _():
        rows_done(nxt)
        tok_copy(used - 1, slot).wait()

    @pl.when(i >= used)
    def _():
        y_ref[...] = jnp.zeros(y_ref.shape, F32)


def _routed_experts(h2d, sorted_tok, blk_e, blk_off, n_used, n_blocks, w_eg, w_eu, w_ed, rows):
    T, D = h2d.shape
    ff = w_eg.shape[-1]
    return pl.pallas_call(
        functools.partial(_experts_kernel, rows=rows),
        grid_spec=pltpu.PrefetchScalarGridSpec(
            num_scalar_prefetch=3,
            grid=(n_blocks,),
            in_specs=[pl.BlockSpec(memory_space=pl.ANY),
                      pl.BlockSpec(memory_space=pl.ANY),
                      pl.BlockSpec((1, D, ff), lambda i, be, bo, nu: (be[i], 0, 0)),
                      pl.BlockSpec((1, D, ff), lambda i, be, bo, nu: (be[i], 0, 0)),
                      pl.BlockSpec((1, ff, D), lambda i, be, bo, nu: (be[i], 0, 0))],
            out_specs=pl.BlockSpec((rows, D), lambda i, be, bo, nu: (i, 0)),
            scratch_shapes=[pltpu.SMEM((2 * (rows + TOK_ALIGN),), I32),
                            pltpu.VMEM((2, rows, D), F32),
                            pltpu.VMEM((D, ff), BF16),
                            pltpu.VMEM((D, ff), BF16),
                            pltpu.VMEM((ff, D), BF16),
                            pltpu.SemaphoreType.DMA((2,)),
                            pltpu.SemaphoreType.DMA((2,))]),
        out_shape=jax.ShapeDtypeStruct((n_blocks * rows, D), F32),
        compiler_params=_cparams(("arbitrary",)),
        name="routed_experts",
    )(blk_e, blk_off, n_used, sorted_tok, h2d, w_eg, w_eu, w_ed)


def _combine_kernel(dest_hbm, ys_hbm, h_ref, gate_ref, wsg_ref, wsu_ref, wsd_ref, g_ref, b_ref, o_ref,
                    dest_s, buf, dest_sem, row_sem, *, tm, alpha):
    i = pl.program_id(0)
    n = pl.num_programs(0)

    def dest_copy(blk, slot):
        return pltpu.make_async_copy(dest_hbm.at[blk], dest_s.at[pl.ds(slot * (TOP_K * tm), TOP_K * tm)],
                                     dest_sem.at[slot])

    def issue_rows(slot, unrolled=False):
        base = slot * (TOP_K * tm)

        def group(g):
            for rr in range(SUBLANES):
                r = g * SUBLANES + rr
                for k in range(TOP_K):
                    d = dest_s[base + k * tm + r]
                    pltpu.make_async_copy(ys_hbm.at[pl.ds(d, 1)], buf.at[slot, k, pl.ds(r, 1)],
                                          row_sem.at[slot]).start()
        if unrolled:
            for g in range(tm // SUBLANES):
                group(g)
        else:
            lax.fori_loop(0, tm // SUBLANES, lambda g, c: (group(g), 0)[1], 0)

    def rows_done(slot):
        for k in range(TOP_K):
            pltpu.make_async_copy(ys_hbm.at[pl.ds(0, tm)], buf.at[slot, k], row_sem.at[slot]).wait()

    @pl.when(i == 0)
    def _():
        dest_copy(0, 0).start()
        dest_copy(0, 0).wait()
        issue_rows(0)
        dest_copy(jnp.minimum(1, n - 1), 1).start()

    slot = i % 2
    nxt = 1 - slot
    dest_copy(jnp.minimum(i + 1, n - 1), nxt).wait()
    rows_done(slot)
    issue_rows(nxt, unrolled=True)
    h = h_ref[...]
    hb = h.astype(BF16)
    shared = _dot((jax.nn.silu(_dot(hb, wsg_ref[...])) * _dot(hb, wsu_ref[...])).astype(BF16), wsd_ref[...])
    total = alpha * h + shared
    for k in range(TOP_K):
        total = total + gate_ref[:, k:k + 1] * buf[slot, k]
    o_ref[...] = _layer_norm(total, g_ref[...], b_ref[...])
    dest_copy(jnp.minimum(i + 2, n - 1), slot).start()

    @pl.when(i == n - 1)
    def _():
        rows_done(nxt)
        dest_copy(n - 1, slot).wait()


def _combine_shared_norm(h2d, ys, dest, gate, w_sg, w_su, w_sd, ln_g, ln_b, alpha, tm):
    T, D = h2d.shape
    nt = T // tm
    ws = [w.astype(BF16) for w in (w_sg, w_su, w_sd)]
    vec = [v.reshape(1, D).astype(F32) for v in (ln_g, ln_b)]
    whole = lambda a: pl.BlockSpec(a.shape, lambda i: (0, 0))
    return pl.pallas_call(
        functools.partial(_combine_kernel, tm=tm, alpha=alpha),
        grid=(nt,),
        in_specs=[pl.BlockSpec(memory_space=pl.ANY), pl.BlockSpec(memory_space=pl.ANY),
                  pl.BlockSpec((tm, D), lambda i: (i, 0)),
                  pl.BlockSpec((tm, TOP_K), lambda i: (i, 0))] + [whole(w) for w in ws] + [whole(v) for v in vec],
        out_specs=pl.BlockSpec((tm, D), lambda i: (i, 0)),
        out_shape=jax.ShapeDtypeStruct((T, D), F32),
        scratch_shapes=[pltpu.SMEM((2 * TOP_K * tm,), I32),
                        pltpu.VMEM((2, TOP_K, tm, D), F32),
                        pltpu.SemaphoreType.DMA((2,)),
                        pltpu.SemaphoreType.DMA((2,))],
        compiler_params=_cparams(("arbitrary",)),
        name="combine_shared_norm",
    )(_token_tiles(dest, tm), ys, h2d, gate.T, *ws, *vec)


PROJ_TM = 512
DA_TQ, DA_TK = 1024, 512
DSA_TQ = 256
MEM_TQ = 512
MERGE_TM = 256
ROUTER_TM = 512
DEST_TM = 2048
COMBINE_TM = 128
MOE_ROWS = 256


def _dispatch_plan(counts, n_tokens, rows):
    counts = counts.reshape(N_EXPERTS)
    padded = (counts + rows - 1) // rows * rows
    pend = jnp.cumsum(padded)
    pstart = pend - padded
    cstart = jnp.cumsum(counts) - counts
    n_blocks = (n_tokens * TOP_K + N_EXPERTS * (rows - 1) + rows - 1) // rows
    n_used = pend[-1] // rows
    blk = jnp.minimum(jnp.arange(n_blocks, dtype=I32), n_used - 1)
    hot = (blk[:, None] * rows >= pend[None, :]).astype(I32)
    blk_e = jnp.sum(hot, 1)
    first = (jnp.arange(N_EXPERTS, dtype=I32)[None, :] == blk_e[:, None]).astype(I32)
    blk_off = jnp.sum(first * (cstart - pstart)[None, :], 1) + blk * rows
    return pstart.astype(I32), blk_e, blk_off.astype(I32), n_used.astype(I32).reshape(1), n_blocks


def _moe(h1, p, l, alpha):
    T = h1.shape[0]
    eidx, gate, rank, counts = _route(h1, p['w_router'][l], p['router_bias'][l], tm=ROUTER_TM)
    pstart, blk_e, blk_off, n_used, n_blocks = _dispatch_plan(counts, T, MOE_ROWS)
    dest = _destinations(pstart, eidx, rank, tm=DEST_TM)
    tok = jnp.broadcast_to(jnp.arange(T, dtype=I32)[None, :], (TOP_K, T))
    _, sorted_tok = lax.sort((dest.reshape(-1), tok.reshape(-1)), num_keys=1)
    sorted_tok = jnp.concatenate([sorted_tok, jnp.zeros((MOE_ROWS + 2 * TOK_ALIGN,), I32)])
    ys = _routed_experts(h1, sorted_tok, blk_e, blk_off, n_used, n_blocks,
                         p['w_eg'][l], p['w_eu'][l], p['w_ed'][l], MOE_ROWS)
    return _combine_shared_norm(h1, ys, dest, gate, p['w_sg'][l], p['w_su'][l], p['w_sd'][l],
                                p['ln2_g'][l], p['ln2_b'][l], alpha, tm=COMBINE_TM)


def _layer(h2d, mem2d, p, l, depth, batch, seq, mem_len):
    alpha = (2 * depth) ** 0.25
    lam_init = 0.8 - 0.6 * math.exp(-0.3 * l)
    w_in = p['w_in'][l]
    daq, dak, dsq, ixq, kv, vk, ikk, dav, memq, ixw = _in_projection(h2d, w_in, seq, tm=min(PROJ_TM, seq))
    lam = (jnp.exp(jnp.sum(p['lq1'][l].astype(F32) * p['lk1'][l].astype(F32)))
           - jnp.exp(jnp.sum(p['lq2'][l].astype(F32) * p['lk2'][l].astype(F32))) + lam_init)
    ya = _diff_attention(daq, dak, dav, lam, p['subln_g'][l], batch, seq, lam_init, tq=min(DA_TQ, seq), tk=min(DA_TK, seq))
    yb = _sparse_attention(ixq, ixw, dsq, ikk, kv, vk, batch, seq, tq=min(DSA_TQ, seq))
    mkv = _matmul(mem2d, p['w_mem_kv'][l].astype(BF16), tm=mem_len, out_dtype=BF16)
    yc = _memory_attention(memq, mkv, batch, seq, mem_len, tq=min(MEM_TQ, seq))
    h1 = _merge_project_norm(h2d, ya, yb, yc, w_in[:, 3272:], p['w_ba'][l], p['w_bb'][l], p['w_bc'][l],
                             p['w_o'][l], p['ln1_g'][l], p['ln1_b'][l], alpha, tm=MERGE_TM)
    return _moe(h1, p, l, alpha)


def kernel(x, mem, w_in, da_lambda_q1, da_lambda_k1, da_lambda_q2, da_lambda_k2, da_subln_g, w_mem_kv, w_branch_a, w_branch_b, w_branch_c, w_out, ln1_g, ln1_b, w_router, router_bias, w_exp_gate, w_exp_up, w_exp_down, w_sh_gate, w_sh_up, w_sh_down, ln2_g, ln2_b):
    batch, seq, d = x.shape
    mem_len = mem.shape[1]
    depth = w_in.shape[0]
    p = dict(w_in=w_in, lq1=da_lambda_q1, lk1=da_lambda_k1, lq2=da_lambda_q2, lk2=da_lambda_k2,
             subln_g=da_subln_g, w_mem_kv=w_mem_kv, w_ba=w_branch_a, w_bb=w_branch_b, w_bc=w_branch_c,
             w_o=w_out, ln1_g=ln1_g, ln1_b=ln1_b, w_router=w_router, router_bias=router_bias,
             w_eg=w_exp_gate, w_eu=w_exp_up, w_ed=w_exp_down, w_sg=w_sh_gate, w_su=w_sh_up,
             w_sd=w_sh_down, ln2_g=ln2_g, ln2_b=ln2_b)
    h = x.reshape(batch * seq, d)
    mem2d = mem.reshape(batch * mem_len, d)
    for l in range(depth):
        h = _layer(h, mem2d, p, l, depth, batch, seq, mem_len)
    return h.reshape(batch, seq, d)
```

```python
import functools
import math

import jax
import jax.numpy as jnp
import numpy as np
from jax import lax
from jax.experimental import pallas as pl
from jax.experimental.pallas import tpu as pltpu

F32 = jnp.float32
BF16 = jnp.bfloat16
I32 = jnp.int32

LANES = 128
SUBLANES = 8
ROPE_THETA = 10000.0
LN_EPS = 1e-5

DA_HEADS = 4
DA_HEAD_DIM = 64
DSA_HEADS = 8
DSA_HEAD_DIM = 64
IDX_HEADS = 8
IDX_ROWS = 16
DSA_TOPK_MAX = 256
MEM_HEADS = 4
MEM_HEAD_DIM = 128

N_EXPERTS = 256
TOP_K = 8
N_GROUPS = 8
TOPK_GROUPS = 4
GROUP_SIZE = N_EXPERTS // N_GROUPS
ROUTED_SCALE = 2.5

NEG_BIG = -1e30
INT_MIN = -(2 ** 31)

VMEM_LIMIT = 56 * 1024 * 1024


def _cparams(sem):
    return pltpu.CompilerParams(dimension_semantics=sem, vmem_limit_bytes=VMEM_LIMIT)


def _dot(a, b):
    return jnp.dot(a, b, preferred_element_type=F32)


def _dot_nt(a, b):
    return lax.dot_general(a, b, (((1,), (1,)), ((), ())), preferred_element_type=F32)


def _tree(op, xs):
    xs = list(xs)
    while len(xs) > 1:
        xs = [op(xs[k], xs[k + 1]) if k + 1 < len(xs) else xs[k] for k in range(0, len(xs), 2)]
    return xs[0]


def _layer_norm(x, g, b):
    mu = jnp.mean(x, -1, keepdims=True)
    xc = x - mu
    var = jnp.mean(xc * xc, -1, keepdims=True)
    return xc * lax.rsqrt(var + LN_EPS) * g + b


def _inproj_kernel(x_ref, wr_ref, wn_ref, wixt_ref, tab_ref,
                   daq_ref, dak_ref, dsq_ref, ixq_ref, kv_ref, vk_ref, ikk_ref,
                   dav_ref, memq_ref, ixwt_ref):
    xb = x_ref[...].astype(BF16)

    def rope(t, kind):
        c = tab_ref[3 * kind]
        sa = tab_ref[3 * kind + 1]
        sb = tab_ref[3 * kind + 2]
        return t * c + pltpu.roll(t, 96, 1) * sa + pltpu.roll(t, 32, 1) * sb

    groups = ((daq_ref, 0.125), (dak_ref, 1.0), (dsq_ref, 0.125), (ixq_ref, 1.0))
    for gi, (ref, scale) in enumerate(groups):
        y = _dot(xb, wr_ref[:, gi * 512:(gi + 1) * 512])
        for c in range(4):
            r = rope(y[:, c * LANES:(c + 1) * LANES], 0)
            if scale != 1.0:
                r = r * scale
            ref[:, c * LANES:(c + 1) * LANES] = r.astype(BF16)
    y = _dot(xb, wr_ref[:, 2048:2432])
    kv_ref[...] = rope(y[:, 0:128], 1).astype(BF16)
    vk_ref[...] = rope(y[:, 128:256], 2).astype(BF16)
    ikk_ref[...] = rope(y[:, 256:384], 0).astype(BF16)
    y = _dot(xb, wn_ref[...])
    dav_ref[...] = y[:, 0:512].astype(BF16)
    memq_ref[...] = y[:, 512:1024].astype(BF16)
    ixwt_ref[...] = _dot_nt(wixt_ref[...], xb)


def _rope_tables(seq):
    dim = 64
    inv = 1.0 / (ROPE_THETA ** (jnp.arange(0, dim, 2, dtype=F32) / dim))
    ang = jnp.arange(seq, dtype=F32)[:, None] * inv[None, :]
    ang = jnp.concatenate([ang, ang], -1)
    cos, sin = jnp.cos(ang), jnp.sin(ang)
    local = jnp.arange(dim)[None, :]
    sa = jnp.where(local < dim // 2, -sin, 0.0)
    sb = jnp.where(local >= dim // 2, sin, 0.0)
    one, zero = jnp.ones_like(cos), jnp.zeros_like(cos)
    cat = lambda a, b: jnp.concatenate([a, b], -1)
    return jnp.stack([cat(cos, cos), cat(sa, sa), cat(sb, sb),
                      cat(cos, one), cat(sa, zero), cat(sb, zero),
                      cat(one, cos), cat(zero, sa), cat(zero, sb)], 0)


def _in_projection(x2d, w_in, seq, tm):
    T, D = x2d.shape
    o = np.cumsum([0, 512, 512, 512, 512, 64, 64, 512, 64, 8, 512]).tolist()
    col = lambda i: w_in[:, o[i]:o[i + 1]]
    da_q, da_k, da_v, ds_q, ds_k, ds_v, ix_q, ix_k, ix_w, mem_q = [col(i) for i in range(10)]
    ds_q = ds_q.reshape(D, DSA_HEADS // 2, 2, DSA_HEAD_DIM)[:, :, ::-1, :].reshape(D, 512)
    w_rope = jnp.concatenate([da_q, da_k, ds_q, ix_q, ds_k, ds_v, ds_v, ds_k, ix_k, ix_k], 1).astype(BF16)
    w_plain = jnp.concatenate([da_v, mem_q], 1).astype(BF16)
    w_ixt = jnp.concatenate([ix_w.T, jnp.zeros((IDX_ROWS - IDX_HEADS, D), w_in.dtype)], 0).astype(BF16)
    tabs = _rope_tables(seq)
    nseq = seq // tm
    bf = lambda n: jax.ShapeDtypeStruct((T, n), BF16)
    row = lambda n: pl.BlockSpec((tm, n), lambda i: (i, 0))
    return pl.pallas_call(
        _inproj_kernel,
        grid=(T // tm,),
        in_specs=[row(D),
                  pl.BlockSpec(w_rope.shape, lambda i: (0, 0)),
                  pl.BlockSpec(w_plain.shape, lambda i: (0, 0)),
                  pl.BlockSpec(w_ixt.shape, lambda i: (0, 0)),
                  pl.BlockSpec((9, tm, LANES), lambda i: (0, i % nseq, 0))],
        out_specs=[row(512), row(512), row(512), row(512), row(128), row(128), row(128),
                   row(512), row(512), pl.BlockSpec((IDX_ROWS, tm), lambda i: (0, i))],
        out_shape=[bf(512), bf(512), bf(512), bf(512), bf(128), bf(128), bf(128),
                   bf(512), bf(512), jax.ShapeDtypeStruct((IDX_ROWS, T), F32)],
        compiler_params=_cparams(("parallel",)),
        name="in_projection",
    )(x2d, w_rope, w_plain, w_ixt, tabs)


def _diffattn_kernel(lam_ref, q_ref, k_ref, v_ref, g_ref, o_ref, m_s, l_s, a_s, *, tq, tk, lam_init):
    i = pl.program_id(2)
    lam = lam_ref[0]
    q = q_ref[...]
    lane = lax.broadcasted_iota(I32, q.shape, 1)
    zero = jnp.zeros_like(q)
    qs = (jnp.where(lane < DA_HEAD_DIM, q, zero), jnp.where(lane >= DA_HEAD_DIM, q, zero))
    m_s[...] = jnp.full(m_s.shape, -jnp.inf, F32)
    l_s[...] = jnp.zeros(l_s.shape, F32)
    a_s[...] = jnp.zeros(a_s.shape, F32)

    def step(j, masked, r0=0):
        off = pl.multiple_of(j * tk, tk)
        kj = k_ref[pl.ds(off, tk), :]
        vj = v_ref[pl.ds(off, tk), :]
        for mp in range(2):
            s = _dot_nt(qs[mp][r0:], kj)
            if masked:
                row = lax.broadcasted_iota(I32, (tq - r0, tk), 0) + r0
                col = lax.broadcasted_iota(I32, (tq - r0, tk), 1)
                s = jnp.where(j * tk + col <= i * tq + row, s, -jnp.inf)
            tiles = [s[:, c * LANES:(c + 1) * LANES] for c in range(tk // LANES)]
            m_old = m_s[mp, r0:]
            m_blk = jnp.max(_tree(jnp.maximum, tiles), -1, keepdims=True)
            m_new = jnp.maximum(m_old, jnp.broadcast_to(m_blk, (tq - r0, LANES)))
            ps = [jnp.exp(t - m_new) for t in tiles]
            al = jnp.exp(m_old - m_new)
            l_s[mp, r0:] = al * l_s[mp, r0:] + _tree(jnp.add, ps)
            a_s[mp, r0:] = al * a_s[mp, r0:] + _dot(jnp.concatenate(ps, 1).astype(BF16), vj)
            m_s[mp, r0:] = m_new
        return 0

    per_q = tq // tk

    def fully_visible(t, _):
        for jj in range(per_q):
            step(t * per_q + jj, False)
        return 0

    lax.fori_loop(0, i, fully_visible, 0)
    for jj in range(per_q):
        step(i * per_q + jj, True, jj * tk)
    o1 = a_s[0] / jnp.sum(l_s[0], -1, keepdims=True)
    o2 = a_s[1] / jnp.sum(l_s[1], -1, keepdims=True)
    o = o1 - lam * o2
    y = o * lax.rsqrt(jnp.mean(o * o, -1, keepdims=True) + LN_EPS) * g_ref[...] * (1.0 - lam_init)
    o_ref[...] = y.astype(BF16)


def _diff_attention(daq, dak, dav, lam, subln_g, batch, seq, lam_init, tq, tk):
    T = batch * seq
    nq = seq // tq
    state = pltpu.VMEM((2, tq, LANES), F32)
    return pl.pallas_call(
        functools.partial(_diffattn_kernel, tq=tq, tk=tk, lam_init=lam_init),
        grid=(batch, DA_HEADS, nq),
        in_specs=[pl.BlockSpec(memory_space=pltpu.SMEM),
                  pl.BlockSpec((tq, LANES), lambda b, h, i: (b * nq + i, h)),
                  pl.BlockSpec((seq, LANES), lambda b, h, i: (b, h)),
                  pl.BlockSpec((seq, LANES), lambda b, h, i: (b, h)),
                  pl.BlockSpec((1, LANES), lambda b, h, i: (0, 0))],
        out_specs=pl.BlockSpec((tq, LANES), lambda b, h, i: (b * nq + i, h)),
        out_shape=jax.ShapeDtypeStruct((T, DA_HEADS * LANES), BF16),
        scratch_shapes=[state, state, state],
        compiler_params=_cparams(("parallel", "parallel", "arbitrary")),
        name="diff_attention",
    )(lam.reshape(1), daq, dak, dav, subln_g.reshape(1, LANES).astype(F32))


def _dsa_kernel(ixq_ref, ixwt_ref, dsq_ref, ikk_ref, kv_ref, vk_ref, o_ref,
                qi_s, qd_s, sc_s, g_s, mx_s, l_s, acc_s, *, tq, topk):
    i = pl.program_id(1)
    nblk = i + 1
    tk = tq
    nslab = tk // 8
    lane = lax.broadcasted_iota(I32, (tq, LANES), 1)
    lower = lane < DSA_HEAD_DIM
    npair = DSA_HEADS // 2
    for h in range(IDX_HEADS):
        sl = slice((h // 2) * LANES, (h // 2 + 1) * LANES)
        keep = lower if h % 2 == 0 else jnp.logical_not(lower)
        blk = ixq_ref[:, sl]
        qi_s[h * tq:(h + 1) * tq, :] = jnp.where(keep, blk, jnp.zeros_like(blk))
        blk = dsq_ref[:, sl]
        qd_s[h % 2, (h // 2) * tq:(h // 2 + 1) * tq, :] = jnp.where(keep, blk, jnp.zeros_like(blk))
    krow = lax.broadcasted_iota(I32, (tk, tq), 0)
    qcol = lax.broadcasted_iota(I32, (tk, tq), 1)
    sub8 = lax.broadcasted_iota(I32, (8, tq), 0)

    def score_chunk(j, _):
        off = pl.multiple_of(j * tk, tk)
        ik = ikk_ref[pl.ds(off, tk), :]
        logits = _dot_nt(ik, qi_s[...])
        sc = jnp.zeros((tk, tq), F32)
        for h in range(IDX_HEADS):
            sc = sc + ixwt_ref[h:h + 1, :] * jnp.maximum(logits[:, h * tq:(h + 1) * tq], 0.0)
        sc = jnp.where(j * tk + krow <= i * tq + qcol, sc, -jnp.inf)
        sc_s[j] = sc
        g_s[j] = sc.astype(BF16)
        return 0

    def over_chunks(body, width=4):
        def group(jg, _):
            for u in range(width):
                body(width * jg + u)
            return 0
        lax.fori_loop(0, nblk // width, group, 0)
        lax.fori_loop(nblk // width * width, nblk, lambda j, c: (body(j), 0)[1], 0)

    over_chunks(lambda j: score_chunk(j, 0))

    n_acc = 4

    def count(pred):
        def body(j, accs):
            accs = list(accs)
            for r in range(nslab):
                hit = pred(sc_s[j, r * 8:(r + 1) * 8, :], j * tk + r * 8 + sub8)
                accs[r % n_acc] = accs[r % n_acc] + hit.astype(F32)
            return tuple(accs)
        accs = lax.fori_loop(0, nblk, body, tuple(jnp.zeros((8, tq), F32) for _ in range(n_acc)))
        return jnp.sum(_tree(jnp.add, accs), 0, keepdims=True)

    def rows8(v):
        return jnp.broadcast_to(v, (8, tq))

    def as_score(key):
        return pltpu.bitcast(key ^ ((key >> 31) & 0x7FFFFFFF), F32)

    def count_coarse(cand):
        cb = jnp.broadcast_to(cand, (16, tq))
        one, nil = jnp.ones((16, tq), BF16), jnp.zeros((16, tq), BF16)

        def body(j, accs):
            accs = list(accs)
            for r in range(tk // 16):
                hit = g_s[j, r * 16:(r + 1) * 16, :] >= cb
                accs[r % n_acc] = accs[r % n_acc] + jnp.where(hit, one, nil)
            return tuple(accs)
        accs = lax.fori_loop(0, nblk, body, tuple(nil for _ in range(n_acc)))
        return jnp.sum(_tree(jnp.add, [a.astype(F32) for a in accs]), 0, keepdims=True)

    def as_coarse(key16):
        bits16 = key16 ^ ((key16 >> 15) & 0x7FFF)
        return pltpu.bitcast(bits16 << 16, F32).astype(BF16)

    kf = float(topk)
    zero_i = jnp.zeros((1, tq), I32)

    def coarse_bit(b, hi):
        cand = jnp.where(b == 0, zero_i, hi | (jnp.int32(1) << (15 - b)))
        c = count_coarse(as_coarse(cand))
        return jnp.where(c >= kf, cand, hi)

    hi16 = lax.fori_loop(0, 16, coarse_bit, jnp.full((1, tq), -(2 ** 15), I32))
    found = hi16 != -(2 ** 15)
    key_t = jnp.where(found, (hi16 << 16) + jnp.where(hi16 < 0, 0xFFFF, 0), 0)
    span = (1 << 16) + (1 << 15) + 2

    def bisect(_, bounds):
        lo, width = bounds
        step = (width + 1) >> 1
        cand = lo + step
        cb = rows8(as_score(cand))
        ok = count(lambda s, kpos: s >= cb) >= kf
        return jnp.where(ok, cand, lo), jnp.where(ok, width - step, step - 1)

    lo, _ = lax.fori_loop(0, 17, bisect, (key_t - ((1 << 15) + 1), jnp.full((1, tq), span, I32)))
    tau = jnp.where(found, lo, INT_MIN)
    tau_f = jnp.where(tau == INT_MIN, -jnp.inf, as_score(tau))
    tau8 = rows8(tau_f)
    c_ge = count(lambda s, kpos: s >= tau8)

    def tie_cut():
        need = kf - count(lambda s, kpos: s > tau8)

        def cut_bit(b, cut):
            cand = cut | (jnp.int32(1) << (12 - b))
            cb = rows8(cand)
            c = count(lambda s, kpos: (s == tau8) & (kpos < cb))
            return jnp.where(c <= need, cand, cut)

        return lax.fori_loop(0, 13, cut_bit, zero_i)

    cut = lax.cond(jnp.max(c_ge) > kf, tie_cut, lambda: jnp.full((1, tq), 2 ** 13, I32))

    mx_s[...] = jnp.full(mx_s.shape, NEG_BIG, F32)
    l_s[...] = jnp.zeros(l_s.shape, F32)
    acc_s[...] = jnp.zeros(acc_s.shape, F32)
    nlt = tk // LANES

    def lane_tiles(a):
        return [a[:, c * LANES:(c + 1) * LANES] for c in range(nlt)]

    def bcast(v):
        return jnp.broadcast_to(v, (v.shape[0], LANES))

    def row_max(j, _):
        off = pl.multiple_of(j * tk, tk)
        sc = sc_s[j]
        kpos = j * tk + krow
        sel = (sc > tau_f) | ((sc == tau_f) & (kpos < cut))
        sel = sel & (kpos <= i * tq + qcol)
        bias = jnp.where(sel, 0.0, NEG_BIG).T
        sc_s[j] = bias
        kvj = kv_ref[pl.ds(off, tk), :]
        vkj = vk_ref[pl.ds(off, tk), :]
        for par, kk in enumerate((kvj, vkj)):
            s = (_dot_nt(qd_s[par], kk).reshape(npair, tq, tk) + bias[None]).reshape(npair * tq, tk)
            mx_s[par] = functools.reduce(jnp.maximum, lane_tiles(s), mx_s[par])
        return 0

    over_chunks(lambda j: row_max(j, 0))
    for par in range(2):
        mx_s[par] = bcast(jnp.max(mx_s[par], -1, keepdims=True))

    def attend(j, _):
        off = pl.multiple_of(j * tk, tk)
        bias = sc_s[j]
        kvj = kv_ref[pl.ds(off, tk), :]
        vkj = vk_ref[pl.ds(off, tk), :]
        for par, kk in enumerate((kvj, vkj)):
            s = (_dot_nt(qd_s[par], kk).reshape(npair, tq, tk) + bias[None]).reshape(npair * tq, tk)
            m = mx_s[par]
            ps = [jnp.exp(t - m) for t in lane_tiles(s)]
            l_s[par] = functools.reduce(jnp.add, ps, l_s[par])
            acc_s[par] = acc_s[par] + _dot(jnp.concatenate(ps, 1).astype(BF16), kk)
        return 0

    over_chunks(lambda j: attend(j, 0))

    def out(h):
        rows = slice((h // 2) * tq, (h // 2 + 1) * tq)
        return acc_s[h % 2, rows, :] / jnp.sum(l_s[h % 2, rows, :], -1, keepdims=True)

    for pr in range(DSA_HEADS // 2):
        o_ref[:, pr * LANES:(pr + 1) * LANES] = jnp.where(lower, out(2 * pr + 1), out(2 * pr)).astype(BF16)


def _sparse_attention(ixq, ixwt, dsq, ikk, kv, vk, batch, seq, tq):
    T = batch * seq
    nq = seq // tq
    topk = min(DSA_TOPK_MAX, seq // 4)
    tile = lambda n: pl.BlockSpec((tq, n), lambda b, i: (b * nq + i, 0))
    full = pl.BlockSpec((seq, LANES), lambda b, i: (b, 0))
    return pl.pallas_call(
        functools.partial(_dsa_kernel, tq=tq, topk=topk),
        grid=(batch, nq),
        in_specs=[tile(512), pl.BlockSpec((ixwt.shape[0], tq), lambda b, i: (0, b * nq + i)), tile(512),
                  full, full, full],
        out_specs=tile(512),
        out_shape=jax.ShapeDtypeStruct((T, 512), BF16),
        scratch_shapes=[pltpu.VMEM((IDX_HEADS * tq, LANES), BF16),
                        pltpu.VMEM((2, DSA_HEADS // 2 * tq, LANES), BF16),
                        pltpu.VMEM((nq, tq, tq), F32),
                        pltpu.VMEM((nq, tq, tq), BF16),
                        pltpu.VMEM((2, DSA_HEADS // 2 * tq, LANES), F32),
                        pltpu.VMEM((2, DSA_HEADS // 2 * tq, LANES), F32),
                        pltpu.VMEM((2, DSA_HEADS // 2 * tq, LANES), F32)],
        compiler_params=_cparams(("parallel", "arbitrary")),
        name="sparse_attention",
    )(ixq, ixwt, dsq, ikk, kv, vk)


def _matmul_kernel(x_ref, w_ref, o_ref):
    o_ref[...] = _dot(x_ref[...].astype(BF16), w_ref[...]).astype(o_ref.dtype)


def _matmul(x, w, tm, out_dtype):
    M, K = x.shape
    N = w.shape[1]
    return pl.pallas_call(
        _matmul_kernel,
        grid=(M // tm,),
        in_specs=[pl.BlockSpec((tm, K), lambda i: (i, 0)), pl.BlockSpec((K, N), lambda i: (0, 0))],
        out_specs=pl.BlockSpec((tm, N), lambda i: (i, 0)),
        out_shape=jax.ShapeDtypeStruct((M, N), out_dtype),
        compiler_params=_cparams(("parallel",)),
        name="matmul",
    )(x, w)


def _memattn_kernel(q_ref, kv_ref, o_ref):
    scale = MEM_HEAD_DIM ** -0.5
    width = MEM_HEADS * MEM_HEAD_DIM
    for h in range(MEM_HEADS):
        sl = slice(h * MEM_HEAD_DIM, (h + 1) * MEM_HEAD_DIM)
        k = kv_ref[:, sl]
        v = kv_ref[:, width + h * MEM_HEAD_DIM:width + (h + 1) * MEM_HEAD_DIM]
        s = _dot_nt(q_ref[:, sl], k) * scale
        p = jnp.exp(s - jnp.max(s, -1, keepdims=True))
        o = _dot(p.astype(BF16), v) / jnp.sum(p, -1, keepdims=True)
        o_ref[:, sl] = o.astype(BF16)


def _memory_attention(memq, mkv, batch, seq, mem_len, tq):
    T = batch * seq
    nq = seq // tq
    return pl.pallas_call(
        _memattn_kernel,
        grid=(batch, nq),
        in_specs=[pl.BlockSpec((tq, 512), lambda b, i: (b * nq + i, 0)),
                  pl.BlockSpec((mem_len, 1024), lambda b, i: (b, 0))],
        out_specs=pl.BlockSpec((tq, 512), lambda b, i: (b * nq + i, 0)),
        out_shape=jax.ShapeDtypeStruct((T, 512), BF16),
        compiler_params=_cparams(("parallel", "parallel")),
        name="memory_attention",
    )(memq, mkv)


def _merge_kernel(x_ref, ya_ref, yb_ref, yc_ref, wg_ref, wa_ref, wb_ref, wc_ref, wo_ref,
                  g_ref, b_ref, h_ref, *, alpha):
    x = x_ref[...]
    xb = x.astype(BF16)
    d = x.shape[1]
    merged = None
    for br, (y_ref, w_ref) in enumerate(((ya_ref, wa_ref), (yb_ref, wb_ref), (yc_ref, wc_ref))):
        gate = jax.nn.sigmoid(_dot(xb, wg_ref[:, br * d:(br + 1) * d]))
        term = gate * _dot(y_ref[...], w_ref[...])
        merged = term if merged is None else merged + term
    mix = _dot(merged.astype(BF16), wo_ref[...])
    h_ref[...] = _layer_norm(alpha * x + mix, g_ref[...], b_ref[...])


def _merge_project_norm(x2d, ya, yb, yc, w_gates, w_ba, w_bb, w_bc, w_o, ln_g, ln_b, alpha, tm):
    T, D = x2d.shape
    row = lambda n: pl.BlockSpec((tm, n), lambda i: (i, 0))
    whole = lambda a: pl.BlockSpec(a.shape, lambda i: (0, 0))
    ws = [w.astype(BF16) for w in (w_gates, w_ba, w_bb, w_bc, w_o)]
    vec = [v.reshape(1, D).astype(F32) for v in (ln_g, ln_b)]
    return pl.pallas_call(
        functools.partial(_merge_kernel, alpha=alpha),
        grid=(T // tm,),
        in_specs=[row(D), row(512), row(512), row(512)] + [whole(w) for w in ws] + [whole(v) for v in vec],
        out_specs=row(D),
        out_shape=jax.ShapeDtypeStruct((T, D), F32),
        compiler_params=_cparams(("parallel",)),
        name="merge_project_norm",
    )(x2d, ya, yb, yc, *ws, *vec)


def _router_kernel(h_ref, wr_ref, bias_ref, eidx_ref, gate_ref, rank_ref, cnt_ref, carry_s, *, tm):
    @pl.when(pl.program_id(0) == 0)
    def _():
        carry_s[...] = jnp.zeros(carry_s.shape, F32)

    neg = -jnp.inf
    logits = _dot_nt(wr_ref[...], h_ref[...].astype(BF16))
    scores = jax.nn.sigmoid(logits)
    biased = scores + bias_ref[...]
    erow = lax.broadcasted_iota(I32, (N_EXPERTS, tm), 0)
    big = jnp.int32(2 ** 30)

    def top1(vals, ids):
        mx = jnp.max(vals, 0, keepdims=True)
        am = jnp.min(jnp.where(vals == mx, ids, big), 0, keepdims=True)
        return mx, am

    gs = []
    ids = lax.broadcasted_iota(I32, (GROUP_SIZE, tm), 0)
    for g in range(N_GROUPS):
        v = biased[g * GROUP_SIZE:(g + 1) * GROUP_SIZE]
        m1, a1 = top1(v, ids)
        m2, _ = top1(jnp.where(ids == a1, neg, v), ids)
        gs.append(m1 + m2)
    gscore = jnp.concatenate(gs, 0)
    grow = lax.broadcasted_iota(I32, (N_GROUPS, tm), 0)
    gsel = jnp.zeros((N_GROUPS, tm), F32)
    for _ in range(TOPK_GROUPS):
        _, ga = top1(gscore, grow)
        hit = grow == ga
        gsel = jnp.where(hit, 1.0, gsel)
        gscore = jnp.where(hit, neg, gscore)
    masked = jnp.concatenate(
        [jnp.where(gsel[g:g + 1] > 0.0, biased[g * GROUP_SIZE:(g + 1) * GROUP_SIZE], neg)
         for g in range(N_GROUPS)], 0)

    eids, ws = [], []
    hot = jnp.zeros((N_EXPERTS, tm), F32)
    for _ in range(TOP_K):
        _, ea = top1(masked, erow)
        hit = erow == ea
        eids.append(ea)
        ws.append(jnp.sum(jnp.where(hit, scores, 0.0), 0, keepdims=True))
        hot = jnp.where(hit, 1.0, hot)
        masked = jnp.where(hit, neg, masked)
    w = jnp.concatenate(ws, 0)
    eidx_ref[...] = jnp.concatenate(eids, 0)
    gate_ref[...] = w / jnp.sum(w, 0, keepdims=True) * ROUTED_SCALE

    r_i =lax.broadcasted_iota(I32, (tm, tm), 0)
    c_i = lax.broadcasted_iota(I32, (tm, tm), 1)
    before = jnp.where(r_i < c_i, 1.0, 0.0).astype(BF16)
    pos = carry_s[...] + _dot(hot.astype(BF16), before)
    rank_ref[...] = jnp.concatenate(
        [jnp.sum(jnp.where(erow == e, pos, 0.0), 0, keepdims=True) for e in eids], 0).astype(I32)
    carry_s[...] = carry_s[...] + jnp.sum(hot, 1, keepdims=True)
    cnt_ref[...] = carry_s[...].astype(I32)


def _route(h2d, w_router, router_bias, tm):
    T, D = h2d.shape
    out = lambda dt: jax.ShapeDtypeStruct((TOP_K, T), dt)
    blk = pl.BlockSpec((TOP_K, tm), lambda i: (0, i))
    return pl.pallas_call(
        functools.partial(_router_kernel, tm=tm),
        grid=(T // tm,),
        in_specs=[pl.BlockSpec((tm, D), lambda i: (i, 0)),
                  pl.BlockSpec((N_EXPERTS, D), lambda i: (0, 0)),
                  pl.BlockSpec((N_EXPERTS, 1), lambda i: (0, 0))],
        out_specs=[blk, blk, blk, pl.BlockSpec((N_EXPERTS, 1), lambda i: (0, 0))],
        out_shape=[out(I32), out(F32), out(I32), jax.ShapeDtypeStruct((N_EXPERTS, 1), I32)],
        scratch_shapes=[pltpu.VMEM((N_EXPERTS, 1), F32)],
        compiler_params=_cparams(("arbitrary",)),
        name="router",
    )(h2d, w_router.T.astype(BF16), router_bias.reshape(N_EXPERTS, 1).astype(F32))


def _dest_kernel(pstart_ref, eidx_ref, rank_ref, dest_ref):
    eidx = eidx_ref[...]

    def body(e, acc):
        return acc + jnp.where(eidx == e, pstart_ref[e], 0)

    dest_ref[...] = lax.fori_loop(0, N_EXPERTS, body, rank_ref[...])


def _destinations(pstart, eidx, rank, tm):
    T = eidx.shape[1]
    blk = lambda: pl.BlockSpec((TOP_K, tm), lambda i, ps: (0, i))
    return pl.pallas_call(
        _dest_kernel,
        grid_spec=pltpu.PrefetchScalarGridSpec(
            num_scalar_prefetch=1, grid=(T // tm,), in_specs=[blk(), blk()], out_specs=blk()),
        out_shape=jax.ShapeDtypeStruct((TOP_K, T), I32),
        compiler_params=_cparams(("parallel",)),
        name="destinations",
    )(pstart, eidx, rank)


def _token_tiles(a, tm):
    k, T = a.shape
    return a.reshape(k, T // tm, tm).transpose(1, 0, 2).reshape(T // tm, k * tm)


TOK_ALIGN = 128
DEPTH = 3


def _experts_kernel(blk_e_ref, blk_off_ref, used_ref, tok_hbm, h_hbm, wg_ref, wu_ref, wd_ref, y_ref,
                    tok_s, xbuf, wgb, wub, wdb, tok_sem, row_sem, *, rows):
    i = pl.program_id(0)
    used = used_ref[0]
    win = rows + TOK_ALIGN

    def window(blk):
        return pl.multiple_of(blk_off_ref[blk] // TOK_ALIGN * TOK_ALIGN, TOK_ALIGN)

    def tok_copy(blk, slot):
        return pltpu.make_async_copy(tok_hbm.at[pl.ds(window(blk), win)], tok_s.at[pl.ds(slot * win, win)],
                                     tok_sem.at[slot])

    def issue_rows(blk, slot, unrolled):
        base = slot * win + blk_off_ref[blk] - window(blk)

        def group(g):
            for rr in range(SUBLANES):
                r = g * SUBLANES + rr
                t = tok_s[base + r]
                pltpu.make_async_copy(h_hbm.at[pl.ds(t, 1)], xbuf.at[slot, pl.ds(r, 1)], row_sem.at[slot]).start()
        if unrolled:
            for g in range(rows // SUBLANES):
                group(g)
        else:
            lax.fori_loop(0, rows // SUBLANES, lambda g, c: (group(g), 0)[1], 0)

    def rows_done(slot):
        pltpu.make_async_copy(h_hbm.at[pl.ds(0, rows)], xbuf.at[slot], row_sem.at[slot]).wait()

    def blk_at(j):
        return jnp.minimum(j, used - 1)

    @pl.when(i == 0)
    def _():
        for j in range(DEPTH - 1):
            tok_copy(blk_at(j), j).start()
        for j in range(DEPTH - 1):
            tok_copy(blk_at(j), j).wait()
            issue_rows(blk_at(j), j, False)
        tok_copy(blk_at(DEPTH - 1), DEPTH - 1).start()

    e = blk_e_ref[i]
    prev = blk_e_ref[jnp.maximum(i - 1, 0)]

    @pl.when((i == 0) | (e != prev))
    def _():
        wgb[...] = wg_ref[0].astype(BF16)
        wub[...] = wu_ref[0].astype(BF16)
        wdb[...] = wd_ref[0].astype(BF16)

    slot = i % DEPTH
    far = (i + DEPTH - 1) % DEPTH

    @pl.when(i < used)
    def _():
        ahead = blk_at(i + DEPTH - 1)
        tok_copy(ahead, far).wait()
        rows_done(slot)
        issue_rows(ahead, far, True)
        x = xbuf[slot].astype(BF16)
        hidden = jax.nn.silu(_dot(x, wgb[...])) * _dot(x, wub[...])
        y_ref[...] = _dot(hidden.astype(BF16), wdb[...])
        tok_copy(blk_at(i + DEPTH), slot).start()

    @pl.when(i == used - 1)
    def _():
        for j in range(1, DEPTH):
            rows_done((i + j) % DEPTH)
        tok_copy(used - 1, slot).wait()

    @pl.when(i >= used)
    def _():
        y_ref[...] = jnp.zeros(y_ref.shape, F32)


def _routed_experts(h2d, sorted_tok, blk_e, blk_off, n_used, n_blocks, w_eg, w_eu, w_ed, rows):
    T, D = h2d.shape
    ff = w_eg.shape[-1]
    return pl.pallas_call(
        functools.partial(_experts_kernel, rows=rows),
        grid_spec=pltpu.PrefetchScalarGridSpec(
            num_scalar_prefetch=3,
            grid=(n_blocks,),
            in_specs=[pl.BlockSpec(memory_space=pl.ANY),
                      pl.BlockSpec(memory_space=pl.ANY),
                      pl.BlockSpec((1, D, ff), lambda i, be, bo, nu: (be[i], 0, 0)),
                      pl.BlockSpec((1, D, ff), lambda i, be, bo, nu: (be[i], 0, 0)),
                      pl.BlockSpec((1, ff, D), lambda i, be, bo, nu: (be[i], 0, 0))],
            out_specs=pl.BlockSpec((rows, D), lambda i, be, bo, nu: (i, 0)),
            scratch_shapes=[pltpu.SMEM((DEPTH * (rows + TOK_ALIGN),), I32),
                            pltpu.VMEM((DEPTH, rows, D), F32),
                            pltpu.VMEM((D, ff), BF16),
                            pltpu.VMEM((D, ff), BF16),
                            pltpu.VMEM((ff, D), BF16),
                            pltpu.SemaphoreType.DMA((DEPTH,)),
                            pltpu.SemaphoreType.DMA((DEPTH,))]),
        out_shape=jax.ShapeDtypeStruct((n_blocks * rows, D), F32),
        compiler_params=_cparams(("arbitrary",)),
        name="routed_experts",
    )(blk_e, blk_off, n_used, sorted_tok, h2d, w_eg, w_eu, w_ed)


def _combine_kernel(dest_hbm, ys_hbm, h_ref, gate_ref, wsg_ref, wsu_ref, wsd_ref, g_ref, b_ref, o_ref,
                    dest_s, buf, dest_sem, row_sem, *, tm, alpha):
    i = pl.program_id(0)
    n = pl.num_programs(0)

    def dest_copy(blk, slot):
        return pltpu.make_async_copy(dest_hbm.at[blk], dest_s.at[pl.ds(slot * (TOP_K * tm), TOP_K * tm)],
                                     dest_sem.at[slot])

    def issue_rows(slot, unrolled=False):
        base = slot * (TOP_K * tm)

        def group(g):
            for rr in range(SUBLANES):
                r = g * SUBLANES + rr
                for k in range(TOP_K):
                    d = dest_s[base + k * tm + r]
                    pltpu.make_async_copy(ys_hbm.at[pl.ds(d, 1)], buf.at[slot, k, pl.ds(r, 1)],
                                          row_sem.at[slot]).start()
        if unrolled:
            for g in range(tm // SUBLANES):
                group(g)
        else:
            lax.fori_loop(0, tm // SUBLANES, lambda g, c: (group(g), 0)[1], 0)

    def rows_done(slot):
        for k in range(TOP_K):
            pltpu.make_async_copy(ys_hbm.at[pl.ds(0, tm)], buf.at[slot, k], row_sem.at[slot]).wait()

    def tile_at(j):
        return jnp.minimum(j, n - 1)

    @pl.when(i == 0)
    def _():
        for j in range(DEPTH - 1):
            dest_copy(tile_at(j), j).start()
        for j in range(DEPTH - 1):
            dest_copy(tile_at(j), j).wait()
            issue_rows(j)
        dest_copy(tile_at(DEPTH - 1), DEPTH - 1).start()

    slot = i % DEPTH
    far = (i + DEPTH - 1) % DEPTH
    dest_copy(tile_at(i + DEPTH - 1), far).wait()
    rows_done(slot)
    issue_rows(far, unrolled=True)
    h = h_ref[...]
    hb = h.astype(BF16)
    shared = _dot((jax.nn.silu(_dot(hb, wsg_ref[...])) * _dot(hb, wsu_ref[...])).astype(BF16), wsd_ref[...])
    total = alpha * h + shared
    for k in range(TOP_K):
        total = total + gate_ref[:, k:k + 1] * buf[slot, k]
    o_ref[...] = _layer_norm(total, g_ref[...], b_ref[...])
    dest_copy(tile_at(i + DEPTH), slot).start()

    @pl.when(i == n - 1)
    def _():
        for j in range(1, DEPTH):
            rows_done((i + j) % DEPTH)
        dest_copy(n - 1, slot).wait()


def _combine_shared_norm(h2d, ys, dest, gate, w_sg, w_su, w_sd, ln_g, ln_b, alpha, tm):
    T, D = h2d.shape
    nt = T // tm
    ws = [w.astype(BF16) for w in (w_sg, w_su, w_sd)]
    vec = [v.reshape(1, D).astype(F32) for v in (ln_g, ln_b)]
    whole = lambda a: pl.BlockSpec(a.shape, lambda i: (0, 0))
    return pl.pallas_call(
        functools.partial(_combine_kernel, tm=tm, alpha=alpha),
        grid=(nt,),
        in_specs=[pl.BlockSpec(memory_space=pl.ANY), pl.BlockSpec(memory_space=pl.ANY),
                  pl.BlockSpec((tm, D), lambda i: (i, 0)),
                  pl.BlockSpec((tm, TOP_K), lambda i: (i, 0))] + [whole(w) for w in ws] + [whole(v) for v in vec],
        out_specs=pl.BlockSpec((tm, D), lambda i: (i, 0)),
        out_shape=jax.ShapeDtypeStruct((T, D), F32),
        scratch_shapes=[pltpu.SMEM((DEPTH * TOP_K * tm,), I32),
                        pltpu.VMEM((DEPTH, TOP_K, tm, D), F32),
                        pltpu.SemaphoreType.DMA((DEPTH,)),
                        pltpu.SemaphoreType.DMA((DEPTH,))],
        compiler_params=_cparams(("arbitrary",)),
        name="combine_shared_norm",
    )(_token_tiles(dest, tm), ys, h2d, gate.T, *ws, *vec)


PROJ_TM = 512
DA_TQ, DA_TK = 1024, 512
DSA_TQ = 256
MEM_TQ = 512
MERGE_TM = 256
ROUTER_TM = 512
DEST_TM = 2048
COMBINE_TM = 128
MOE_ROWS = 256


def _dispatch_plan(counts, n_tokens, rows):
    counts = counts.reshape(N_EXPERTS)
    padded = (counts + rows - 1) // rows * rows
    pend = jnp.cumsum(padded)
    pstart = pend - padded
    cstart = jnp.cumsum(counts) - counts
    n_blocks = (n_tokens * TOP_K + N_EXPERTS * (rows - 1) + rows - 1) // rows
    n_used = pend[-1] // rows
    blk = jnp.minimum(jnp.arange(n_blocks, dtype=I32), n_used - 1)
    hot = (blk[:, None] * rows >= pend[None, :]).astype(I32)
    blk_e = jnp.sum(hot, 1)
    first = (jnp.arange(N_EXPERTS, dtype=I32)[None, :] == blk_e[:, None]).astype(I32)
    blk_off = jnp.sum(first * (cstart - pstart)[None, :], 1) + blk * rows
    return pstart.astype(I32), blk_e, blk_off.astype(I32), n_used.astype(I32).reshape(1), n_blocks


def _moe(h1, p, l, alpha):
    T = h1.shape[0]
    eidx, gate, rank, counts = _route(h1, p['w_router'][l], p['router_bias'][l], tm=ROUTER_TM)
    pstart, blk_e, blk_off, n_used, n_blocks = _dispatch_plan(counts, T, MOE_ROWS)
    dest = _destinations(pstart, eidx, rank, tm=DEST_TM)
    tok = jnp.broadcast_to(jnp.arange(T, dtype=I32)[None, :], (TOP_K, T))
    _, sorted_tok = lax.sort((dest.reshape(-1), tok.reshape(-1)), num_keys=1)
    sorted_tok = jnp.concatenate([sorted_tok, jnp.zeros((MOE_ROWS + 2 * TOK_ALIGN,), I32)])
    ys = _routed_experts(h1, sorted_tok, blk_e, blk_off, n_used, n_blocks,
                         p['w_eg'][l], p['w_eu'][l], p['w_ed'][l], MOE_ROWS)
    return _combine_shared_norm(h1, ys, dest, gate, p['w_sg'][l], p['w_su'][l], p['w_sd'][l],
                                p['ln2_g'][l], p['ln2_b'][l], alpha, tm=COMBINE_TM)


def _layer(h2d, mem2d, p, l, depth, batch, seq, mem_len):
    alpha = (2 * depth) ** 0.25
    lam_init = 0.8 - 0.6 * math.exp(-0.3 * l)
    w_in = p['w_in'][l]
    daq, dak, dsq, ixq, kv, vk, ikk, dav, memq, ixw = _in_projection(h2d, w_in, seq, tm=min(PROJ_TM, seq))
    lam = (jnp.exp(jnp.sum(p['lq1'][l].astype(F32) * p['lk1'][l].astype(F32)))
           - jnp.exp(jnp.sum(p['lq2'][l].astype(F32) * p['lk2'][l].astype(F32))) + lam_init)
    ya = _diff_attention(daq, dak, dav, lam, p['subln_g'][l], batch, seq, lam_init, tq=min(DA_TQ, seq), tk=min(DA_TK, seq))
    yb = _sparse_attention(ixq, ixw, dsq, ikk, kv, vk, batch, seq, tq=min(DSA_TQ, seq))
    mkv = _matmul(mem2d, p['w_mem_kv'][l].astype(BF16), tm=mem_len, out_dtype=BF16)
    yc = _memory_attention(memq, mkv, batch, seq, mem_len, tq=min(MEM_TQ, seq))
    h1 = _merge_project_norm(h2d, ya, yb, yc, w_in[:, 3272:], p['w_ba'][l], p['w_bb'][l], p['w_bc'][l],
                             p['w_o'][l], p['ln1_g'][l], p['ln1_b'][l], alpha, tm=MERGE_TM)
    return _moe(h1, p, l, alpha)


def kernel(x, mem, w_in, da_lambda_q1, da_lambda_k1, da_lambda_q2, da_lambda_k2, da_subln_g, w_mem_kv, w_branch_a, w_branch_b, w_branch_c, w_out, ln1_g, ln1_b, w_router, router_bias, w_exp_gate, w_exp_up, w_exp_down, w_sh_gate, w_sh_up, w_sh_down, ln2_g, ln2_b):
    batch, seq, d = x.shape
    mem_len = mem.shape[1]
    depth = w_in.shape[0]
    p = dict(w_in=w_in, lq1=da_lambda_q1, lk1=da_lambda_k1, lq2=da_lambda_q2, lk2=da_lambda_k2,
             subln_g=da_subln_g, w_mem_kv=w_mem_kv, w_ba=w_branch_a, w_bb=w_branch_b, w_bc=w_branch_c,
             w_o=w_out, ln1_g=ln1_g, ln1_b=ln1_b, w_router=w_router, router_bias=router_bias,
             w_eg=w_exp_gate, w_eu=w_exp_up, w_ed=w_exp_down, w_sg=w_sh_gate, w_su=w_sh_up,
             w_sd=w_sh_down, ln2_g=ln2_g, ln2_b=ln2_b)
    h = x.reshape(batch * seq, d)
    mem2d = mem.reshape(batch * mem_len, d)
    for l in range(depth):
        h = _layer(h, mem2d, p, l, depth, batch, seq, mem_len)
    return h.reshape(batch, seq, d)
```

```python
import functools
import math

import jax
import jax.numpy as jnp
import numpy as np
from jax import lax
from jax.experimental import pallas as pl
from jax.experimental.pallas import tpu as pltpu

F32 = jnp.float32
BF16 = jnp.bfloat16
I32 = jnp.int32

LANES = 128
SUBLANES = 8
ROPE_THETA = 10000.0
LN_EPS = 1e-5

DA_HEADS = 4
DA_HEAD_DIM = 64
DSA_HEADS = 8
DSA_HEAD_DIM = 64
IDX_HEADS = 8
IDX_ROWS = 16
DSA_TOPK_MAX = 256
MEM_HEADS = 4
MEM_HEAD_DIM = 128

N_EXPERTS = 256
TOP_K = 8
N_GROUPS = 8
TOPK_GROUPS = 4
GROUP_SIZE = N_EXPERTS // N_GROUPS
ROUTED_SCALE = 2.5

NEG_BIG = -1e30
INT_MIN = -(2 ** 31)

VMEM_LIMIT = 56 * 1024 * 1024


def _cparams(sem):
    return pltpu.CompilerParams(dimension_semantics=sem, vmem_limit_bytes=VMEM_LIMIT)


def _dot(a, b):
    return jnp.dot(a, b, preferred_element_type=F32)


def _dot_nt(a, b):
    return lax.dot_general(a, b, (((1,), (1,)), ((), ())), preferred_element_type=F32)


def _tree(op, xs):
    xs = list(xs)
    while len(xs) > 1:
        xs = [op(xs[k], xs[k + 1]) if k + 1 < len(xs) else xs[k] for k in range(0, len(xs), 2)]
    return xs[0]


def _layer_norm(x, g, b):
    mu = jnp.mean(x, -1, keepdims=True)
    xc = x - mu
    var = jnp.mean(xc * xc, -1, keepdims=True)
    return xc * lax.rsqrt(var + LN_EPS) * g + b


def _inproj_kernel(x_ref, wr_ref, wn_ref, wixt_ref, tab_ref,
                   daq_ref, dak_ref, dsq_ref, ixq_ref, kv_ref, vk_ref, ikk_ref,
                   dav_ref, memq_ref, ixwt_ref):
    xb = x_ref[...].astype(BF16)

    def rope(t, kind):
        c = tab_ref[3 * kind]
        sa = tab_ref[3 * kind + 1]
        sb = tab_ref[3 * kind + 2]
        return t * c + pltpu.roll(t, 96, 1) * sa + pltpu.roll(t, 32, 1) * sb

    groups = ((daq_ref, 0.125), (dak_ref, 1.0), (dsq_ref, 0.125), (ixq_ref, 1.0))
    for gi, (ref, scale) in enumerate(groups):
        y = _dot(xb, wr_ref[:, gi * 512:(gi + 1) * 512])
        for c in range(4):
            r = rope(y[:, c * LANES:(c + 1) * LANES], 0)
            if scale != 1.0:
                r = r * scale
            ref[:, c * LANES:(c + 1) * LANES] = r.astype(BF16)
    y = _dot(xb, wr_ref[:, 2048:2432])
    kv_ref[...] = rope(y[:, 0:128], 1).astype(BF16)
    vk_ref[...] = rope(y[:, 128:256], 2).astype(BF16)
    ikk_ref[...] = rope(y[:, 256:384], 0).astype(BF16)
    y = _dot(xb, wn_ref[...])
    dav_ref[...] = y[:, 0:512].astype(BF16)
    memq_ref[...] = y[:, 512:1024].astype(BF16)
    ixwt_ref[...] = _dot_nt(wixt_ref[...], xb)


def _rope_tables(seq):
    dim = 64
    inv = 1.0 / (ROPE_THETA ** (jnp.arange(0, dim, 2, dtype=F32) / dim))
    ang = jnp.arange(seq, dtype=F32)[:, None] * inv[None, :]
    ang = jnp.concatenate([ang, ang], -1)
    cos, sin = jnp.cos(ang), jnp.sin(ang)
    local = jnp.arange(dim)[None, :]
    sa = jnp.where(local < dim // 2, -sin, 0.0)
    sb = jnp.where(local >= dim // 2, sin, 0.0)
    one, zero = jnp.ones_like(cos), jnp.zeros_like(cos)
    cat = lambda a, b: jnp.concatenate([a, b], -1)
    return jnp.stack([cat(cos, cos), cat(sa, sa), cat(sb, sb),
                      cat(cos, one), cat(sa, zero), cat(sb, zero),
                      cat(one, cos), cat(zero, sa), cat(zero, sb)], 0)


def _in_projection(x2d, w_in, seq, tm):
    T, D = x2d.shape
    o = np.cumsum([0, 512, 512, 512, 512, 64, 64, 512, 64, 8, 512]).tolist()
    col = lambda i: w_in[:, o[i]:o[i + 1]]
    da_q, da_k, da_v, ds_q, ds_k, ds_v, ix_q, ix_k, ix_w, mem_q = [col(i) for i in range(10)]
    ds_q = ds_q.reshape(D, DSA_HEADS // 2, 2, DSA_HEAD_DIM)[:, :, ::-1, :].reshape(D, 512)
    w_rope = jnp.concatenate([da_q, da_k, ds_q, ix_q, ds_k, ds_v, ds_v, ds_k, ix_k, ix_k], 1).astype(BF16)
    w_plain = jnp.concatenate([da_v, mem_q], 1).astype(BF16)
    w_ixt = jnp.concatenate([ix_w.T, jnp.zeros((IDX_ROWS - IDX_HEADS, D), w_in.dtype)], 0).astype(BF16)
    tabs = _rope_tables(seq)
    nseq = seq // tm
    bf = lambda n: jax.ShapeDtypeStruct((T, n), BF16)
    row = lambda n: pl.BlockSpec((tm, n), lambda i: (i, 0))
    return pl.pallas_call(
        _inproj_kernel,
        grid=(T // tm,),
        in_specs=[row(D),
                  pl.BlockSpec(w_rope.shape, lambda i: (0, 0)),
                  pl.BlockSpec(w_plain.shape, lambda i: (0, 0)),
                  pl.BlockSpec(w_ixt.shape, lambda i: (0, 0)),
                  pl.BlockSpec((9, tm, LANES), lambda i: (0, i % nseq, 0))],
        out_specs=[row(512), row(512), row(512), row(512), row(128), row(128), row(128),
                   row(512), row(512), pl.BlockSpec((IDX_ROWS, tm), lambda i: (0, i))],
        out_shape=[bf(512), bf(512), bf(512), bf(512), bf(128), bf(128), bf(128),
                   bf(512), bf(512), jax.ShapeDtypeStruct((IDX_ROWS, T), F32)],
        compiler_params=_cparams(("parallel",)),
        name="in_projection",
    )(x2d, w_rope, w_plain, w_ixt, tabs)


def _diffattn_kernel(lam_ref, q_ref, k_ref, v_ref, g_ref, o_ref, m_s, l_s, a_s, *, tq, tk, lam_init):
    i = pl.program_id(2)
    lam = lam_ref[0]
    q = q_ref[...]
    lane = lax.broadcasted_iota(I32, q.shape, 1)
    zero = jnp.zeros_like(q)
    qs = (jnp.where(lane < DA_HEAD_DIM, q, zero), jnp.where(lane >= DA_HEAD_DIM, q, zero))
    m_s[...] = jnp.full(m_s.shape, -jnp.inf, F32)
    l_s[...] = jnp.zeros(l_s.shape, F32)
    a_s[...] = jnp.zeros(a_s.shape, F32)

    def step(j, masked, r0=0):
        off = pl.multiple_of(j * tk, tk)
        kj = k_ref[pl.ds(off, tk), :]
        vj = v_ref[pl.ds(off, tk), :]
        for mp in range(2):
            s = _dot_nt(qs[mp][r0:], kj)
            if masked:
                row = lax.broadcasted_iota(I32, (tq - r0, tk), 0) + r0
                col = lax.broadcasted_iota(I32, (tq - r0, tk), 1)
                s = jnp.where(j * tk + col <= i * tq + row, s, -jnp.inf)
            tiles = [s[:, c * LANES:(c + 1) * LANES] for c in range(tk // LANES)]
            m_old = m_s[mp, r0:]
            m_blk = jnp.max(_tree(jnp.maximum, tiles), -1, keepdims=True)
            m_new = jnp.maximum(m_old, jnp.broadcast_to(m_blk, (tq - r0, LANES)))
            ps = [jnp.exp(t - m_new) for t in tiles]
            al = jnp.exp(m_old - m_new)
            l_s[mp, r0:] = al * l_s[mp, r0:] + _tree(jnp.add, ps)
            a_s[mp, r0:] = al * a_s[mp, r0:] + _dot(jnp.concatenate(ps, 1).astype(BF16), vj)
            m_s[mp, r0:] = m_new
        return 0

    per_q = tq // tk

    def fully_visible(t, _):
        for jj in range(per_q):
            step(t * per_q + jj, False)
        return 0

    lax.fori_loop(0, i, fully_visible, 0)
    for jj in range(per_q):
        step(i * per_q + jj, True, jj * tk)
    o1 = a_s[0] / jnp.sum(l_s[0], -1, keepdims=True)
    o2 = a_s[1] / jnp.sum(l_s[1], -1, keepdims=True)
    o = o1 - lam * o2
    y = o * lax.rsqrt(jnp.mean(o * o, -1, keepdims=True) + LN_EPS) * g_ref[...] * (1.0 - lam_init)
    o_ref[...] = y.astype(BF16)


def _diff_attention(daq, dak, dav, lam, subln_g, batch, seq, lam_init, tq, tk):
    T = batch * seq
    nq = seq // tq
    state = pltpu.VMEM((2, tq, LANES), F32)
    return pl.pallas_call(
        functools.partial(_diffattn_kernel, tq=tq, tk=tk, lam_init=lam_init),
        grid=(batch, DA_HEADS, nq),
        in_specs=[pl.BlockSpec(memory_space=pltpu.SMEM),
                  pl.BlockSpec((tq, LANES), lambda b, h, i: (b * nq + i, h)),
                  pl.BlockSpec((seq, LANES), lambda b, h, i: (b, h)),
                  pl.BlockSpec((seq, LANES), lambda b, h, i: (b, h)),
                  pl.BlockSpec((1, LANES), lambda b, h, i: (0, 0))],
        out_specs=pl.BlockSpec((tq, LANES), lambda b, h, i: (b * nq + i, h)),
        out_shape=jax.ShapeDtypeStruct((T, DA_HEADS * LANES), BF16),
        scratch_shapes=[state, state, state],
        compiler_params=_cparams(("parallel", "parallel", "arbitrary")),
        name="diff_attention",
    )(lam.reshape(1), daq, dak, dav, subln_g.reshape(1, LANES).astype(F32))


def _dsa_kernel(ixq_ref, ixwt_ref, dsq_ref, ikk_ref, kv_ref, vk_ref, o_ref,
                qi_s, qd_s, sc_s, g_s, mx_s, l_s, acc_s, *, tq, topk):
    i = pl.program_id(1)
    nblk = i + 1
    tk = tq
    nslab = tk // 8
    lane = lax.broadcasted_iota(I32, (tq, LANES), 1)
    lower = lane < DSA_HEAD_DIM
    npair = DSA_HEADS // 2
    for h in range(IDX_HEADS):
        sl = slice((h // 2) * LANES, (h // 2 + 1) * LANES)
        keep = lower if h % 2 == 0 else jnp.logical_not(lower)
        blk = ixq_ref[:, sl]
        qi_s[h * tq:(h + 1) * tq, :] = jnp.where(keep, blk, jnp.zeros_like(blk))
        blk = dsq_ref[:, sl]
        qd_s[h % 2, (h // 2) * tq:(h // 2 + 1) * tq, :] = jnp.where(keep, blk, jnp.zeros_like(blk))
    krow = lax.broadcasted_iota(I32, (tk, tq), 0)
    qcol = lax.broadcasted_iota(I32, (tk, tq), 1)
    sub8 = lax.broadcasted_iota(I32, (8, tq), 0)

    def score_chunk(j, _):
        off = pl.multiple_of(j * tk, tk)
        ik = ikk_ref[pl.ds(off, tk), :]
        logits = _dot_nt(ik, qi_s[...])
        sc = jnp.zeros((tk, tq), F32)
        for h in range(IDX_HEADS):
            sc = sc + ixwt_ref[h:h + 1, :] * jnp.maximum(logits[:, h * tq:(h + 1) * tq], 0.0)
        sc = jnp.where(j * tk + krow <= i * tq + qcol, sc, -jnp.inf)
        sc_s[j] = sc
        g_s[j] = sc.astype(BF16)
        return 0

    def over_chunks(body, width=4):
        def group(jg, _):
            for u in range(width):
                body(width * jg + u)
            return 0
        lax.fori_loop(0, nblk // width, group, 0)
        lax.fori_loop(nblk // width * width, nblk, lambda j, c: (body(j), 0)[1], 0)

    over_chunks(lambda j: score_chunk(j, 0))

    n_acc = 4

    def count(pred):
        def body(j, accs):
            accs = list(accs)
            for r in range(nslab):
                hit = pred(sc_s[j, r * 8:(r + 1) * 8, :], j * tk + r * 8 + sub8)
                accs[r % n_acc] = accs[r % n_acc] + hit.astype(F32)
            return tuple(accs)
        accs = lax.fori_loop(0, nblk, body, tuple(jnp.zeros((8, tq), F32) for _ in range(n_acc)))
        return jnp.sum(_tree(jnp.add, accs), 0, keepdims=True)

    def rows8(v):
        return jnp.broadcast_to(v, (8, tq))

    def as_score(key):
        return pltpu.bitcast(key ^ ((key >> 31) & 0x7FFFFFFF), F32)

    def count_coarse(cand):
        cb = jnp.broadcast_to(cand, (16, tq))
        one, nil = jnp.ones((16, tq), BF16), jnp.zeros((16, tq), BF16)

        def body(j, accs):
            accs = list(accs)
            for r in range(tk // 16):
                hit = g_s[j, r * 16:(r + 1) * 16, :] >= cb
                accs[r % n_acc] = accs[r % n_acc] + jnp.where(hit, one, nil)
            return tuple(accs)
        accs = lax.fori_loop(0, nblk, body, tuple(nil for _ in range(n_acc)))
        return jnp.sum(_tree(jnp.add, [a.astype(F32) for a in accs]), 0, keepdims=True)

    def as_coarse(key16):
        bits16 = key16 ^ ((key16 >> 15) & 0x7FFF)
        return pltpu.bitcast(bits16 << 16, F32).astype(BF16)

    kf = float(topk)
    zero_i = jnp.zeros((1, tq), I32)

    def coarse_bit(b, hi):
        cand = jnp.where(b == 0, zero_i, hi | (jnp.int32(1) << (15 - b)))
        c = count_coarse(as_coarse(cand))
        return jnp.where(c >= kf, cand, hi)

    hi16 = lax.fori_loop(0, 16, coarse_bit, jnp.full((1, tq), -(2 ** 15), I32))
    found = hi16 != -(2 ** 15)
    key_t = jnp.where(found, (hi16 << 16) + jnp.where(hi16 < 0, 0xFFFF, 0), 0)
    span = (1 << 16) + (1 << 15) + 2

    def bisect(_, bounds):
        lo, width = bounds
        step = (width + 1) >> 1
        cand = lo + step
        cb = rows8(as_score(cand))
        ok = count(lambda s, kpos: s >= cb) >= kf
        return jnp.where(ok, cand, lo), jnp.where(ok, width - step, step - 1)

    lo, _ = lax.fori_loop(0, 17, bisect, (key_t - ((1 << 15) + 1), jnp.full((1, tq), span, I32)))
    tau = jnp.where(found, lo, INT_MIN)
    tau_f = jnp.where(tau == INT_MIN, -jnp.inf, as_score(tau))
    tau8 = rows8(tau_f)
    c_ge = count(lambda s, kpos: s >= tau8)

    def tie_cut():
        need = kf - count(lambda s, kpos: s > tau8)

        def cut_bit(b, cut):
            cand = cut | (jnp.int32(1) << (12 - b))
            cb = rows8(cand)
            c = count(lambda s, kpos: (s == tau8) & (kpos < cb))
            return jnp.where(c <= need, cand, cut)

        return lax.fori_loop(0, 13, cut_bit, zero_i)

    cut = lax.cond(jnp.max(c_ge) > kf, tie_cut, lambda: jnp.full((1, tq), 2 ** 13, I32))

    mx_s[...] = jnp.full(mx_s.shape, NEG_BIG, F32)
    l_s[...] = jnp.zeros(l_s.shape, F32)
    acc_s[...] = jnp.zeros(acc_s.shape, F32)
    nlt = tk // LANES

    def lane_tiles(a):
        return [a[:, c * LANES:(c + 1) * LANES] for c in range(nlt)]

    def bcast(v):
        return jnp.broadcast_to(v, (v.shape[0], LANES))

    def row_max(j, _):
        off = pl.multiple_of(j * tk, tk)
        sc = sc_s[j]
        kpos = j * tk + krow
        sel = (sc > tau_f) | ((sc == tau_f) & (kpos < cut))
        sel = sel & (kpos <= i * tq + qcol)
        bias = jnp.where(sel, 0.0, NEG_BIG).T
        sc_s[j] = bias
        kvj = kv_ref[pl.ds(off, tk), :]
        vkj = vk_ref[pl.ds(off, tk), :]
        for par, kk in enumerate((kvj, vkj)):
            s = (_dot_nt(qd_s[par], kk).reshape(npair, tq, tk) + bias[None]).reshape(npair * tq, tk)
            mx_s[par] = functools.reduce(jnp.maximum, lane_tiles(s), mx_s[par])
        return 0

    over_chunks(lambda j: row_max(j, 0))
    for par in range(2):
        mx_s[par] = bcast(jnp.max(mx_s[par], -1, keepdims=True))

    def attend(j, _):
        off = pl.multiple_of(j * tk, tk)
        bias = sc_s[j]
        kvj = kv_ref[pl.ds(off, tk), :]
        vkj = vk_ref[pl.ds(off, tk), :]
        for par, kk in enumerate((kvj, vkj)):
            s = (_dot_nt(qd_s[par], kk).reshape(npair, tq, tk) + bias[None]).reshape(npair * tq, tk)
            m = mx_s[par]
            ps = [jnp.exp(t - m) for t in lane_tiles(s)]
            l_s[par] = functools.reduce(jnp.add, ps, l_s[par])
            acc_s[par] = acc_s[par] + _dot(jnp.concatenate(ps, 1).astype(BF16), kk)
        return 0

    over_chunks(lambda j: attend(j, 0))

    def out(h):
        rows = slice((h // 2) * tq, (h // 2 + 1) * tq)
        return acc_s[h % 2, rows, :] / jnp.sum(l_s[h % 2, rows, :], -1, keepdims=True)

    for pr in range(DSA_HEADS // 2):
        o_ref[:, pr * LANES:(pr + 1) * LANES] = jnp.where(lower, out(2 * pr + 1), out(2 * pr)).astype(BF16)


def _sparse_attention(ixq, ixwt, dsq, ikk, kv, vk, batch, seq, tq):
    T = batch * seq
    nq = seq // tq
    topk = min(DSA_TOPK_MAX, seq // 4)
    tile = lambda n: pl.BlockSpec((tq, n), lambda b, i: (b * nq + i, 0))
    full = pl.BlockSpec((seq, LANES), lambda b, i: (b, 0))
    return pl.pallas_call(
        functools.partial(_dsa_kernel, tq=tq, topk=topk),
        grid=(batch, nq),
        in_specs=[tile(512), pl.BlockSpec((ixwt.shape[0], tq), lambda b, i: (0, b * nq + i)), tile(512),
                  full, full, full],
        out_specs=tile(512),
        out_shape=jax.ShapeDtypeStruct((T, 512), BF16),
        scratch_shapes=[pltpu.VMEM((IDX_HEADS * tq, LANES), BF16),
                        pltpu.VMEM((2, DSA_HEADS // 2 * tq, LANES), BF16),
                        pltpu.VMEM((nq, tq, tq), F32),
                        pltpu.VMEM((nq, tq, tq), BF16),
                        pltpu.VMEM((2, DSA_HEADS // 2 * tq, LANES), F32),
                        pltpu.VMEM((2, DSA_HEADS // 2 * tq, LANES), F32),
                        pltpu.VMEM((2, DSA_HEADS // 2 * tq, LANES), F32)],
        compiler_params=_cparams(("parallel", "arbitrary")),
        name="sparse_attention",
    )(ixq, ixwt, dsq, ikk, kv, vk)


def _matmul_kernel(x_ref, w_ref, o_ref):
    o_ref[...] = _dot(x_ref[...].astype(BF16), w_ref[...]).astype(o_ref.dtype)


def _matmul(x, w, tm, out_dtype):
    M, K = x.shape
    N = w.shape[1]
    return pl.pallas_call(
        _matmul_kernel,
        grid=(M // tm,),
        in_specs=[pl.BlockSpec((tm, K), lambda i: (i, 0)), pl.BlockSpec((K, N), lambda i: (0, 0))],
        out_specs=pl.BlockSpec((tm, N), lambda i: (i, 0)),
        out_shape=jax.ShapeDtypeStruct((M, N), out_dtype),
        compiler_params=_cparams(("parallel",)),
        name="matmul",
    )(x, w)


def _memattn_kernel(q_ref, kv_ref, o_ref):
    scale = MEM_HEAD_DIM ** -0.5
    width = MEM_HEADS * MEM_HEAD_DIM
    for h in range(MEM_HEADS):
        sl = slice(h * MEM_HEAD_DIM, (h + 1) * MEM_HEAD_DIM)
        k = kv_ref[:, sl]
        v = kv_ref[:, width + h * MEM_HEAD_DIM:width + (h + 1) * MEM_HEAD_DIM]
        s = _dot_nt(q_ref[:, sl], k) * scale
        p = jnp.exp(s - jnp.max(s, -1, keepdims=True))
        o = _dot(p.astype(BF16), v) / jnp.sum(p, -1, keepdims=True)
        o_ref[:, sl] = o.astype(BF16)


def _memory_attention(memq, mkv, batch, seq, mem_len, tq):
    T = batch * seq
    nq = seq // tq
    return pl.pallas_call(
        _memattn_kernel,
        grid=(batch, nq),
        in_specs=[pl.BlockSpec((tq, 512), lambda b, i: (b * nq + i, 0)),
                  pl.BlockSpec((mem_len, 1024), lambda b, i: (b, 0))],
        out_specs=pl.BlockSpec((tq, 512), lambda b, i: (b * nq + i, 0)),
        out_shape=jax.ShapeDtypeStruct((T, 512), BF16),
        compiler_params=_cparams(("parallel", "parallel")),
        name="memory_attention",
    )(memq, mkv)


def _merge_kernel(x_ref, ya_ref, yb_ref, yc_ref, wg_ref, wa_ref, wb_ref, wc_ref, wo_ref,
                  g_ref, b_ref, h_ref, *, alpha):
    x = x_ref[...]
    xb = x.astype(BF16)
    d = x.shape[1]
    merged = None
    for br, (y_ref, w_ref) in enumerate(((ya_ref, wa_ref), (yb_ref, wb_ref), (yc_ref, wc_ref))):
        gate = jax.nn.sigmoid(_dot(xb, wg_ref[:, br * d:(br + 1) * d]))
        term = gate * _dot(y_ref[...], w_ref[...])
        merged = term if merged is None else merged + term
    mix = _dot(merged.astype(BF16), wo_ref[...])
    h_ref[...] = _layer_norm(alpha * x + mix, g_ref[...], b_ref[...])


def _merge_project_norm(x2d, ya, yb, yc, w_gates, w_ba, w_bb, w_bc, w_o, ln_g, ln_b, alpha, tm):
    T, D = x2d.shape
    row = lambda n: pl.BlockSpec((tm, n), lambda i: (i, 0))
    whole = lambda a: pl.BlockSpec(a.shape, lambda i: (0, 0))
    ws = [w.astype(BF16) for w in (w_gates, w_ba, w_bb, w_bc, w_o)]
    vec = [v.reshape(1, D).astype(F32) for v in (ln_g, ln_b)]
    return pl.pallas_call(
        functools.partial(_merge_kernel, alpha=alpha),
        grid=(T // tm,),
        in_specs=[row(D), row(512), row(512), row(512)] + [whole(w) for w in ws] + [whole(v) for v in vec],
        out_specs=row(D),
        out_shape=jax.ShapeDtypeStruct((T, D), F32),
        compiler_params=_cparams(("parallel",)),
        name="merge_project_norm",
    )(x2d, ya, yb, yc, *ws, *vec)


def _router_kernel(h_ref, wr_ref, bias_ref, eidx_ref, gate_ref, rank_ref, cnt_ref, carry_s, *, tm):
    @pl.when(pl.program_id(0) == 0)
    def _():
        carry_s[...] = jnp.zeros(carry_s.shape, F32)

    neg = -jnp.inf
    logits = _dot_nt(wr_ref[...], h_ref[...].astype(BF16))
    scores = jax.nn.sigmoid(logits)
    biased = scores + bias_ref[...]
    erow = lax.broadcasted_iota(I32, (N_EXPERTS, tm), 0)
    big = jnp.int32(2 ** 30)

    def top1(vals, ids):
        mx = jnp.max(vals, 0, keepdims=True)
        am = jnp.min(jnp.where(vals == mx, ids, big), 0, keepdims=True)
        return mx, am

    gs = []
    ids = lax.broadcasted_iota(I32, (GROUP_SIZE, tm), 0)
    for g in range(N_GROUPS):
        v = biased[g * GROUP_SIZE:(g + 1) * GROUP_SIZE]
        m1, a1 = top1(v, ids)
        m2, _ = top1(jnp.where(ids == a1, neg, v), ids)
        gs.append(m1 + m2)
    gscore = jnp.concatenate(gs, 0)
    grow = lax.broadcasted_iota(I32, (N_GROUPS, tm), 0)
    gsel = jnp.zeros((N_GROUPS, tm), F32)
    for _ in range(TOPK_GROUPS):
        _, ga = top1(gscore, grow)
        hit = grow == ga
        gsel = jnp.where(hit, 1.0, gsel)
        gscore = jnp.where(hit, neg, gscore)
    masked = jnp.concatenate(
        [jnp.where(gsel[g:g + 1] > 0.0, biased[g * GROUP_SIZE:(g + 1) * GROUP_SIZE], neg)
         for g in range(N_GROUPS)], 0)

    eids, ws = [], []
    hot = jnp.zeros((N_EXPERTS, tm), F32)
    for _ in range(TOP_K):
        _, ea = top1(masked, erow)
        hit = erow == ea
        eids.append(ea)
        ws.append(jnp.sum(jnp.where(hit, scores, 0.0), 0, keepdims=True))
        hot = jnp.where(hit, 1.0, hot)
        masked = jnp.where(hit, neg, masked)
    w = jnp.concatenate(ws, 0)
    eidx_ref[...] = jnp.concatenate(eids, 0)
    gate_ref[...] = w / jnp.sum(w, 0, keepdims=True) * ROUTED_SCALE

    r_i =lax.broadcasted_iota(I32, (tm, tm), 0)
    c_i = lax.broadcasted_iota(I32, (tm, tm), 1)
    before = jnp.where(r_i < c_i, 1.0, 0.0).astype(BF16)
    pos = carry_s[...] + _dot(hot.astype(BF16), before)
    rank_ref[...] = jnp.concatenate(
        [jnp.sum(jnp.where(erow == e, pos, 0.0), 0, keepdims=True) for e in eids], 0).astype(I32)
    carry_s[...] = carry_s[...] + jnp.sum(hot, 1, keepdims=True)
    cnt_ref[...] = carry_s[...].astype(I32)


def _route(h2d, w_router, router_bias, tm):
    T, D = h2d.shape
    out = lambda dt: jax.ShapeDtypeStruct((TOP_K, T), dt)
    blk = pl.BlockSpec((TOP_K, tm), lambda i: (0, i))
    return pl.pallas_call(
        functools.partial(_router_kernel, tm=tm),
        grid=(T // tm,),
        in_specs=[pl.BlockSpec((tm, D), lambda i: (i, 0)),
                  pl.BlockSpec((N_EXPERTS, D), lambda i: (0, 0)),
                  pl.BlockSpec((N_EXPERTS, 1), lambda i: (0, 0))],
        out_specs=[blk, blk, blk, pl.BlockSpec((N_EXPERTS, 1), lambda i: (0, 0))],
        out_shape=[out(I32), out(F32), out(I32), jax.ShapeDtypeStruct((N_EXPERTS, 1), I32)],
        scratch_shapes=[pltpu.VMEM((N_EXPERTS, 1), F32)],
        compiler_params=_cparams(("arbitrary",)),
        name="router",
    )(h2d, w_router.T.astype(BF16), router_bias.reshape(N_EXPERTS, 1).astype(F32))


def _dest_kernel(pstart_ref, eidx_ref, rank_ref, dest_ref):
    eidx = eidx_ref[...]

    def body(e, acc):
        return acc + jnp.where(eidx == e, pstart_ref[e], 0)

    dest_ref[...] = lax.fori_loop(0, N_EXPERTS, body, rank_ref[...])


def _destinations(pstart, eidx, rank, tm):
    T = eidx.shape[1]
    blk = lambda: pl.BlockSpec((TOP_K, tm), lambda i, ps: (0, i))
    return pl.pallas_call(
        _dest_kernel,
        grid_spec=pltpu.PrefetchScalarGridSpec(
            num_scalar_prefetch=1, grid=(T // tm,), in_specs=[blk(), blk()], out_specs=blk()),
        out_shape=jax.ShapeDtypeStruct((TOP_K, T), I32),
        compiler_params=_cparams(("parallel",)),
        name="destinations",
    )(pstart, eidx, rank)


def _token_tiles(a, tm):
    k, T = a.shape
    return a.reshape(k, T // tm, tm).transpose(1, 0, 2).reshape(T // tm, k * tm)


TOK_ALIGN = 128
DEPTH = 3


def _experts_kernel(blk_e_ref, blk_off_ref, used_ref, tok_hbm, h_hbm, wg_ref, wu_ref, wd_ref, y_ref,
                    tok_s, xbuf, wgb, wub, wdb, tok_sem, row_sem, *, rows):
    i = pl.program_id(0)
    used = used_ref[0]
    win = rows + TOK_ALIGN

    def window(blk):
        return pl.multiple_of(blk_off_ref[blk] // TOK_ALIGN * TOK_ALIGN, TOK_ALIGN)

    def tok_copy(blk, slot):
        return pltpu.make_async_copy(tok_hbm.at[pl.ds(window(blk), win)], tok_s.at[pl.ds(slot * win, win)],
                                     tok_sem.at[slot])

    def issue_rows(blk, slot, unrolled):
        base = slot * win + blk_off_ref[blk] - window(blk)

        def group(g):
            for rr in range(SUBLANES):
                r = g * SUBLANES + rr
                t = tok_s[base + r]
                pltpu.make_async_copy(h_hbm.at[pl.ds(t, 1)], xbuf.at[slot, pl.ds(r, 1)],
                                      row_sem.at[slot]).start(priority=rr % 2)
        if unrolled:
            for g in range(rows // SUBLANES):
                group(g)
        else:
            lax.fori_loop(0, rows // SUBLANES, lambda g, c: (group(g), 0)[1], 0)

    def rows_done(slot):
        pltpu.make_async_copy(h_hbm.at[pl.ds(0, rows)], xbuf.at[slot], row_sem.at[slot]).wait()

    def blk_at(j):
        return jnp.minimum(j, used - 1)

    @pl.when(i == 0)
    def _():
        for j in range(DEPTH - 1):
            tok_copy(blk_at(j), j).start()
        for j in range(DEPTH - 1):
            tok_copy(blk_at(j), j).wait()
            issue_rows(blk_at(j), j, False)
        tok_copy(blk_at(DEPTH - 1), DEPTH - 1).start()

    e = blk_e_ref[i]
    prev = blk_e_ref[jnp.maximum(i - 1, 0)]

    @pl.when((i == 0) | (e != prev))
    def _():
        wgb[...] = wg_ref[0].astype(BF16)
        wub[...] = wu_ref[0].astype(BF16)
        wdb[...] = wd_ref[0].astype(BF16)

    slot = i % DEPTH
    far = (i + DEPTH - 1) % DEPTH

    @pl.when(i < used)
    def _():
        ahead = blk_at(i + DEPTH - 1)
        tok_copy(ahead, far).wait()
        rows_done(slot)
        issue_rows(ahead, far, True)
        x = xbuf[slot].astype(BF16)
        hidden = jax.nn.silu(_dot(x, wgb[...])) * _dot(x, wub[...])
        y_ref[...] = _dot(hidden.astype(BF16), wdb[...])
        tok_copy(blk_at(i + DEPTH), slot).start()

    @pl.when(i == used - 1)
    def _():
        for j in range(1, DEPTH):
            rows_done((i + j) % DEPTH)
        tok_copy(used - 1, slot).wait()

    @pl.when(i >= used)
    def _():
        y_ref[...] = jnp.zeros(y_ref.shape, F32)


def _routed_experts(h2d, sorted_tok, blk_e, blk_off, n_used, n_blocks, w_eg, w_eu, w_ed, rows):
    T, D = h2d.shape
    ff = w_eg.shape[-1]
    return pl.pallas_call(
        functools.partial(_experts_kernel, rows=rows),
        grid_spec=pltpu.PrefetchScalarGridSpec(
            num_scalar_prefetch=3,
            grid=(n_blocks,),
            in_specs=[pl.BlockSpec(memory_space=pl.ANY),
                      pl.BlockSpec(memory_space=pl.ANY),
                      pl.BlockSpec((1, D, ff), lambda i, be, bo, nu: (be[i], 0, 0)),
                      pl.BlockSpec((1, D, ff), lambda i, be, bo, nu: (be[i], 0, 0)),
                      pl.BlockSpec((1, ff, D), lambda i, be, bo, nu: (be[i], 0, 0))],
            out_specs=pl.BlockSpec((rows, D), lambda i, be, bo, nu: (i, 0)),
            scratch_shapes=[pltpu.SMEM((DEPTH * (rows + TOK_ALIGN),), I32),
                            pltpu.VMEM((DEPTH, rows, D), F32),
                            pltpu.VMEM((D, ff), BF16),
                            pltpu.VMEM((D, ff), BF16),
                            pltpu.VMEM((ff, D), BF16),
                            pltpu.SemaphoreType.DMA((DEPTH,)),
                            pltpu.SemaphoreType.DMA((DEPTH,))]),
        out_shape=jax.ShapeDtypeStruct((n_blocks * rows, D), F32),
        compiler_params=_cparams(("arbitrary",)),
        name="routed_experts",
    )(blk_e, blk_off, n_used, sorted_tok, h2d, w_eg, w_eu, w_ed)


def _combine_kernel(dest_hbm, ys_hbm, h_ref, gate_ref, wsg_ref, wsu_ref, wsd_ref, g_ref, b_ref, o_ref,
                    dest_s, buf, dest_sem, row_sem, *, tm, alpha):
    i = pl.program_id(0)
    n = pl.num_programs(0)

    def dest_copy(blk, slot):
        return pltpu.make_async_copy(dest_hbm.at[blk], dest_s.at[pl.ds(slot * (TOP_K * tm), TOP_K * tm)],
                                     dest_sem.at[slot])

    def issue_rows(slot, unrolled=False):
        base = slot * (TOP_K * tm)

        def group(g):
            for rr in range(SUBLANES):
                r = g * SUBLANES + rr
                for k in range(TOP_K):
                    d = dest_s[base + k * tm + r]
                    pltpu.make_async_copy(ys_hbm.at[pl.ds(d, 1)], buf.at[slot, k, pl.ds(r, 1)],
                                          row_sem.at[slot]).start(priority=k % 2)
        if unrolled:
            for g in range(tm // SUBLANES):
                group(g)
        else:
            lax.fori_loop(0, tm // SUBLANES, lambda g, c: (group(g), 0)[1], 0)

    def rows_done(slot):
        for k in range(TOP_K):
            pltpu.make_async_copy(ys_hbm.at[pl.ds(0, tm)], buf.at[slot, k], row_sem.at[slot]).wait()

    def tile_at(j):
        return jnp.minimum(j, n - 1)

    @pl.when(i == 0)
    def _():
        for j in range(DEPTH - 1):
            dest_copy(tile_at(j), j).start()
        for j in range(DEPTH - 1):
            dest_copy(tile_at(j), j).wait()
            issue_rows(j)
        dest_copy(tile_at(DEPTH - 1), DEPTH - 1).start()

    slot = i % DEPTH
    far = (i + DEPTH - 1) % DEPTH
    dest_copy(tile_at(i + DEPTH - 1), far).wait()
    rows_done(slot)
    issue_rows(far, unrolled=True)
    h = h_ref[...]
    hb = h.astype(BF16)
    shared = _dot((jax.nn.silu(_dot(hb, wsg_ref[...])) * _dot(hb, wsu_ref[...])).astype(BF16), wsd_ref[...])
    total = alpha * h + shared
    for k in range(TOP_K):
        total = total + gate_ref[:, k:k + 1] * buf[slot, k]
    o_ref[...] = _layer_norm(total, g_ref[...], b_ref[...])
    dest_copy(tile_at(i + DEPTH), slot).start()

    @pl.when(i == n - 1)
    def _():
        for j in range(1, DEPTH):
            rows_done((i + j) % DEPTH)
        dest_copy(n - 1, slot).wait()


def _combine_shared_norm(h2d, ys, dest, gate, w_sg, w_su, w_sd, ln_g, ln_b, alpha, tm):
    T, D = h2d.shape
    nt = T // tm
    ws = [w.astype(BF16) for w in (w_sg, w_su, w_sd)]
    vec = [v.reshape(1, D).astype(F32) for v in (ln_g, ln_b)]
    whole = lambda a: pl.BlockSpec(a.shape, lambda i: (0, 0))
    return pl.pallas_call(
        functools.partial(_combine_kernel, tm=tm, alpha=alpha),
        grid=(nt,),
        in_specs=[pl.BlockSpec(memory_space=pl.ANY), pl.BlockSpec(memory_space=pl.ANY),
                  pl.BlockSpec((tm, D), lambda i: (i, 0)),
                  pl.BlockSpec((tm, TOP_K), lambda i: (i, 0))] + [whole(w) for w in ws] + [whole(v) for v in vec],
        out_specs=pl.BlockSpec((tm, D), lambda i: (i, 0)),
        out_shape=jax.ShapeDtypeStruct((T, D), F32),
        scratch_shapes=[pltpu.SMEM((DEPTH * TOP_K * tm,), I32),
                        pltpu.VMEM((DEPTH, TOP_K, tm, D), F32),
                        pltpu.SemaphoreType.DMA((DEPTH,)),
                        pltpu.SemaphoreType.DMA((DEPTH,))],
        compiler_params=_cparams(("arbitrary",)),
        name="combine_shared_norm",
    )(_token_tiles(dest, tm), ys, h2d, gate.T, *ws, *vec)


PROJ_TM = 512
DA_TQ, DA_TK = 1024, 512
DSA_TQ = 256
MEM_TQ = 512
MERGE_TM = 256
ROUTER_TM = 512
DEST_TM = 2048
COMBINE_TM = 128
MOE_ROWS = 256


def _dispatch_plan(counts, n_tokens, rows):
    counts = counts.reshape(N_EXPERTS)
    padded = (counts + rows - 1) // rows * rows
    pend = jnp.cumsum(padded)
    pstart = pend - padded
    cstart = jnp.cumsum(counts) - counts
    n_blocks = (n_tokens * TOP_K + N_EXPERTS * (rows - 1) + rows - 1) // rows
    n_used = pend[-1] // rows
    blk = jnp.minimum(jnp.arange(n_blocks, dtype=I32), n_used - 1)
    hot = (blk[:, None] * rows >= pend[None, :]).astype(I32)
    blk_e = jnp.sum(hot, 1)
    first = (jnp.arange(N_EXPERTS, dtype=I32)[None, :] == blk_e[:, None]).astype(I32)
    blk_off = jnp.sum(first * (cstart - pstart)[None, :], 1) + blk * rows
    return pstart.astype(I32), blk_e, blk_off.astype(I32), n_used.astype(I32).reshape(1), n_blocks


def _moe(h1, p, l, alpha):
    T = h1.shape[0]
    eidx, gate, rank, counts = _route(h1, p['w_router'][l], p['router_bias'][l], tm=ROUTER_TM)
    pstart, blk_e, blk_off, n_used, n_blocks = _dispatch_plan(counts, T, MOE_ROWS)
    dest = _destinations(pstart, eidx, rank, tm=DEST_TM)
    tok = jnp.broadcast_to(jnp.arange(T, dtype=I32)[None, :], (TOP_K, T))
    _, sorted_tok = lax.sort((dest.reshape(-1), tok.reshape(-1)), num_keys=1)
    sorted_tok = jnp.concatenate([sorted_tok, jnp.zeros((MOE_ROWS + 2 * TOK_ALIGN,), I32)])
    ys = _routed_experts(h1, sorted_tok, blk_e, blk_off, n_used, n_blocks,
                         p['w_eg'][l], p['w_eu'][l], p['w_ed'][l], MOE_ROWS)
    return _combine_shared_norm(h1, ys, dest, gate, p['w_sg'][l], p['w_su'][l], p['w_sd'][l],
                                p['ln2_g'][l], p['ln2_b'][l], alpha, tm=COMBINE_TM)


def _layer(h2d, mem2d, p, l, depth, batch, seq, mem_len):
    alpha = (2 * depth) ** 0.25
    lam_init = 0.8 - 0.6 * math.exp(-0.3 * l)
    w_in = p['w_in'][l]
    daq, dak, dsq, ixq, kv, vk, ikk, dav, memq, ixw = _in_projection(h2d, w_in, seq, tm=min(PROJ_TM, seq))
    lam = (jnp.exp(jnp.sum(p['lq1'][l].astype(F32) * p['lk1'][l].astype(F32)))
           - jnp.exp(jnp.sum(p['lq2'][l].astype(F32) * p['lk2'][l].astype(F32))) + lam_init)
    ya = _diff_attention(daq, dak, dav, lam, p['subln_g'][l], batch, seq, lam_init, tq=min(DA_TQ, seq), tk=min(DA_TK, seq))
    yb = _sparse_attention(ixq, ixw, dsq, ikk, kv, vk, batch, seq, tq=min(DSA_TQ, seq))
    mkv = _matmul(mem2d, p['w_mem_kv'][l].astype(BF16), tm=mem_len, out_dtype=BF16)
    yc = _memory_attention(memq, mkv, batch, seq, mem_len, tq=min(MEM_TQ, seq))
    h1 = _merge_project_norm(h2d, ya, yb, yc, w_in[:, 3272:], p['w_ba'][l], p['w_bb'][l], p['w_bc'][l],
                             p['w_o'][l], p['ln1_g'][l], p['ln1_b'][l], alpha, tm=MERGE_TM)
    return _moe(h1, p, l, alpha)


def kernel(x, mem, w_in, da_lambda_q1, da_lambda_k1, da_lambda_q2, da_lambda_k2, da_subln_g, w_mem_kv, w_branch_a, w_branch_b, w_branch_c, w_out, ln1_g, ln1_b, w_router, router_bias, w_exp_gate, w_exp_up, w_exp_down, w_sh_gate, w_sh_up, w_sh_down, ln2_g, ln2_b):
    batch, seq, d = x.shape
    mem_len = mem.shape[1]
    depth = w_in.shape[0]
    p = dict(w_in=w_in, lq1=da_lambda_q1, lk1=da_lambda_k1, lq2=da_lambda_q2, lk2=da_lambda_k2,
             subln_g=da_subln_g, w_mem_kv=w_mem_kv, w_ba=w_branch_a, w_bb=w_branch_b, w_bc=w_branch_c,
             w_o=w_out, ln1_g=ln1_g, ln1_b=ln1_b, w_router=w_router, router_bias=router_bias,
             w_eg=w_exp_gate, w_eu=w_exp_up, w_ed=w_exp_down, w_sg=w_sh_gate, w_su=w_sh_up,
             w_sd=w_sh_down, ln2_g=ln2_g, ln2_b=ln2_b)
    h = x.reshape(batch * seq, d)
    mem2d = mem.reshape(batch * mem_len, d)
    for l in range(depth):
        h = _layer(h, mem2d, p, l, depth, batch, seq, mem_len)
    return h.reshape(batch, seq, d)
```

```python
import functools
import math

import jax
import jax.numpy as jnp
import numpy as np
from jax import lax
from jax.experimental import pallas as pl
from jax.experimental.pallas import tpu as pltpu

F32 = jnp.float32
BF16 = jnp.bfloat16
I32 = jnp.int32

LANES = 128
SUBLANES = 8
ROPE_THETA = 10000.0
LN_EPS = 1e-5

DA_HEADS = 4
DA_HEAD_DIM = 64
DSA_HEADS = 8
DSA_HEAD_DIM = 64
IDX_HEADS = 8
IDX_ROWS = 16
DSA_TOPK_MAX = 256
MEM_HEADS = 4
MEM_HEAD_DIM = 128

N_EXPERTS = 256
TOP_K = 8
N_GROUPS = 8
TOPK_GROUPS = 4
GROUP_SIZE = N_EXPERTS // N_GROUPS
ROUTED_SCALE = 2.5

NEG_BIG = -1e30
INT_MIN = -(2 ** 31)

VMEM_LIMIT = 56 * 1024 * 1024


def _cparams(sem):
    return pltpu.CompilerParams(dimension_semantics=sem, vmem_limit_bytes=VMEM_LIMIT)


def _dot(a, b):
    return jnp.dot(a, b, preferred_element_type=F32)


def _dot_nt(a, b):
    return lax.dot_general(a, b, (((1,), (1,)), ((), ())), preferred_element_type=F32)


def _tree(op, xs):
    xs = list(xs)
    while len(xs) > 1:
        xs = [op(xs[k], xs[k + 1]) if k + 1 < len(xs) else xs[k] for k in range(0, len(xs), 2)]
    return xs[0]


def _layer_norm(x, g, b):
    mu = jnp.mean(x, -1, keepdims=True)
    xc = x - mu
    var = jnp.mean(xc * xc, -1, keepdims=True)
    return xc * lax.rsqrt(var + LN_EPS) * g + b


def _inproj_kernel(x_ref, wr_ref, wn_ref, wixt_ref, tab_ref,
                   daq_ref, dak_ref, dsq_ref, ixq_ref, kv_ref, vk_ref, ikk_ref,
                   dav_ref, memq_ref, ixwt_ref):
    xb = x_ref[...].astype(BF16)

    def rope(t, kind):
        c = tab_ref[3 * kind]
        sa = tab_ref[3 * kind + 1]
        sb = tab_ref[3 * kind + 2]
        return t * c + pltpu.roll(t, 96, 1) * sa + pltpu.roll(t, 32, 1) * sb

    groups = ((daq_ref, 0.125), (dak_ref, 1.0), (dsq_ref, 0.125), (ixq_ref, 1.0))
    for gi, (ref, scale) in enumerate(groups):
        y = _dot(xb, wr_ref[:, gi * 512:(gi + 1) * 512])
        for c in range(4):
            r = rope(y[:, c * LANES:(c + 1) * LANES], 0)
            if scale != 1.0:
                r = r * scale
            ref[:, c * LANES:(c + 1) * LANES] = r.astype(BF16)
    y = _dot(xb, wr_ref[:, 2048:2432])
    kv_ref[...] = rope(y[:, 0:128], 1).astype(BF16)
    vk_ref[...] = rope(y[:, 128:256], 2).astype(BF16)
    ikk_ref[...] = rope(y[:, 256:384], 0).astype(BF16)
    y = _dot(xb, wn_ref[...])
    dav_ref[...] = y[:, 0:512].astype(BF16)
    memq_ref[...] = y[:, 512:1024].astype(BF16)
    ixwt_ref[...] = _dot_nt(wixt_ref[...], xb)


def _rope_tables(seq):
    dim = 64
    inv = 1.0 / (ROPE_THETA ** (jnp.arange(0, dim, 2, dtype=F32) / dim))
    ang = jnp.arange(seq, dtype=F32)[:, None] * inv[None, :]
    ang = jnp.concatenate([ang, ang], -1)
    cos, sin = jnp.cos(ang), jnp.sin(ang)
    local = jnp.arange(dim)[None, :]
    sa = jnp.where(local < dim // 2, -sin, 0.0)
    sb = jnp.where(local >= dim // 2, sin, 0.0)
    one, zero = jnp.ones_like(cos), jnp.zeros_like(cos)
    cat = lambda a, b: jnp.concatenate([a, b], -1)
    return jnp.stack([cat(cos, cos), cat(sa, sa), cat(sb, sb),
                      cat(cos, one), cat(sa, zero), cat(sb, zero),
                      cat(one, cos), cat(zero, sa), cat(zero, sb)], 0)


def _in_projection(x2d, w_in, seq, tm):
    T, D = x2d.shape
    o = np.cumsum([0, 512, 512, 512, 512, 64, 64, 512, 64, 8, 512]).tolist()
    col = lambda i: w_in[:, o[i]:o[i + 1]]
    da_q, da_k, da_v, ds_q, ds_k, ds_v, ix_q, ix_k, ix_w, mem_q = [col(i) for i in range(10)]
    ds_q = ds_q.reshape(D, DSA_HEADS // 2, 2, DSA_HEAD_DIM)[:, :, ::-1, :].reshape(D, 512)
    w_rope = jnp.concatenate([da_q, da_k, ds_q, ix_q, ds_k, ds_v, ds_v, ds_k, ix_k, ix_k], 1).astype(BF16)
    w_plain = jnp.concatenate([da_v, mem_q], 1).astype(BF16)
    w_ixt = jnp.concatenate([ix_w.T, jnp.zeros((IDX_ROWS - IDX_HEADS, D), w_in.dtype)], 0).astype(BF16)
    tabs = _rope_tables(seq)
    nseq = seq // tm
    bf = lambda n: jax.ShapeDtypeStruct((T, n), BF16)
    row = lambda n: pl.BlockSpec((tm, n), lambda i: (i, 0))
    return pl.pallas_call(
        _inproj_kernel,
        grid=(T // tm,),
        in_specs=[row(D),
                  pl.BlockSpec(w_rope.shape, lambda i: (0, 0)),
                  pl.BlockSpec(w_plain.shape, lambda i: (0, 0)),
                  pl.BlockSpec(w_ixt.shape, lambda i: (0, 0)),
                  pl.BlockSpec((9, tm, LANES), lambda i: (0, i % nseq, 0))],
        out_specs=[row(512), row(512), row(512), row(512), row(128), row(128), row(128),
                   row(512), row(512), pl.BlockSpec((IDX_ROWS, tm), lambda i: (0, i))],
        out_shape=[bf(512), bf(512), bf(512), bf(512), bf(128), bf(128), bf(128),
                   bf(512), bf(512), jax.ShapeDtypeStruct((IDX_ROWS, T), F32)],
        compiler_params=_cparams(("parallel",)),
        name="in_projection",
    )(x2d, w_rope, w_plain, w_ixt, tabs)


def _diffattn_kernel(lam_ref, q_ref, k_ref, v_ref, g_ref, o_ref, m_s, l_s, a_s, *, tq, tk, lam_init):
    i = pl.program_id(2)
    lam = lam_ref[0]
    q = q_ref[...]
    lane = lax.broadcasted_iota(I32, q.shape, 1)
    zero = jnp.zeros_like(q)
    qs = (jnp.where(lane < DA_HEAD_DIM, q, zero), jnp.where(lane >= DA_HEAD_DIM, q, zero))
    m_s[...] = jnp.full(m_s.shape, -jnp.inf, F32)
    l_s[...] = jnp.zeros(l_s.shape, F32)
    a_s[...] = jnp.zeros(a_s.shape, F32)

    def step(j, masked, r0=0):
        off = pl.multiple_of(j * tk, tk)
        kj = k_ref[pl.ds(off, tk), :]
        vj = v_ref[pl.ds(off, tk), :]
        for mp in range(2):
            s = _dot_nt(qs[mp][r0:], kj)
            if masked:
                row = lax.broadcasted_iota(I32, (tq - r0, tk), 0) + r0
                col = lax.broadcasted_iota(I32, (tq - r0, tk), 1)
                s = jnp.where(j * tk + col <= i * tq + row, s, -jnp.inf)
            tiles = [s[:, c * LANES:(c + 1) * LANES] for c in range(tk // LANES)]
            m_old = m_s[mp, r0:]
            m_blk = jnp.max(_tree(jnp.maximum, tiles), -1, keepdims=True)
            m_new = jnp.maximum(m_old, jnp.broadcast_to(m_blk, (tq - r0, LANES)))
            ps = [jnp.exp(t - m_new) for t in tiles]
            al = jnp.exp(m_old - m_new)
            l_s[mp, r0:] = al * l_s[mp, r0:] + _tree(jnp.add, ps)
            a_s[mp, r0:] = al * a_s[mp, r0:] + _dot(jnp.concatenate(ps, 1).astype(BF16), vj)
            m_s[mp, r0:] = m_new
        return 0

    per_q = tq // tk

    def fully_visible(t, _):
        for jj in range(per_q):
            step(t * per_q + jj, False)
        return 0

    lax.fori_loop(0, i, fully_visible, 0)
    for jj in range(per_q):
        step(i * per_q + jj, True, jj * tk)
    o1 = a_s[0] / jnp.sum(l_s[0], -1, keepdims=True)
    o2 = a_s[1] / jnp.sum(l_s[1], -1, keepdims=True)
    o = o1 - lam * o2
    y = o * lax.rsqrt(jnp.mean(o * o, -1, keepdims=True) + LN_EPS) * g_ref[...] * (1.0 - lam_init)
    o_ref[...] = y.astype(BF16)


def _diff_attention(daq, dak, dav, lam, subln_g, batch, seq, lam_init, tq, tk):
    T = batch * seq
    nq = seq // tq
    state = pltpu.VMEM((2, tq, LANES), F32)
    return pl.pallas_call(
        functools.partial(_diffattn_kernel, tq=tq, tk=tk, lam_init=lam_init),
        grid=(batch, DA_HEADS, nq),
        in_specs=[pl.BlockSpec(memory_space=pltpu.SMEM),
                  pl.BlockSpec((tq, LANES), lambda b, h, i: (b * nq + i, h)),
                  pl.BlockSpec((seq, LANES), lambda b, h, i: (b, h)),
                  pl.BlockSpec((seq, LANES), lambda b, h, i: (b, h)),
                  pl.BlockSpec((1, LANES), lambda b, h, i: (0, 0))],
        out_specs=pl.BlockSpec((tq, LANES), lambda b, h, i: (b * nq + i, h)),
        out_shape=jax.ShapeDtypeStruct((T, DA_HEADS * LANES), BF16),
        scratch_shapes=[state, state, state],
        compiler_params=_cparams(("parallel", "parallel", "arbitrary")),
        name="diff_attention",
    )(lam.reshape(1), daq, dak, dav, subln_g.reshape(1, LANES).astype(F32))


def _dsa_kernel(ixq_ref, ixwt_ref, dsq_ref, ikk_ref, kv_ref, vk_ref, o_ref,
                qi_s, qd_s, sc_s, g_s, mx_s, l_s, acc_s, *, tq, topk):
    i = pl.program_id(1)
    nblk = i + 1
    tk = tq
    nslab = tk // 8
    lane = lax.broadcasted_iota(I32, (tq, LANES), 1)
    lower = lane < DSA_HEAD_DIM
    npair = DSA_HEADS // 2
    for h in range(IDX_HEADS):
        sl = slice((h // 2) * LANES, (h // 2 + 1) * LANES)
        keep = lower if h % 2 == 0 else jnp.logical_not(lower)
        blk = ixq_ref[:, sl]
        qi_s[h * tq:(h + 1) * tq, :] = jnp.where(keep, blk, jnp.zeros_like(blk))
        blk = dsq_ref[:, sl]
        qd_s[h % 2, (h // 2) * tq:(h // 2 + 1) * tq, :] = jnp.where(keep, blk, jnp.zeros_like(blk))
    krow = lax.broadcasted_iota(I32, (tk, tq), 0)
    qcol = lax.broadcasted_iota(I32, (tk, tq), 1)
    sub8 = lax.broadcasted_iota(I32, (8, tq), 0)

    def score_chunk(j, _):
        off = pl.multiple_of(j * tk, tk)
        ik = ikk_ref[pl.ds(off, tk), :]
        logits = _dot_nt(ik, qi_s[...])
        sc = jnp.zeros((tk, tq), F32)
        for h in range(IDX_HEADS):
            sc = sc + ixwt_ref[h:h + 1, :] * jnp.maximum(logits[:, h * tq:(h + 1) * tq], 0.0)
        sc = jnp.where(j * tk + krow <= i * tq + qcol, sc, -jnp.inf)
        sc_s[j] = sc
        g_s[j] = sc.astype(BF16)
        return 0

    def over_chunks(body, width=4):
        def group(jg, _):
            for u in range(width):
                body(width * jg + u)
            return 0
        lax.fori_loop(0, nblk // width, group, 0)
        lax.fori_loop(nblk // width * width, nblk, lambda j, c: (body(j), 0)[1], 0)

    over_chunks(lambda j: score_chunk(j, 0))

    n_acc = 4

    def count(pred):
        def body(j, accs):
            accs = list(accs)
            for r in range(nslab):
                hit = pred(sc_s[j, r * 8:(r + 1) * 8, :], j * tk + r * 8 + sub8)
                accs[r % n_acc] = accs[r % n_acc] + hit.astype(F32)
            return tuple(accs)
        accs = lax.fori_loop(0, nblk, body, tuple(jnp.zeros((8, tq), F32) for _ in range(n_acc)))
        return jnp.sum(_tree(jnp.add, accs), 0, keepdims=True)

    def rows8(v):
        return jnp.broadcast_to(v, (8, tq))

    def as_score(key):
        return pltpu.bitcast(key ^ ((key >> 31) & 0x7FFFFFFF), F32)

    def count_coarse(cand):
        cb = jnp.broadcast_to(cand, (16, tq))
        one, nil = jnp.ones((16, tq), BF16), jnp.zeros((16, tq), BF16)

        def body(j, accs):
            accs = list(accs)
            for r in range(tk // 16):
                hit = g_s[j, r * 16:(r + 1) * 16, :] >= cb
                accs[r % n_acc] = accs[r % n_acc] + jnp.where(hit, one, nil)
            return tuple(accs)
        accs = lax.fori_loop(0, nblk, body, tuple(nil for _ in range(n_acc)))
        return jnp.sum(_tree(jnp.add, [a.astype(F32) for a in accs]), 0, keepdims=True)

    def as_coarse(key16):
        bits16 = key16 ^ ((key16 >> 15) & 0x7FFF)
        return pltpu.bitcast(bits16 << 16, F32).astype(BF16)

    kf = float(topk)
    zero_i = jnp.zeros((1, tq), I32)

    def coarse_bit(b, hi):
        cand = jnp.where(b == 0, zero_i, hi | (jnp.int32(1) << (15 - b)))
        c = count_coarse(as_coarse(cand))
        return jnp.where(c >= kf, cand, hi)

    hi16 = lax.fori_loop(0, 16, coarse_bit, jnp.full((1, tq), -(2 ** 15), I32))
    found = hi16 != -(2 ** 15)
    key_t = jnp.where(found, (hi16 << 16) + jnp.where(hi16 < 0, 0xFFFF, 0), 0)
    span = (1 << 16) + (1 << 15) + 2

    def bisect(_, bounds):
        lo, width = bounds
        step = (width + 1) >> 1
        cand = lo + step
        cb = rows8(as_score(cand))
        ok = count(lambda s, kpos: s >= cb) >= kf
        return jnp.where(ok, cand, lo), jnp.where(ok, width - step, step - 1)

    lo, _ = lax.fori_loop(0, 17, bisect, (key_t - ((1 << 15) + 1), jnp.full((1, tq), span, I32)))
    tau = jnp.where(found, lo, INT_MIN)
    tau_f = jnp.where(tau == INT_MIN, -jnp.inf, as_score(tau))
    tau8 = rows8(tau_f)
    c_ge = count(lambda s, kpos: s >= tau8)

    def tie_cut():
        need = kf - count(lambda s, kpos: s > tau8)

        def cut_bit(b, cut):
            cand = cut | (jnp.int32(1) << (12 - b))
            cb = rows8(cand)
            c = count(lambda s, kpos: (s == tau8) & (kpos < cb))
            return jnp.where(c <= need, cand, cut)

        return lax.fori_loop(0, 13, cut_bit, zero_i)

    cut = lax.cond(jnp.max(c_ge) > kf, tie_cut, lambda: jnp.full((1, tq), 2 ** 13, I32))

    mx_s[...] = jnp.full(mx_s.shape, NEG_BIG, F32)
    l_s[...] = jnp.zeros(l_s.shape, F32)
    acc_s[...] = jnp.zeros(acc_s.shape, F32)
    nlt = tk // LANES

    def lane_tiles(a):
        return [a[:, c * LANES:(c + 1) * LANES] for c in range(nlt)]

    def bcast(v):
        return jnp.broadcast_to(v, (v.shape[0], LANES))

    def row_max(j, _):
        off = pl.multiple_of(j * tk, tk)
        sc = sc_s[j]
        kpos = j * tk + krow
        sel = (sc > tau_f) | ((sc == tau_f) & (kpos < cut))
        sel = sel & (kpos <= i * tq + qcol)
        bias = jnp.where(sel, 0.0, NEG_BIG).T
        sc_s[j] = bias
        kvj = kv_ref[pl.ds(off, tk), :]
        vkj = vk_ref[pl.ds(off, tk), :]
        for par, kk in enumerate((kvj, vkj)):
            s = (_dot_nt(qd_s[par], kk).reshape(npair, tq, tk) + bias[None]).reshape(npair * tq, tk)
            mx_s[par] = functools.reduce(jnp.maximum, lane_tiles(s), mx_s[par])
        return 0

    over_chunks(lambda j: row_max(j, 0))
    for par in range(2):
        mx_s[par] = bcast(jnp.max(mx_s[par], -1, keepdims=True))

    def attend(j, _):
        off = pl.multiple_of(j * tk, tk)
        bias = sc_s[j]
        kvj = kv_ref[pl.ds(off, tk), :]
        vkj = vk_ref[pl.ds(off, tk), :]
        for par, kk in enumerate((kvj, vkj)):
            s = (_dot_nt(qd_s[par], kk).reshape(npair, tq, tk) + bias[None]).reshape(npair * tq, tk)
            m = mx_s[par]
            ps = [jnp.exp(t - m) for t in lane_tiles(s)]
            l_s[par] = functools.reduce(jnp.add, ps, l_s[par])
            acc_s[par] = acc_s[par] + _dot(jnp.concatenate(ps, 1).astype(BF16), kk)
        return 0

    over_chunks(lambda j: attend(j, 0))

    def out(h):
        rows = slice((h // 2) * tq, (h // 2 + 1) * tq)
        return acc_s[h % 2, rows, :] / jnp.sum(l_s[h % 2, rows, :], -1, keepdims=True)

    for pr in range(DSA_HEADS // 2):
        o_ref[:, pr * LANES:(pr + 1) * LANES] = jnp.where(lower, out(2 * pr + 1), out(2 * pr)).astype(BF16)


def _sparse_attention(ixq, ixwt, dsq, ikk, kv, vk, batch, seq, tq):
    T = batch * seq
    nq = seq // tq
    topk = min(DSA_TOPK_MAX, seq // 4)
    tile = lambda n: pl.BlockSpec((tq, n), lambda b, i: (b * nq + i, 0))
    full = pl.BlockSpec((seq, LANES), lambda b, i: (b, 0))
    return pl.pallas_call(
        functools.partial(_dsa_kernel, tq=tq, topk=topk),
        grid=(batch, nq),
        in_specs=[tile(512), pl.BlockSpec((ixwt.shape[0], tq), lambda b, i: (0, b * nq + i)), tile(512),
                  full, full, full],
        out_specs=tile(512),
        out_shape=jax.ShapeDtypeStruct((T, 512), BF16),
        scratch_shapes=[pltpu.VMEM((IDX_HEADS * tq, LANES), BF16),
                        pltpu.VMEM((2, DSA_HEADS // 2 * tq, LANES), BF16),
                        pltpu.VMEM((nq, tq, tq), F32),
                        pltpu.VMEM((nq, tq, tq), BF16),
                        pltpu.VMEM((2, DSA_HEADS // 2 * tq, LANES), F32),
                        pltpu.VMEM((2, DSA_HEADS // 2 * tq, LANES), F32),
                        pltpu.VMEM((2, DSA_HEADS // 2 * tq, LANES), F32)],
        compiler_params=_cparams(("parallel", "arbitrary")),
        name="sparse_attention",
    )(ixq, ixwt, dsq, ikk, kv, vk)


def _matmul_kernel(x_ref, w_ref, o_ref):
    o_ref[...] = _dot(x_ref[...].astype(BF16), w_ref[...]).astype(o_ref.dtype)


def _matmul(x, w, tm, out_dtype):
    M, K = x.shape
    N = w.shape[1]
    return pl.pallas_call(
        _matmul_kernel,
        grid=(M // tm,),
        in_specs=[pl.BlockSpec((tm, K), lambda i: (i, 0)), pl.BlockSpec((K, N), lambda i: (0, 0))],
        out_specs=pl.BlockSpec((tm, N), lambda i: (i, 0)),
        out_shape=jax.ShapeDtypeStruct((M, N), out_dtype),
        compiler_params=_cparams(("parallel",)),
        name="matmul",
    )(x, w)


def _memattn_kernel(q_ref, kv_ref, o_ref):
    scale = MEM_HEAD_DIM ** -0.5
    width = MEM_HEADS * MEM_HEAD_DIM
    for h in range(MEM_HEADS):
        sl = slice(h * MEM_HEAD_DIM, (h + 1) * MEM_HEAD_DIM)
        k = kv_ref[:, sl]
        v = kv_ref[:, width + h * MEM_HEAD_DIM:width + (h + 1) * MEM_HEAD_DIM]
        s = _dot_nt(q_ref[:, sl], k) * scale
        p = jnp.exp(s - jnp.max(s, -1, keepdims=True))
        o = _dot(p.astype(BF16), v) / jnp.sum(p, -1, keepdims=True)
        o_ref[:, sl] = o.astype(BF16)


def _memory_attention(memq, mkv, batch, seq, mem_len, tq):
    T = batch * seq
    nq = seq // tq
    return pl.pallas_call(
        _memattn_kernel,
        grid=(batch, nq),
        in_specs=[pl.BlockSpec((tq, 512), lambda b, i: (b * nq + i, 0)),
                  pl.BlockSpec((mem_len, 1024), lambda b, i: (b, 0))],
        out_specs=pl.BlockSpec((tq, 512), lambda b, i: (b * nq + i, 0)),
        out_shape=jax.ShapeDtypeStruct((T, 512), BF16),
        compiler_params=_cparams(("parallel", "parallel")),
        name="memory_attention",
    )(memq, mkv)


def _merge_kernel(x_ref, ya_ref, yb_ref, yc_ref, wg_ref, wa_ref, wb_ref, wc_ref, wo_ref,
                  g_ref, b_ref, h_ref, *, alpha):
    x = x_ref[...]
    xb = x.astype(BF16)
    d = x.shape[1]
    merged = None
    for br, (y_ref, w_ref) in enumerate(((ya_ref, wa_ref), (yb_ref, wb_ref), (yc_ref, wc_ref))):
        gate = jax.nn.sigmoid(_dot(xb, wg_ref[:, br * d:(br + 1) * d]))
        term = gate * _dot(y_ref[...], w_ref[...])
        merged = term if merged is None else merged + term
    mix = _dot(merged.astype(BF16), wo_ref[...])
    h_ref[...] = _layer_norm(alpha * x + mix, g_ref[...], b_ref[...])


def _merge_project_norm(x2d, ya, yb, yc, w_gates, w_ba, w_bb, w_bc, w_o, ln_g, ln_b, alpha, tm):
    T, D = x2d.shape
    row = lambda n: pl.BlockSpec((tm, n), lambda i: (i, 0))
    whole = lambda a: pl.BlockSpec(a.shape, lambda i: (0, 0))
    ws = [w.astype(BF16) for w in (w_gates, w_ba, w_bb, w_bc, w_o)]
    vec = [v.reshape(1, D).astype(F32) for v in (ln_g, ln_b)]
    return pl.pallas_call(
        functools.partial(_merge_kernel, alpha=alpha),
        grid=(T // tm,),
        in_specs=[row(D), row(512), row(512), row(512)] + [whole(w) for w in ws] + [whole(v) for v in vec],
        out_specs=row(D),
        out_shape=jax.ShapeDtypeStruct((T, D), F32),
        compiler_params=_cparams(("parallel",)),
        name="merge_project_norm",
    )(x2d, ya, yb, yc, *ws, *vec)


def _router_kernel(h_ref, wr_ref, bias_ref, eidx_ref, gate_ref, rank_ref, cnt_ref, carry_s, *, tm):
    @pl.when(pl.program_id(0) == 0)
    def _():
        carry_s[...] = jnp.zeros(carry_s.shape, F32)

    neg = -jnp.inf
    logits = _dot_nt(wr_ref[...], h_ref[...].astype(BF16))
    scores = jax.nn.sigmoid(logits)
    biased = scores + bias_ref[...]
    erow = lax.broadcasted_iota(I32, (N_EXPERTS, tm), 0)
    big = jnp.int32(2 ** 30)

    def top1(vals, ids):
        mx = jnp.max(vals, 0, keepdims=True)
        am = jnp.min(jnp.where(vals == mx, ids, big), 0, keepdims=True)
        return mx, am

    gs = []
    ids = lax.broadcasted_iota(I32, (GROUP_SIZE, tm), 0)
    for g in range(N_GROUPS):
        v = biased[g * GROUP_SIZE:(g + 1) * GROUP_SIZE]
        m1, a1 = top1(v, ids)
        m2, _ = top1(jnp.where(ids == a1, neg, v), ids)
        gs.append(m1 + m2)
    gscore = jnp.concatenate(gs, 0)
    grow = lax.broadcasted_iota(I32, (N_GROUPS, tm), 0)
    gsel = jnp.zeros((N_GROUPS, tm), F32)
    for _ in range(TOPK_GROUPS):
        _, ga = top1(gscore, grow)
        hit = grow == ga
        gsel = jnp.where(hit, 1.0, gsel)
        gscore = jnp.where(hit, neg, gscore)
    masked = jnp.concatenate(
        [jnp.where(gsel[g:g + 1] > 0.0, biased[g * GROUP_SIZE:(g + 1) * GROUP_SIZE], neg)
         for g in range(N_GROUPS)], 0)

    eids, ws = [], []
    hot = jnp.zeros((N_EXPERTS, tm), F32)
    for _ in range(TOP_K):
        _, ea = top1(masked, erow)
        hit = erow == ea
        eids.append(ea)
        ws.append(jnp.sum(jnp.where(hit, scores, 0.0), 0, keepdims=True))
        hot = jnp.where(hit, 1.0, hot)
        masked = jnp.where(hit, neg, masked)
    w = jnp.concatenate(ws, 0)
    eidx_ref[...] = jnp.concatenate(eids, 0)
    gate_ref[...] = w / jnp.sum(w, 0, keepdims=True) * ROUTED_SCALE

    r_i =lax.broadcasted_iota(I32, (tm, tm), 0)
    c_i = lax.broadcasted_iota(I32, (tm, tm), 1)
    before = jnp.where(r_i < c_i, 1.0, 0.0).astype(BF16)
    pos = carry_s[...] + _dot(hot.astype(BF16), before)
    rank_ref[...] = jnp.concatenate(
        [jnp.sum(jnp.where(erow == e, pos, 0.0), 0, keepdims=True) for e in eids], 0).astype(I32)
    carry_s[...] = carry_s[...] + jnp.sum(hot, 1, keepdims=True)
    cnt_ref[...] = carry_s[...].astype(I32)


def _route(h2d, w_router, router_bias, tm):
    T, D = h2d.shape
    out = lambda dt: jax.ShapeDtypeStruct((TOP_K, T), dt)
    blk = pl.BlockSpec((TOP_K, tm), lambda i: (0, i))
    return pl.pallas_call(
        functools.partial(_router_kernel, tm=tm),
        grid=(T // tm,),
        in_specs=[pl.BlockSpec((tm, D), lambda i: (i, 0)),
                  pl.BlockSpec((N_EXPERTS, D), lambda i: (0, 0)),
                  pl.BlockSpec((N_EXPERTS, 1), lambda i: (0, 0))],
        out_specs=[blk, blk, blk, pl.BlockSpec((N_EXPERTS, 1), lambda i: (0, 0))],
        out_shape=[out(I32), out(F32), out(I32), jax.ShapeDtypeStruct((N_EXPERTS, 1), I32)],
        scratch_shapes=[pltpu.VMEM((N_EXPERTS, 1), F32)],
        compiler_params=_cparams(("arbitrary",)),
        name="router",
    )(h2d, w_router.T.astype(BF16), router_bias.reshape(N_EXPERTS, 1).astype(F32))


def _dest_kernel(pstart_ref, eidx_ref, rank_ref, dest_ref):
    eidx = eidx_ref[...]

    def body(e, acc):
        return acc + jnp.where(eidx == e, pstart_ref[e], 0)

    dest_ref[...] = lax.fori_loop(0, N_EXPERTS, body, rank_ref[...])


def _destinations(pstart, eidx, rank, tm):
    T = eidx.shape[1]
    blk = lambda: pl.BlockSpec((TOP_K, tm), lambda i, ps: (0, i))
    return pl.pallas_call(
        _dest_kernel,
        grid_spec=pltpu.PrefetchScalarGridSpec(
            num_scalar_prefetch=1, grid=(T // tm,), in_specs=[blk(), blk()], out_specs=blk()),
        out_shape=jax.ShapeDtypeStruct((TOP_K, T), I32),
        compiler_params=_cparams(("parallel",)),
        name="destinations",
    )(pstart, eidx, rank)


def _token_tiles(a, tm):
    k, T = a.shape
    return a.reshape(k, T // tm, tm).transpose(1, 0, 2).reshape(T // tm, k * tm)


TOK_ALIGN = 128
DEPTH = 3


def _experts_kernel(blk_e_ref, blk_off_ref, used_ref, tok_hbm, h_hbm, wg_ref, wu_ref, wd_ref, y_ref,
                    tok_s, xbuf, wgb, wub, wdb, tok_sem, row_sem, *, rows):
    i = pl.program_id(0)
    used = used_ref[0]
    win = rows + TOK_ALIGN

    def window(blk):
        return pl.multiple_of(blk_off_ref[blk] // TOK_ALIGN * TOK_ALIGN, TOK_ALIGN)

    def tok_copy(blk, slot):
        return pltpu.make_async_copy(tok_hbm.at[pl.ds(window(blk), win)], tok_s.at[pl.ds(slot * win, win)],
                                     tok_sem.at[slot])

    def issue_rows(blk, slot, unrolled):
        base = slot * win + blk_off_ref[blk] - window(blk)

        def group(g):
            for rr in range(SUBLANES):
                r = g * SUBLANES + rr
                t = tok_s[base + r]
                pltpu.make_async_copy(h_hbm.at[pl.ds(t, 1)], xbuf.at[slot, pl.ds(r, 1)], row_sem.at[slot]).start()
        if unrolled:
            for g in range(rows // SUBLANES):
                group(g)
        else:
            lax.fori_loop(0, rows // SUBLANES, lambda g, c: (group(g), 0)[1], 0)

    def rows_done(slot):
        pltpu.make_async_copy(h_hbm.at[pl.ds(0, rows)], xbuf.at[slot], row_sem.at[slot]).wait()

    def blk_at(j):
        return jnp.minimum(j, used - 1)

    @pl.when(i == 0)
    def _():
        for j in range(DEPTH - 1):
            tok_copy(blk_at(j), j).start()
        for j in range(DEPTH - 1):
            tok_copy(blk_at(j), j).wait()
            issue_rows(blk_at(j), j, False)
        tok_copy(blk_at(DEPTH - 1), DEPTH - 1).start()

    e = blk_e_ref[i]
    prev = blk_e_ref[jnp.maximum(i - 1, 0)]

    @pl.when((i == 0) | (e != prev))
    def _():
        wgb[...] = wg_ref[0].astype(BF16)
        wub[...] = wu_ref[0].astype(BF16)
        wdb[...] = wd_ref[0].astype(BF16)

    def step(slot):
        far = (slot + DEPTH - 1) % DEPTH
        ahead = blk_at(i + DEPTH - 1)
        tok_copy(ahead, far).wait()
        rows_done(slot)
        issue_rows(ahead, far, True)
        x = xbuf[slot].astype(BF16)
        hidden = jax.nn.silu(_dot(x, wgb[...])) * _dot(x, wub[...])
        y_ref[...] = _dot(hidden.astype(BF16), wdb[...])
        tok_copy(blk_at(i + DEPTH), slot).start()

    for slot in range(DEPTH):
        pl.when((i < used) & (i % DEPTH == slot))(functools.partial(step, slot))

    @pl.when(i == used - 1)
    def _():
        for j in range(1, DEPTH):
            rows_done((i + j) % DEPTH)
        tok_copy(used - 1, i % DEPTH).wait()

    @pl.when(i >= used)
    def _():
        y_ref[...] = jnp.zeros(y_ref.shape, F32)


def _routed_experts(h2d, sorted_tok, blk_e, blk_off, n_used, n_blocks, w_eg, w_eu, w_ed, rows):
    T, D = h2d.shape
    ff = w_eg.shape[-1]
    return pl.pallas_call(
        functools.partial(_experts_kernel, rows=rows),
        grid_spec=pltpu.PrefetchScalarGridSpec(
            num_scalar_prefetch=3,
            grid=(n_blocks,),
            in_specs=[pl.BlockSpec(memory_space=pl.ANY),
                      pl.BlockSpec(memory_space=pl.ANY),
                      pl.BlockSpec((1, D, ff), lambda i, be, bo, nu: (be[i], 0, 0)),
                      pl.BlockSpec((1, D, ff), lambda i, be, bo, nu: (be[i], 0, 0)),
                      pl.BlockSpec((1, ff, D), lambda i, be, bo, nu: (be[i], 0, 0))],
            out_specs=pl.BlockSpec((rows, D), lambda i, be, bo, nu: (i, 0)),
            scratch_shapes=[pltpu.SMEM((DEPTH * (rows + TOK_ALIGN),), I32),
                            pltpu.VMEM((DEPTH, rows, D), F32),
                            pltpu.VMEM((D, ff), BF16),
                            pltpu.VMEM((D, ff), BF16),
                            pltpu.VMEM((ff, D), BF16),
                            pltpu.SemaphoreType.DMA((DEPTH,)),
                            pltpu.SemaphoreType.DMA((DEPTH,))]),
        out_shape=jax.ShapeDtypeStruct((n_blocks * rows, D), F32),
        compiler_params=_cparams(("arbitrary",)),
        name="routed_experts",
    )(blk_e, blk_off, n_used, sorted_tok, h2d, w_eg, w_eu, w_ed)


def _combine_kernel(dest_hbm, ys_hbm, h_ref, gate_ref, wsg_ref, wsu_ref, wsd_ref, g_ref, b_ref, o_ref,
                    dest_s, buf, dest_sem, row_sem, *, tm, alpha):
    i = pl.program_id(0)
    n = pl.num_programs(0)

    def dest_copy(blk, slot):
        return pltpu.make_async_copy(dest_hbm.at[blk], dest_s.at[pl.ds(slot * (TOP_K * tm), TOP_K * tm)],
                                     dest_sem.at[slot])

    def issue_rows(slot, unrolled=False):
        base = slot * (TOP_K * tm)

        def group(g):
            for rr in range(SUBLANES):
                r = g * SUBLANES + rr
                for k in range(TOP_K):
                    d = dest_s[base + k * tm + r]
                    pltpu.make_async_copy(ys_hbm.at[pl.ds(d, 1)], buf.at[slot, k, pl.ds(r, 1)],
                                          row_sem.at[slot]).start()
        if unrolled:
            for g in range(tm // SUBLANES):
                group(g)
        else:
            lax.fori_loop(0, tm // SUBLANES, lambda g, c: (group(g), 0)[1], 0)

    def rows_done(slot):
        for k in range(TOP_K):
            pltpu.make_async_copy(ys_hbm.at[pl.ds(0, tm)], buf.at[slot, k], row_sem.at[slot]).wait()

    def tile_at(j):
        return jnp.minimum(j, n - 1)

    @pl.when(i == 0)
    def _():
        for j in range(DEPTH - 1):
            dest_copy(tile_at(j), j).start()
        for j in range(DEPTH - 1):
            dest_copy(tile_at(j), j).wait()
            issue_rows(j)
        dest_copy(tile_at(DEPTH - 1), DEPTH - 1).start()

    def step(slot):
        far = (slot + DEPTH - 1) % DEPTH
        dest_copy(tile_at(i + DEPTH - 1), far).wait()
        rows_done(slot)
        issue_rows(far, unrolled=True)
        h = h_ref[...]
        hb = h.astype(BF16)
        shared = _dot((jax.nn.silu(_dot(hb, wsg_ref[...])) * _dot(hb, wsu_ref[...])).astype(BF16), wsd_ref[...])
        total = alpha * h + shared
        for k in range(TOP_K):
            total = total + gate_ref[:, k:k + 1] * buf[slot, k]
        o_ref[...] = _layer_norm(total, g_ref[...], b_ref[...])
        dest_copy(tile_at(i + DEPTH), slot).start()

    for slot in range(DEPTH):
        pl.when(i % DEPTH == slot)(functools.partial(step, slot))

    @pl.when(i == n - 1)
    def _():
        for j in range(1, DEPTH):
            rows_done((i + j) % DEPTH)
        dest_copy(n - 1, i % DEPTH).wait()


def _combine_shared_norm(h2d, ys, dest, gate, w_sg, w_su, w_sd, ln_g, ln_b, alpha, tm):
    T, D = h2d.shape
    nt = T // tm
    ws = [w.astype(BF16) for w in (w_sg, w_su, w_sd)]
    vec = [v.reshape(1, D).astype(F32) for v in (ln_g, ln_b)]
    whole = lambda a: pl.BlockSpec(a.shape, lambda i: (0, 0))
    return pl.pallas_call(
        functools.partial(_combine_kernel, tm=tm, alpha=alpha),
        grid=(nt,),
        in_specs=[pl.BlockSpec(memory_space=pl.ANY), pl.BlockSpec(memory_space=pl.ANY),
                  pl.BlockSpec((tm, D), lambda i: (i, 0)),
                  pl.BlockSpec((tm, TOP_K), lambda i: (i, 0))] + [whole(w) for w in ws] + [whole(v) for v in vec],
        out_specs=pl.BlockSpec((tm, D), lambda i: (i, 0)),
        out_shape=jax.ShapeDtypeStruct((T, D), F32),
        scratch_shapes=[pltpu.SMEM((DEPTH * TOP_K * tm,), I32),
                        pltpu.VMEM((DEPTH, TOP_K, tm, D), F32),
                        pltpu.SemaphoreType.DMA((DEPTH,)),
                        pltpu.SemaphoreType.DMA((DEPTH,))],
        compiler_params=_cparams(("arbitrary",)),
        name="combine_shared_norm",
    )(_token_tiles(dest, tm), ys, h2d, gate.T, *ws, *vec)


PROJ_TM = 512
DA_TQ, DA_TK = 1024, 512
DSA_TQ = 256
MEM_TQ = 512
MERGE_TM = 256
ROUTER_TM = 512
DEST_TM = 2048
COMBINE_TM = 128
MOE_ROWS = 256


def _dispatch_plan(counts, n_tokens, rows):
    counts = counts.reshape(N_EXPERTS)
    padded = (counts + rows - 1) // rows * rows
    pend = jnp.cumsum(padded)
    pstart = pend - padded
    cstart = jnp.cumsum(counts) - counts
    n_blocks = (n_tokens * TOP_K + N_EXPERTS * (rows - 1) + rows - 1) // rows
    n_used = pend[-1] // rows
    blk = jnp.minimum(jnp.arange(n_blocks, dtype=I32), n_used - 1)
    hot = (blk[:, None] * rows >= pend[None, :]).astype(I32)
    blk_e = jnp.sum(hot, 1)
    first = (jnp.arange(N_EXPERTS, dtype=I32)[None, :] == blk_e[:, None]).astype(I32)
    blk_off = jnp.sum(first * (cstart - pstart)[None, :], 1) + blk * rows
    return pstart.astype(I32), blk_e, blk_off.astype(I32), n_used.astype(I32).reshape(1), n_blocks


def _moe(h1, p, l, alpha):
    T = h1.shape[0]
    eidx, gate, rank, counts = _route(h1, p['w_router'][l], p['router_bias'][l], tm=ROUTER_TM)
    pstart, blk_e, blk_off, n_used, n_blocks = _dispatch_plan(counts, T, MOE_ROWS)
    dest = _destinations(pstart, eidx, rank, tm=DEST_TM)
    tok = jnp.broadcast_to(jnp.arange(T, dtype=I32)[None, :], (TOP_K, T))
    _, sorted_tok = lax.sort((dest.reshape(-1), tok.reshape(-1)), num_keys=1)
    sorted_tok = jnp.concatenate([sorted_tok, jnp.zeros((MOE_ROWS + 2 * TOK_ALIGN,), I32)])
    ys = _routed_experts(h1, sorted_tok, blk_e, blk_off, n_used, n_blocks,
                         p['w_eg'][l], p['w_eu'][l], p['w_ed'][l], MOE_ROWS)
    return _combine_shared_norm(h1, ys, dest, gate, p['w_sg'][l], p['w_su'][l], p['w_sd'][l],
                                p['ln2_g'][l], p['ln2_b'][l], alpha, tm=COMBINE_TM)


def _layer(h2d, mem2d, p, l, depth, batch, seq, mem_len):
    alpha = (2 * depth) ** 0.25
    lam_init = 0.8 - 0.6 * math.exp(-0.3 * l)
    w_in = p['w_in'][l]
    daq, dak, dsq, ixq, kv, vk, ikk, dav, memq, ixw = _in_projection(h2d, w_in, seq, tm=min(PROJ_TM, seq))
    lam = (jnp.exp(jnp.sum(p['lq1'][l].astype(F32) * p['lk1'][l].astype(F32)))
           - jnp.exp(jnp.sum(p['lq2'][l].astype(F32) * p['lk2'][l].astype(F32))) + lam_init)
    ya = _diff_attention(daq, dak, dav, lam, p['subln_g'][l], batch, seq, lam_init, tq=min(DA_TQ, seq), tk=min(DA_TK, seq))
    yb = _sparse_attention(ixq, ixw, dsq, ikk, kv, vk, batch, seq, tq=min(DSA_TQ, seq))
    mkv = _matmul(mem2d, p['w_mem_kv'][l].astype(BF16), tm=mem_len, out_dtype=BF16)
    yc = _memory_attention(memq, mkv, batch, seq, mem_len, tq=min(MEM_TQ, seq))
    h1 = _merge_project_norm(h2d, ya, yb, yc, w_in[:, 3272:], p['w_ba'][l], p['w_bb'][l], p['w_bc'][l],
                             p['w_o'][l], p['ln1_g'][l], p['ln1_b'][l], alpha, tm=MERGE_TM)
    return _moe(h1, p, l, alpha)


def kernel(x, mem, w_in, da_lambda_q1, da_lambda_k1, da_lambda_q2, da_lambda_k2, da_subln_g, w_mem_kv, w_branch_a, w_branch_b, w_branch_c, w_out, ln1_g, ln1_b, w_router, router_bias, w_exp_gate, w_exp_up, w_exp_down, w_sh_gate, w_sh_up, w_sh_down, ln2_g, ln2_b):
    batch, seq, d = x.shape
    mem_len = mem.shape[1]
    depth = w_in.shape[0]
    p = dict(w_in=w_in, lq1=da_lambda_q1, lk1=da_lambda_k1, lq2=da_lambda_q2, lk2=da_lambda_k2,
             subln_g=da_subln_g, w_mem_kv=w_mem_kv, w_ba=w_branch_a, w_bb=w_branch_b, w_bc=w_branch_c,
             w_o=w_out, ln1_g=ln1_g, ln1_b=ln1_b, w_router=w_router, router_bias=router_bias,
             w_eg=w_exp_gate, w_eu=w_exp_up, w_ed=w_exp_down, w_sg=w_sh_gate, w_su=w_sh_up,
             w_sd=w_sh_down, ln2_g=ln2_g, ln2_b=ln2_b)
    h = x.reshape(batch * seq, d)
    mem2d = mem.reshape(batch * mem_len, d)
    for l in range(depth):
        h = _layer(h, mem2d, p, l, depth, batch, seq, mem_len)
    return h.reshape(batch, seq, d)
```

```python
import functools
import math

import jax
import jax.numpy as jnp
import numpy as np
from jax import lax
from jax.experimental import pallas as pl
from jax.experimental.pallas import tpu as pltpu

F32 = jnp.float32
BF16 = jnp.bfloat16
I32 = jnp.int32

LANES = 128
SUBLANES = 8
ROPE_THETA = 10000.0
LN_EPS = 1e-5

DA_HEADS = 4
DA_HEAD_DIM = 64
DSA_HEADS = 8
DSA_HEAD_DIM = 64
IDX_HEADS = 8
IDX_ROWS = 16
DSA_TOPK_MAX = 256
MEM_HEADS = 4
MEM_HEAD_DIM = 128

N_EXPERTS = 256
TOP_K = 8
N_GROUPS = 8
TOPK_GROUPS = 4
GROUP_SIZE = N_EXPERTS // N_GROUPS
ROUTED_SCALE = 2.5

NEG_BIG = -1e30
INT_MIN = -(2 ** 31)

VMEM_LIMIT = 56 * 1024 * 1024


def _cparams(sem):
    return pltpu.CompilerParams(dimension_semantics=sem, vmem_limit_bytes=VMEM_LIMIT)


def _dot(a, b):
    return jnp.dot(a, b, preferred_element_type=F32)


def _dot_nt(a, b):
    return lax.dot_general(a, b, (((1,), (1,)), ((), ())), preferred_element_type=F32)


def _tree(op, xs):
    xs = list(xs)
    while len(xs) > 1:
        xs = [op(xs[k], xs[k + 1]) if k + 1 < len(xs) else xs[k] for k in range(0, len(xs), 2)]
    return xs[0]


def _layer_norm(x, g, b):
    mu = jnp.mean(x, -1, keepdims=True)
    xc = x - mu
    var = jnp.mean(xc * xc, -1, keepdims=True)
    return xc * lax.rsqrt(var + LN_EPS) * g + b


def _inproj_kernel(x_ref, wr_ref, wn_ref, wixt_ref, tab_ref,
                   daq_ref, dak_ref, dsq_ref, ixq_ref, kv_ref, vk_ref, ikk_ref,
                   dav_ref, memq_ref, ixwt_ref):
    xb = x_ref[...].astype(BF16)

    def rope(t, kind):
        c = tab_ref[3 * kind]
        sa = tab_ref[3 * kind + 1]
        sb = tab_ref[3 * kind + 2]
        return t * c + pltpu.roll(t, 96, 1) * sa + pltpu.roll(t, 32, 1) * sb

    groups = ((daq_ref, 0.125), (dak_ref, 1.0), (dsq_ref, 0.125), (ixq_ref, 1.0))
    for gi, (ref, scale) in enumerate(groups):
        y = _dot(xb, wr_ref[:, gi * 512:(gi + 1) * 512])
        for c in range(4):
            r = rope(y[:, c * LANES:(c + 1) * LANES], 0)
            if scale != 1.0:
                r = r * scale
            ref[:, c * LANES:(c + 1) * LANES] = r.astype(BF16)
    y = _dot(xb, wr_ref[:, 2048:2432])
    kv_ref[...] = rope(y[:, 0:128], 1).astype(BF16)
    vk_ref[...] = rope(y[:, 128:256], 2).astype(BF16)
    ikk_ref[...] = rope(y[:, 256:384], 0).astype(BF16)
    y = _dot(xb, wn_ref[...])
    dav_ref[...] = y[:, 0:512].astype(BF16)
    memq_ref[...] = y[:, 512:1024].astype(BF16)
    ixwt_ref[...] = _dot_nt(wixt_ref[...], xb)


def _rope_tables(seq):
    dim = 64
    inv = 1.0 / (ROPE_THETA ** (jnp.arange(0, dim, 2, dtype=F32) / dim))
    ang = jnp.arange(seq, dtype=F32)[:, None] * inv[None, :]
    ang = jnp.concatenate([ang, ang], -1)
    cos, sin = jnp.cos(ang), jnp.sin(ang)
    local = jnp.arange(dim)[None, :]
    sa = jnp.where(local < dim // 2, -sin, 0.0)
    sb = jnp.where(local >= dim // 2, sin, 0.0)
    one, zero = jnp.ones_like(cos), jnp.zeros_like(cos)
    cat = lambda a, b: jnp.concatenate([a, b], -1)
    return jnp.stack([cat(cos, cos), cat(sa, sa), cat(sb, sb),
                      cat(cos, one), cat(sa, zero), cat(sb, zero),
                      cat(one, cos), cat(zero, sa), cat(zero, sb)], 0)


def _in_projection(x2d, w_in, seq, tm):
    T, D = x2d.shape
    o = np.cumsum([0, 512, 512, 512, 512, 64, 64, 512, 64, 8, 512]).tolist()
    col = lambda i: w_in[:, o[i]:o[i + 1]]
    da_q, da_k, da_v, ds_q, ds_k, ds_v, ix_q, ix_k, ix_w, mem_q = [col(i) for i in range(10)]
    ds_q = ds_q.reshape(D, DSA_HEADS // 2, 2, DSA_HEAD_DIM)[:, :, ::-1, :].reshape(D, 512)
    w_rope = jnp.concatenate([da_q, da_k, ds_q, ix_q, ds_k, ds_v, ds_v, ds_k, ix_k, ix_k], 1).astype(BF16)
    w_plain = jnp.concatenate([da_v, mem_q], 1).astype(BF16)
    w_ixt = jnp.concatenate([ix_w.T, jnp.zeros((IDX_ROWS - IDX_HEADS, D), w_in.dtype)], 0).astype(BF16)
    tabs = _rope_tables(seq)
    nseq = seq // tm
    bf = lambda n: jax.ShapeDtypeStruct((T, n), BF16)
    row = lambda n: pl.BlockSpec((tm, n), lambda i: (i, 0))
    return pl.pallas_call(
        _inproj_kernel,
        grid=(T // tm,),
        in_specs=[row(D),
                  pl.BlockSpec(w_rope.shape, lambda i: (0, 0)),
                  pl.BlockSpec(w_plain.shape, lambda i: (0, 0)),
                  pl.BlockSpec(w_ixt.shape, lambda i: (0, 0)),
                  pl.BlockSpec((9, tm, LANES), lambda i: (0, i % nseq, 0))],
        out_specs=[row(512), row(512), row(512), row(512), row(128), row(128), row(128),
                   row(512), row(512), pl.BlockSpec((IDX_ROWS, tm), lambda i: (0, i))],
        out_shape=[bf(512), bf(512), bf(512), bf(512), bf(128), bf(128), bf(128),
                   bf(512), bf(512), jax.ShapeDtypeStruct((IDX_ROWS, T), F32)],
        compiler_params=_cparams(("parallel",)),
        name="in_projection",
    )(x2d, w_rope, w_plain, w_ixt, tabs)


def _diffattn_kernel(lam_ref, q_ref, k_ref, v_ref, g_ref, o_ref, m_s, l_s, a_s, *, tq, tk, lam_init):
    i = pl.program_id(2)
    lam = lam_ref[0]
    q = q_ref[...]
    lane = lax.broadcasted_iota(I32, q.shape, 1)
    zero = jnp.zeros_like(q)
    qs = (jnp.where(lane < DA_HEAD_DIM, q, zero), jnp.where(lane >= DA_HEAD_DIM, q, zero))
    m_s[...] = jnp.full(m_s.shape, -jnp.inf, F32)
    l_s[...] = jnp.zeros(l_s.shape, F32)
    a_s[...] = jnp.zeros(a_s.shape, F32)

    def step(j, masked, r0=0):
        off = pl.multiple_of(j * tk, tk)
        kj = k_ref[pl.ds(off, tk), :]
        vj = v_ref[pl.ds(off, tk), :]
        for mp in range(2):
            s = _dot_nt(qs[mp][r0:], kj)
            if masked:
                row = lax.broadcasted_iota(I32, (tq - r0, tk), 0) + r0
                col = lax.broadcasted_iota(I32, (tq - r0, tk), 1)
                s = jnp.where(j * tk + col <= i * tq + row, s, -jnp.inf)
            tiles = [s[:, c * LANES:(c + 1) * LANES] for c in range(tk // LANES)]
            m_old = m_s[mp, r0:]
            m_blk = jnp.max(_tree(jnp.maximum, tiles), -1, keepdims=True)
            m_new = jnp.maximum(m_old, jnp.broadcast_to(m_blk, (tq - r0, LANES)))
            ps = [jnp.exp(t - m_new) for t in tiles]
            al = jnp.exp(m_old - m_new)
            l_s[mp, r0:] = al * l_s[mp, r0:] + _tree(jnp.add, ps)
            a_s[mp, r0:] = al * a_s[mp, r0:] + _dot(jnp.concatenate(ps, 1).astype(BF16), vj)
            m_s[mp, r0:] = m_new
        return 0

    per_q = tq // tk

    def fully_visible(t, _):
        for jj in range(per_q):
            step(t * per_q + jj, False)
        return 0

    lax.fori_loop(0, i, fully_visible, 0)
    for jj in range(per_q):
        step(i * per_q + jj, True, jj * tk)
    o1 = a_s[0] / jnp.sum(l_s[0], -1, keepdims=True)
    o2 = a_s[1] / jnp.sum(l_s[1], -1, keepdims=True)
    o = o1 - lam * o2
    y = o * lax.rsqrt(jnp.mean(o * o, -1, keepdims=True) + LN_EPS) * g_ref[...] * (1.0 - lam_init)
    o_ref[...] = y.astype(BF16)


def _diff_attention(daq, dak, dav, lam, subln_g, batch, seq, lam_init, tq, tk):
    T = batch * seq
    nq = seq // tq
    state = pltpu.VMEM((2, tq, LANES), F32)
    return pl.pallas_call(
        functools.partial(_diffattn_kernel, tq=tq, tk=tk, lam_init=lam_init),
        grid=(batch, DA_HEADS, nq),
        in_specs=[pl.BlockSpec(memory_space=pltpu.SMEM),
                  pl.BlockSpec((tq, LANES), lambda b, h, i: (b * nq + i, h)),
                  pl.BlockSpec((seq, LANES), lambda b, h, i: (b, h)),
                  pl.BlockSpec((seq, LANES), lambda b, h, i: (b, h)),
                  pl.BlockSpec((1, LANES), lambda b, h, i: (0, 0))],
        out_specs=pl.BlockSpec((tq, LANES), lambda b, h, i: (b * nq + i, h)),
        out_shape=jax.ShapeDtypeStruct((T, DA_HEADS * LANES), BF16),
        scratch_shapes=[state, state, state],
        compiler_params=_cparams(("parallel", "parallel", "arbitrary")),
        name="diff_attention",
    )(lam.reshape(1), daq, dak, dav, subln_g.reshape(1, LANES).astype(F32))


def _dsa_kernel(ixq_ref, ixwt_ref, dsq_ref, ikk_ref, kv_ref, vk_ref, o_ref,
                qi_s, qd_s, sc_s, g_s, mx_s, l_s, acc_s, *, tq, topk):
    i = pl.program_id(1)
    nblk = i + 1
    tk = tq
    nslab = tk // 8
    lane = lax.broadcasted_iota(I32, (tq, LANES), 1)
    lower = lane < DSA_HEAD_DIM
    npair = DSA_HEADS // 2
    for h in range(IDX_HEADS):
        sl = slice((h // 2) * LANES, (h // 2 + 1) * LANES)
        keep = lower if h % 2 == 0 else jnp.logical_not(lower)
        blk = ixq_ref[:, sl]
        qi_s[h * tq:(h + 1) * tq, :] = jnp.where(keep, blk, jnp.zeros_like(blk))
        blk = dsq_ref[:, sl]
        qd_s[h % 2, (h // 2) * tq:(h // 2 + 1) * tq, :] = jnp.where(keep, blk, jnp.zeros_like(blk))
    krow = lax.broadcasted_iota(I32, (tk, tq), 0)
    qcol = lax.broadcasted_iota(I32, (tk, tq), 1)
    sub8 = lax.broadcasted_iota(I32, (8, tq), 0)

    def score_chunk(j, _):
        off = pl.multiple_of(j * tk, tk)
        ik = ikk_ref[pl.ds(off, tk), :]
        logits = _dot_nt(ik, qi_s[...])
        sc = jnp.zeros((tk, tq), F32)
        for h in range(IDX_HEADS):
            sc = sc + ixwt_ref[h:h + 1, :] * jnp.maximum(logits[:, h * tq:(h + 1) * tq], 0.0)
        sc = jnp.where(j * tk + krow <= i * tq + qcol, sc, -jnp.inf)
        sc_s[j] = sc
        g_s[j] = sc.astype(BF16)
        return 0

    def over_chunks(body, width=4):
        def group(jg, _):
            for u in range(width):
                body(width * jg + u)
            return 0
        lax.fori_loop(0, nblk // width, group, 0)
        done = nblk // width * width

        def pair(jp, _):
            body(done + 2 * jp)
            body(done + 2 * jp + 1)
            return 0
        lax.fori_loop(0, (nblk - done) // 2, pair, 0)
        lax.fori_loop(done + (nblk - done) // 2 * 2, nblk, lambda j, c: (body(j), 0)[1], 0)

    over_chunks(lambda j: score_chunk(j, 0))

    n_acc = 4

    def count(pred):
        def body(j, accs):
            accs = list(accs)
            for r in range(nslab):
                hit = pred(sc_s[j, r * 8:(r + 1) * 8, :], j * tk + r * 8 + sub8)
                accs[r % n_acc] = accs[r % n_acc] + hit.astype(F32)
            return tuple(accs)
        accs = lax.fori_loop(0, nblk, body, tuple(jnp.zeros((8, tq), F32) for _ in range(n_acc)))
        return jnp.sum(_tree(jnp.add, accs), 0, keepdims=True)

    def rows8(v):
        return jnp.broadcast_to(v, (8, tq))

    def as_score(key):
        return pltpu.bitcast(key ^ ((key >> 31) & 0x7FFFFFFF), F32)

    def count_coarse(cand):
        cb = jnp.broadcast_to(cand, (16, tq))
        one, nil = jnp.ones((16, tq), BF16), jnp.zeros((16, tq), BF16)

        def body(j, accs):
            accs = list(accs)
            for r in range(tk // 16):
                hit = g_s[j, r * 16:(r + 1) * 16, :] >= cb
                accs[r % n_acc] = accs[r % n_acc] + jnp.where(hit, one, nil)
            return tuple(accs)
        accs = lax.fori_loop(0, nblk, body, tuple(nil for _ in range(n_acc)))
        return jnp.sum(_tree(jnp.add, [a.astype(F32) for a in accs]), 0, keepdims=True)

    def as_coarse(key16):
        bits16 = key16 ^ ((key16 >> 15) & 0x7FFF)
        return pltpu.bitcast(bits16 << 16, F32).astype(BF16)

    kf = float(topk)
    zero_i = jnp.zeros((1, tq), I32)

    def coarse_bit(b, hi):
        cand = jnp.where(b == 0, zero_i, hi | (jnp.int32(1) << (15 - b)))
        c = count_coarse(as_coarse(cand))
        return jnp.where(c >= kf, cand, hi)

    hi16 = lax.fori_loop(0, 16, coarse_bit, jnp.full((1, tq), -(2 ** 15), I32))
    found = hi16 != -(2 ** 15)
    key_t = jnp.where(found, (hi16 << 16) + jnp.where(hi16 < 0, 0xFFFF, 0), 0)
    span = (1 << 16) + (1 << 15) + 2

    def bisect(_, bounds):
        lo, width, c_lo = bounds
        step = (width + 1) >> 1
        cand = lo + step
        cb = rows8(as_score(cand))
        c = count(lambda s, kpos: s >= cb)
        ok = c >= kf
        return jnp.where(ok, cand, lo), jnp.where(ok, width - step, step - 1), jnp.where(ok, c, c_lo)

    lo, _, c_lo = lax.fori_loop(0, 17, bisect, (key_t - ((1 << 15) + 1), jnp.full((1, tq), span, I32),
                                              jnp.full((1, tq), kf, F32)))
    tau = jnp.where(found, lo, INT_MIN)
    tau_f = jnp.where(tau == INT_MIN, -jnp.inf, as_score(tau))
    tau8 = rows8(tau_f)
    c_ge = jnp.where(found, c_lo, kf)

    def tie_cut():
        need = kf - count(lambda s, kpos: s > tau8)

        def cut_bit(b, cut):
            cand = cut | (jnp.int32(1) << (12 - b))
            cb = rows8(cand)
            c = count(lambda s, kpos: (s == tau8) & (kpos < cb))
            return jnp.where(c <= need, cand, cut)

        return lax.fori_loop(0, 13, cut_bit, zero_i)

    cut = lax.cond(jnp.max(c_ge) > kf, tie_cut, lambda: jnp.full((1, tq), 2 ** 13, I32))

    mx_s[...] = jnp.full(mx_s.shape, NEG_BIG, F32)
    l_s[...] = jnp.zeros(l_s.shape, F32)
    acc_s[...] = jnp.zeros(acc_s.shape, F32)
    nlt = tk // LANES

    def lane_tiles(a):
        return [a[:, c * LANES:(c + 1) * LANES] for c in range(nlt)]

    def bcast(v):
        return jnp.broadcast_to(v, (v.shape[0], LANES))

    def row_max(j, _):
        off = pl.multiple_of(j * tk, tk)
        sc = sc_s[j]
        kpos = j * tk + krow
        sel = (sc > tau_f) | ((sc == tau_f) & (kpos < cut))
        sel = sel & (kpos <= i * tq + qcol)
        bias = jnp.where(sel, 0.0, NEG_BIG).T
        sc_s[j] = bias
        kvj = kv_ref[pl.ds(off, tk), :]
        vkj = vk_ref[pl.ds(off, tk), :]
        for par, kk in enumerate((kvj, vkj)):
            s = (_dot_nt(qd_s[par], kk).reshape(npair, tq, tk) + bias[None]).reshape(npair * tq, tk)
            mx_s[par] = functools.reduce(jnp.maximum, lane_tiles(s), mx_s[par])
        return 0

    over_chunks(lambda j: row_max(j, 0))
    for par in range(2):
        mx_s[par] = bcast(jnp.max(mx_s[par], -1, keepdims=True))

    def attend(j, _):
        off = pl.multiple_of(j * tk, tk)
        bias = sc_s[j]
        kvj = kv_ref[pl.ds(off, tk), :]
        vkj = vk_ref[pl.ds(off, tk), :]
        for par, kk in enumerate((kvj, vkj)):
            s = (_dot_nt(qd_s[par], kk).reshape(npair, tq, tk) + bias[None]).reshape(npair * tq, tk)
            m = mx_s[par]
            ps = [jnp.exp(t - m) for t in lane_tiles(s)]
            l_s[par] = functools.reduce(jnp.add, ps, l_s[par])
            acc_s[par] = acc_s[par] + _dot(jnp.concatenate(ps, 1).astype(BF16), kk)
        return 0

    over_chunks(lambda j: attend(j, 0))

    def out(h):
        rows = slice((h // 2) * tq, (h // 2 + 1) * tq)
        return acc_s[h % 2, rows, :] / jnp.sum(l_s[h % 2, rows, :], -1, keepdims=True)

    for pr in range(DSA_HEADS // 2):
        o_ref[:, pr * LANES:(pr + 1) * LANES] = jnp.where(lower, out(2 * pr + 1), out(2 * pr)).astype(BF16)


def _sparse_attention(ixq, ixwt, dsq, ikk, kv, vk, batch, seq, tq):
    T = batch * seq
    nq = seq // tq
    topk = min(DSA_TOPK_MAX, seq // 4)
    tile = lambda n: pl.BlockSpec((tq, n), lambda b, i: (b * nq + i, 0))
    full = pl.BlockSpec((seq, LANES), lambda b, i: (b, 0))
    return pl.pallas_call(
        functools.partial(_dsa_kernel, tq=tq, topk=topk),
        grid=(batch, nq),
        in_specs=[tile(512), pl.BlockSpec((ixwt.shape[0], tq), lambda b, i: (0, b * nq + i)), tile(512),
                  full, full, full],
        out_specs=tile(512),
        out_shape=jax.ShapeDtypeStruct((T, 512), BF16),
        scratch_shapes=[pltpu.VMEM((IDX_HEADS * tq, LANES), BF16),
                        pltpu.VMEM((2, DSA_HEADS // 2 * tq, LANES), BF16),
                        pltpu.VMEM((nq, tq, tq), F32),
                        pltpu.VMEM((nq, tq, tq), BF16),
                        pltpu.VMEM((2, DSA_HEADS // 2 * tq, LANES), F32),
                        pltpu.VMEM((2, DSA_HEADS // 2 * tq, LANES), F32),
                        pltpu.VMEM((2, DSA_HEADS // 2 * tq, LANES), F32)],
        compiler_params=_cparams(("parallel", "arbitrary")),
        name="sparse_attention",
    )(ixq, ixwt, dsq, ikk, kv, vk)


def _matmul_kernel(x_ref, w_ref, o_ref):
    o_ref[...] = _dot(x_ref[...].astype(BF16), w_ref[...]).astype(o_ref.dtype)


def _matmul(x, w, tm, out_dtype):
    M, K = x.shape
    N = w.shape[1]
    return pl.pallas_call(
        _matmul_kernel,
        grid=(M // tm,),
        in_specs=[pl.BlockSpec((tm, K), lambda i: (i, 0)), pl.BlockSpec((K, N), lambda i: (0, 0))],
        out_specs=pl.BlockSpec((tm, N), lambda i: (i, 0)),
        out_shape=jax.ShapeDtypeStruct((M, N), out_dtype),
        compiler_params=_cparams(("parallel",)),
        name="matmul",
    )(x, w)


def _memattn_kernel(q_ref, kv_ref, o_ref):
    scale = MEM_HEAD_DIM ** -0.5
    width = MEM_HEADS * MEM_HEAD_DIM
    for h in range(MEM_HEADS):
        sl = slice(h * MEM_HEAD_DIM, (h + 1) * MEM_HEAD_DIM)
        k = kv_ref[:, sl]
        v = kv_ref[:, width + h * MEM_HEAD_DIM:width + (h + 1) * MEM_HEAD_DIM]
        s = _dot_nt(q_ref[:, sl], k) * scale
        p = jnp.exp(s - jnp.max(s, -1, keepdims=True))
        o = _dot(p.astype(BF16), v) / jnp.sum(p, -1, keepdims=True)
        o_ref[:, sl] = o.astype(BF16)


def _memory_attention(memq, mkv, batch, seq, mem_len, tq):
    T = batch * seq
    nq = seq // tq
    return pl.pallas_call(
        _memattn_kernel,
        grid=(batch, nq),
        in_specs=[pl.BlockSpec((tq, 512), lambda b, i: (b * nq + i, 0)),
                  pl.BlockSpec((mem_len, 1024), lambda b, i: (b, 0))],
        out_specs=pl.BlockSpec((tq, 512), lambda b, i: (b * nq + i, 0)),
        out_shape=jax.ShapeDtypeStruct((T, 512), BF16),
        compiler_params=_cparams(("parallel", "parallel")),
        name="memory_attention",
    )(memq, mkv)


def _merge_kernel(x_ref, ya_ref, yb_ref, yc_ref, wg_ref, wa_ref, wb_ref, wc_ref, wo_ref,
                  g_ref, b_ref, h_ref, *, alpha):
    x = x_ref[...]
    xb = x.astype(BF16)
    d = x.shape[1]
    merged = None
    for br, (y_ref, w_ref) in enumerate(((ya_ref, wa_ref), (yb_ref, wb_ref), (yc_ref, wc_ref))):
        gate = jax.nn.sigmoid(_dot(xb, wg_ref[:, br * d:(br + 1) * d]))
        term = gate * _dot(y_ref[...], w_ref[...])
        merged = term if merged is None else merged + term
    mix = _dot(merged.astype(BF16), wo_ref[...])
    h_ref[...] = _layer_norm(alpha * x + mix, g_ref[...], b_ref[...])


def _merge_project_norm(x2d, ya, yb, yc, w_gates, w_ba, w_bb, w_bc, w_o, ln_g, ln_b, alpha, tm):
    T, D = x2d.shape
    row = lambda n: pl.BlockSpec((tm, n), lambda i: (i, 0))
    whole = lambda a: pl.BlockSpec(a.shape, lambda i: (0, 0))
    ws = [w.astype(BF16) for w in (w_gates, w_ba, w_bb, w_bc, w_o)]
    vec = [v.reshape(1, D).astype(F32) for v in (ln_g, ln_b)]
    return pl.pallas_call(
        functools.partial(_merge_kernel, alpha=alpha),
        grid=(T // tm,),
        in_specs=[row(D), row(512), row(512), row(512)] + [whole(w) for w in ws] + [whole(v) for v in vec],
        out_specs=row(D),
        out_shape=jax.ShapeDtypeStruct((T, D), F32),
        compiler_params=_cparams(("parallel",)),
        name="merge_project_norm",
    )(x2d, ya, yb, yc, *ws, *vec)


def _router_kernel(h_ref, wr_ref, bias_ref, eidx_ref, gate_ref, rank_ref, cnt_ref, carry_s, *, tm):
    @pl.when(pl.program_id(0) == 0)
    def _():
        carry_s[...] = jnp.zeros(carry_s.shape, F32)

    neg = -jnp.inf
    logits = _dot_nt(wr_ref[...], h_ref[...].astype(BF16))
    scores = jax.nn.sigmoid(logits)
    biased = scores + bias_ref[...]
    erow = lax.broadcasted_iota(I32, (N_EXPERTS, tm), 0)
    big = jnp.int32(2 ** 30)

    def top1(vals, ids):
        mx = jnp.max(vals, 0, keepdims=True)
        am = jnp.min(jnp.where(vals == mx, ids, big), 0, keepdims=True)
        return mx, am

    gs = []
    ids = lax.broadcasted_iota(I32, (GROUP_SIZE, tm), 0)
    for g in range(N_GROUPS):
        v = biased[g * GROUP_SIZE:(g + 1) * GROUP_SIZE]
        m1, a1 = top1(v, ids)
        m2, _ = top1(jnp.where(ids == a1, neg, v), ids)
        gs.append(m1 + m2)
    gscore = jnp.concatenate(gs, 0)
    grow = lax.broadcasted_iota(I32, (N_GROUPS, tm), 0)
    gsel = jnp.zeros((N_GROUPS, tm), F32)
    for _ in range(TOPK_GROUPS):
        _, ga = top1(gscore, grow)
        hit = grow == ga
        gsel = jnp.where(hit, 1.0, gsel)
        gscore = jnp.where(hit, neg, gscore)
    masked = jnp.concatenate(
        [jnp.where(gsel[g:g + 1] > 0.0, biased[g * GROUP_SIZE:(g + 1) * GROUP_SIZE], neg)
         for g in range(N_GROUPS)], 0)

    eids, ws = [], []
    hot = jnp.zeros((N_EXPERTS, tm), F32)
    for _ in range(TOP_K):
        _, ea = top1(masked, erow)
        hit = erow == ea
        eids.append(ea)
        ws.append(jnp.sum(jnp.where(hit, scores, 0.0), 0, keepdims=True))
        hot = jnp.where(hit, 1.0, hot)
        masked = jnp.where(hit, neg, masked)
    w = jnp.concatenate(ws, 0)
    eidx_ref[...] = jnp.concatenate(eids, 0)
    gate_ref[...] = w / jnp.sum(w, 0, keepdims=True) * ROUTED_SCALE

    r_i =lax.broadcasted_iota(I32, (tm, tm), 0)
    c_i = lax.broadcasted_iota(I32, (tm, tm), 1)
    before = jnp.where(r_i < c_i, 1.0, 0.0).astype(BF16)
    pos = carry_s[...] + _dot(hot.astype(BF16), before)
    rank_ref[...] = jnp.concatenate(
        [jnp.sum(jnp.where(erow == e, pos, 0.0), 0, keepdims=True) for e in eids], 0).astype(I32)
    carry_s[...] = carry_s[...] + jnp.sum(hot, 1, keepdims=True)
    cnt_ref[...] = carry_s[...].astype(I32)


def _route(h2d, w_router, router_bias, tm):
    T, D = h2d.shape
    out = lambda dt: jax.ShapeDtypeStruct((TOP_K, T), dt)
    blk = pl.BlockSpec((TOP_K, tm), lambda i: (0, i))
    return pl.pallas_call(
        functools.partial(_router_kernel, tm=tm),
        grid=(T // tm,),
        in_specs=[pl.BlockSpec((tm, D), lambda i: (i, 0)),
                  pl.BlockSpec((N_EXPERTS, D), lambda i: (0, 0)),
                  pl.BlockSpec((N_EXPERTS, 1), lambda i: (0, 0))],
        out_specs=[blk, blk, blk, pl.BlockSpec((N_EXPERTS, 1), lambda i: (0, 0))],
        out_shape=[out(I32), out(F32), out(I32), jax.ShapeDtypeStruct((N_EXPERTS, 1), I32)],
        scratch_shapes=[pltpu.VMEM((N_EXPERTS, 1), F32)],
        compiler_params=_cparams(("arbitrary",)),
        name="router",
    )(h2d, w_router.T.astype(BF16), router_bias.reshape(N_EXPERTS, 1).astype(F32))


def _dest_kernel(pstart_ref, eidx_ref, rank_ref, dest_ref):
    eidx = eidx_ref[...]

    def body(e, acc):
        return acc + jnp.where(eidx == e, pstart_ref[e], 0)

    dest_ref[...] = lax.fori_loop(0, N_EXPERTS, body, rank_ref[...])


def _destinations(pstart, eidx, rank, tm):
    T = eidx.shape[1]
    blk = lambda: pl.BlockSpec((TOP_K, tm), lambda i, ps: (0, i))
    return pl.pallas_call(
        _dest_kernel,
        grid_spec=pltpu.PrefetchScalarGridSpec(
            num_scalar_prefetch=1, grid=(T // tm,), in_specs=[blk(), blk()], out_specs=blk()),
        out_shape=jax.ShapeDtypeStruct((TOP_K, T), I32),
        compiler_params=_cparams(("parallel",)),
        name="destinations",
    )(pstart, eidx, rank)


def _token_tiles(a, tm):
    k, T = a.shape
    return a.reshape(k, T // tm, tm).transpose(1, 0, 2).reshape(T // tm, k * tm)


TOK_ALIGN = 128
DEPTH = 3


def _experts_kernel(blk_e_ref, blk_off_ref, used_ref, tok_hbm, h_hbm, wg_ref, wu_ref, wd_ref, y_ref,
                    tok_s, xbuf, wgb, wub, wdb, tok_sem, row_sem, *, rows):
    i = pl.program_id(0)
    used = used_ref[0]
    win = rows + TOK_ALIGN

    def window(blk):
        return pl.multiple_of(blk_off_ref[blk] // TOK_ALIGN * TOK_ALIGN, TOK_ALIGN)

    def tok_copy(blk, slot):
        return pltpu.make_async_copy(tok_hbm.at[pl.ds(window(blk), win)], tok_s.at[pl.ds(slot * win, win)],
                                     tok_sem.at[slot])

    def issue_rows(blk, slot, unrolled):
        base = slot * win + blk_off_ref[blk] - window(blk)

        def group(g):
            for rr in range(SUBLANES):
                r = g * SUBLANES + rr
                t = tok_s[base + r]
                pltpu.make_async_copy(h_hbm.at[pl.ds(t, 1)], xbuf.at[slot, pl.ds(r, 1)], row_sem.at[slot]).start()
        if unrolled:
            for g in range(rows // SUBLANES):
                group(g)
        else:
            lax.fori_loop(0, rows // SUBLANES, lambda g, c: (group(g), 0)[1], 0)

    def rows_done(slot):
        pltpu.make_async_copy(h_hbm.at[pl.ds(0, rows)], xbuf.at[slot], row_sem.at[slot]).wait()

    def blk_at(j):
        return jnp.minimum(j, used - 1)

    @pl.when(i == 0)
    def _():
        for j in range(DEPTH - 1):
            tok_copy(blk_at(j), j).start()
        for j in range(DEPTH - 1):
            tok_copy(blk_at(j), j).wait()
            issue_rows(blk_at(j), j, False)
        tok_copy(blk_at(DEPTH - 1), DEPTH - 1).start()

    e = blk_e_ref[i]
    prev = blk_e_ref[jnp.maximum(i - 1, 0)]

    @pl.when((i == 0) | (e != prev))
    def _():
        wgb[...] = wg_ref[0].astype(BF16)
        wub[...] = wu_ref[0].astype(BF16)
        wdb[...] = wd_ref[0].astype(BF16)

    def step(slot):
        far = (slot + DEPTH - 1) % DEPTH
        ahead = blk_at(i + DEPTH - 1)
        tok_copy(ahead, far).wait()
        rows_done(slot)
        issue_rows(ahead, far, True)
        x = xbuf[slot].astype(BF16)
        hidden = jax.nn.silu(_dot(x, wgb[...])) * _dot(x, wub[...])
        y_ref[...] = _dot(hidden.astype(BF16), wdb[...])
        tok_copy(blk_at(i + DEPTH), slot).start()

    for slot in range(DEPTH):
        pl.when((i < used) & (i % DEPTH == slot))(functools.partial(step, slot))

    @pl.when(i == used - 1)
    def _():
        for j in range(1, DEPTH):
            rows_done((i + j) % DEPTH)
        tok_copy(used - 1, i % DEPTH).wait()

    @pl.when(i >= used)
    def _():
        y_ref[...] = jnp.zeros(y_ref.shape, F32)


def _routed_experts(h2d, sorted_tok, blk_e, blk_off, n_used, n_blocks, w_eg, w_eu, w_ed, rows):
    T, D = h2d.shape
    ff = w_eg.shape[-1]
    return pl.pallas_call(
        functools.partial(_experts_kernel, rows=rows),
        grid_spec=pltpu.PrefetchScalarGridSpec(
            num_scalar_prefetch=3,
            grid=(n_blocks,),
            in_specs=[pl.BlockSpec(memory_space=pl.ANY),
                      pl.BlockSpec(memory_space=pl.ANY),
                      pl.BlockSpec((1, D, ff), lambda i, be, bo, nu: (be[i], 0, 0)),
                      pl.BlockSpec((1, D, ff), lambda i, be, bo, nu: (be[i], 0, 0)),
                      pl.BlockSpec((1, ff, D), lambda i, be, bo, nu: (be[i], 0, 0))],
            out_specs=pl.BlockSpec((rows, D), lambda i, be, bo, nu: (i, 0)),
            scratch_shapes=[pltpu.SMEM((DEPTH * (rows + TOK_ALIGN),), I32),
                            pltpu.VMEM((DEPTH, rows, D), F32),
                            pltpu.VMEM((D, ff), BF16),
                            pltpu.VMEM((D, ff), BF16),
                            pltpu.VMEM((ff, D), BF16),
                            pltpu.SemaphoreType.DMA((DEPTH,)),
                            pltpu.SemaphoreType.DMA((DEPTH,))]),
        out_shape=jax.ShapeDtypeStruct((n_blocks * rows, D), F32),
        compiler_params=_cparams(("arbitrary",)),
        name="routed_experts",
    )(blk_e, blk_off, n_used, sorted_tok, h2d, w_eg, w_eu, w_ed)


def _combine_kernel(dest_hbm, ys_hbm, h_ref, gate_ref, wsg_ref, wsu_ref, wsd_ref, g_ref, b_ref, o_ref,
                    dest_s, buf, dest_sem, row_sem, *, tm, alpha):
    i = pl.program_id(0)
    n = pl.num_programs(0)

    def dest_copy(blk, slot):
        return pltpu.make_async_copy(dest_hbm.at[blk], dest_s.at[pl.ds(slot * (TOP_K * tm), TOP_K * tm)],
                                     dest_sem.at[slot])

    def issue_rows(slot, unrolled=False):
        base = slot * (TOP_K * tm)

        def group(g):
            for rr in range(SUBLANES):
                r = g * SUBLANES + rr
                for k in range(TOP_K):
                    d = dest_s[base + k * tm + r]
                    pltpu.make_async_copy(ys_hbm.at[pl.ds(d, 1)], buf.at[slot, k, pl.ds(r, 1)],
                                          row_sem.at[slot]).start()
        if unrolled:
            for g in range(tm // SUBLANES):
                group(g)
        else:
            lax.fori_loop(0, tm // SUBLANES, lambda g, c: (group(g), 0)[1], 0)

    def rows_done(slot):
        for k in range(TOP_K):
            pltpu.make_async_copy(ys_hbm.at[pl.ds(0, tm)], buf.at[slot, k], row_sem.at[slot]).wait()

    def tile_at(j):
        return jnp.minimum(j, n - 1)

    @pl.when(i == 0)
    def _():
        for j in range(DEPTH - 1):
            dest_copy(tile_at(j), j).start()
        for j in range(DEPTH - 1):
            dest_copy(tile_at(j), j).wait()
            issue_rows(j)
        dest_copy(tile_at(DEPTH - 1), DEPTH - 1).start()

    def step(slot):
        far = (slot + DEPTH - 1) % DEPTH
        dest_copy(tile_at(i + DEPTH - 1), far).wait()
        rows_done(slot)
        issue_rows(far, unrolled=True)
        h = h_ref[...]
        hb = h.astype(BF16)
        shared = _dot((jax.nn.silu(_dot(hb, wsg_ref[...])) * _dot(hb, wsu_ref[...])).astype(BF16), wsd_ref[...])
        total = alpha * h + shared
        for k in range(TOP_K):
            total = total + gate_ref[:, k:k + 1] * buf[slot, k]
        o_ref[...] = _layer_norm(total, g_ref[...], b_ref[...])
        dest_copy(tile_at(i + DEPTH), slot).start()

    for slot in range(DEPTH):
        pl.when(i % DEPTH == slot)(functools.partial(step, slot))

    @pl.when(i == n - 1)
    def _():
        for j in range(1, DEPTH):
            rows_done((i + j) % DEPTH)
        dest_copy(n - 1, i % DEPTH).wait()


def _combine_shared_norm(h2d, ys, dest, gate, w_sg, w_su, w_sd, ln_g, ln_b, alpha, tm):
    T, D = h2d.shape
    nt = T // tm
    ws = [w.astype(BF16) for w in (w_sg, w_su, w_sd)]
    vec = [v.reshape(1, D).astype(F32) for v in (ln_g, ln_b)]
    whole = lambda a: pl.BlockSpec(a.shape, lambda i: (0, 0))
    return pl.pallas_call(
        functools.partial(_combine_kernel, tm=tm, alpha=alpha),
        grid=(nt,),
        in_specs=[pl.BlockSpec(memory_space=pl.ANY), pl.BlockSpec(memory_space=pl.ANY),
                  pl.BlockSpec((tm, D), lambda i: (i, 0)),
                  pl.BlockSpec((tm, TOP_K), lambda i: (i, 0))] + [whole(w) for w in ws] + [whole(v) for v in vec],
        out_specs=pl.BlockSpec((tm, D), lambda i: (i, 0)),
        out_shape=jax.ShapeDtypeStruct((T, D), F32),
        scratch_shapes=[pltpu.SMEM((DEPTH * TOP_K * tm,), I32),
                        pltpu.VMEM((DEPTH, TOP_K, tm, D), F32),
                        pltpu.SemaphoreType.DMA((DEPTH,)),
                        pltpu.SemaphoreType.DMA((DEPTH,))],
        compiler_params=_cparams(("arbitrary",)),
        name="combine_shared_norm",
    )(_token_tiles(dest, tm), ys, h2d, gate.T, *ws, *vec)


PROJ_TM = 512
DA_TQ, DA_TK = 1024, 512
DSA_TQ = 256
MEM_TQ = 512
MERGE_TM = 256
ROUTER_TM = 512
DEST_TM = 2048
COMBINE_TM = 128
MOE_ROWS = 256


def _dispatch_plan(counts, n_tokens, rows):
    counts = counts.reshape(N_EXPERTS)
    padded = (counts + rows - 1) // rows * rows
    pend = jnp.cumsum(padded)
    pstart = pend - padded
    cstart = jnp.cumsum(counts) - counts
    n_blocks = (n_tokens * TOP_K + N_EXPERTS * (rows - 1) + rows - 1) // rows
    n_used = pend[-1] // rows
    blk = jnp.minimum(jnp.arange(n_blocks, dtype=I32), n_used - 1)
    hot = (blk[:, None] * rows >= pend[None, :]).astype(I32)
    blk_e = jnp.sum(hot, 1)
    first = (jnp.arange(N_EXPERTS, dtype=I32)[None, :] == blk_e[:, None]).astype(I32)
    blk_off = jnp.sum(first * (cstart - pstart)[None, :], 1) + blk * rows
    return pstart.astype(I32), blk_e, blk_off.astype(I32), n_used.astype(I32).reshape(1), n_blocks


def _moe(h1, p, l, alpha):
    T = h1.shape[0]
    eidx, gate, rank, counts = _route(h1, p['w_router'][l], p['router_bias'][l], tm=ROUTER_TM)
    pstart, blk_e, blk_off, n_used, n_blocks = _dispatch_plan(counts, T, MOE_ROWS)
    dest = _destinations(pstart, eidx, rank, tm=DEST_TM)
    tok = jnp.broadcast_to(jnp.arange(T, dtype=I32)[None, :], (TOP_K, T))
    _, sorted_tok = lax.sort((dest.reshape(-1), tok.reshape(-1)), num_keys=1)
    sorted_tok = jnp.concatenate([sorted_tok, jnp.zeros((MOE_ROWS + 2 * TOK_ALIGN,), I32)])
    ys = _routed_experts(h1, sorted_tok, blk_e, blk_off, n_used, n_blocks,
                         p['w_eg'][l], p['w_eu'][l], p['w_ed'][l], MOE_ROWS)
    return _combine_shared_norm(h1, ys, dest, gate, p['w_sg'][l], p['w_su'][l], p['w_sd'][l],
                                p['ln2_g'][l], p['ln2_b'][l], alpha, tm=COMBINE_TM)


def _layer(h2d, mem2d, p, l, depth, batch, seq, mem_len):
    alpha = (2 * depth) ** 0.25
    lam_init = 0.8 - 0.6 * math.exp(-0.3 * l)
    w_in = p['w_in'][l]
    daq, dak, dsq, ixq, kv, vk, ikk, dav, memq, ixw = _in_projection(h2d, w_in, seq, tm=min(PROJ_TM, seq))
    lam = (jnp.exp(jnp.sum(p['lq1'][l].astype(F32) * p['lk1'][l].astype(F32)))
           - jnp.exp(jnp.sum(p['lq2'][l].astype(F32) * p['lk2'][l].astype(F32))) + lam_init)
    ya = _diff_attention(daq, dak, dav, lam, p['subln_g'][l], batch, seq, lam_init, tq=min(DA_TQ, seq), tk=min(DA_TK, seq))
    yb = _sparse_attention(ixq, ixw, dsq, ikk, kv, vk, batch, seq, tq=min(DSA_TQ, seq))
    mkv = _matmul(mem2d, p['w_mem_kv'][l].astype(BF16), tm=mem_len, out_dtype=BF16)
    yc = _memory_attention(memq, mkv, batch, seq, mem_len, tq=min(MEM_TQ, seq))
    h1 = _merge_project_norm(h2d, ya, yb, yc, w_in[:, 3272:], p['w_ba'][l], p['w_bb'][l], p['w_bc'][l],
                             p['w_o'][l], p['ln1_g'][l], p['ln1_b'][l], alpha, tm=MERGE_TM)
    return _moe(h1, p, l, alpha)


def kernel(x, mem, w_in, da_lambda_q1, da_lambda_k1, da_lambda_q2, da_lambda_k2, da_subln_g, w_mem_kv, w_branch_a, w_branch_b, w_branch_c, w_out, ln1_g, ln1_b, w_router, router_bias, w_exp_gate, w_exp_up, w_exp_down, w_sh_gate, w_sh_up, w_sh_down, ln2_g, ln2_b):
    batch, seq, d = x.shape
    mem_len = mem.shape[1]
    depth = w_in.shape[0]
    p = dict(w_in=w_in, lq1=da_lambda_q1, lk1=da_lambda_k1, lq2=da_lambda_q2, lk2=da_lambda_k2,
             subln_g=da_subln_g, w_mem_kv=w_mem_kv, w_ba=w_branch_a, w_bb=w_branch_b, w_bc=w_branch_c,
             w_o=w_out, ln1_g=ln1_g, ln1_b=ln1_b, w_router=w_router, router_bias=router_bias,
             w_eg=w_exp_gate, w_eu=w_exp_up, w_ed=w_exp_down, w_sg=w_sh_gate, w_su=w_sh_up,
             w_sd=w_sh_down, ln2_g=ln2_g, ln2_b=ln2_b)
    h = x.reshape(batch * seq, d)
    mem2d = mem.reshape(batch * mem_len, d)
    for l in range(depth):
        h = _layer(h, mem2d, p, l, depth, batch, seq, mem_len)
    return h.reshape(batch, seq, d)
```
